```python
import jax
import jax.numpy as jnp
from jax import lax
import numpy as np


D_MODEL = 2048
BATCH = 4
SEQ = 2048
DEPTH = 4

GRID_W = 64
CTX_LEN = 256
HEAD_DIM = 128
N_HEADS_TOTAL = D_MODEL // HEAD_DIM
B_HEADS = N_HEADS_TOTAL // 4
A_HEADS = (N_HEADS_TOTAL - B_HEADS) // 2
C_HEADS = N_HEADS_TOTAL - B_HEADS - A_HEADS
A_KV_HEADS = 2
A_WINDOW = 128
A_BLOCK = 128
NB_ROWS = 8
NB_COLS = 16
MLSTM_CHUNK = 64
CONV_W = 3
ROPE_BASE = 10000.0
N_EXPERTS = 16
N_GROUPS = 4
TOP_K = 2
EXPERT_FF = 1024
ADA_CHUNKS = 6
EPS = 1e-6
A_Q = A_HEADS * HEAD_DIM
A_KV = A_KV_HEADS * HEAD_DIM
B_W = B_HEADS * HEAD_DIM
B_G = 4 * B_HEADS
C_W = C_HEADS * HEAD_DIM
IN_SPLITS = (A_Q, A_KV, A_KV, B_W, B_W, B_W, B_W, B_G, C_W, C_W, C_W)
D_IN = A_Q + 2 * A_KV + 4 * B_W + B_G + 3 * C_W

kernel_name = 'hybrid_prefix_diffusion_block'


def rmsnorm(z, g):
    zf = z.astype(jnp.float32)
    zf = zf * lax.rsqrt(jnp.mean(zf * zf, axis=-1, keepdims=True) + EPS)
    return (zf * g.astype(jnp.float32)).astype(z.dtype)


def modulate(z, shift, scale):
    return z * (1 + scale) + shift


def split_cols(z):
    parts, start = [], 0
    for width in IN_SPLITS:
        parts.append(z[..., start:start + width])
        start += width
    return parts


def to_heads(z, n_heads):
    return z.reshape(z.shape[0], z.shape[1], n_heads, HEAD_DIM)


def axial_rope_tables(seq_len):
    t = jnp.arange(seq_len, dtype=jnp.int32)
    row = (t // GRID_W).astype(jnp.float32)
    col = (t % GRID_W).astype(jnp.float32)
    n_freq = HEAD_DIM // 4
    inv_freq = ROPE_BASE ** (-jnp.arange(n_freq, dtype=jnp.float32) / n_freq)
    ang_r = row[:, None] * inv_freq[None, :]
    ang_c = col[:, None] * inv_freq[None, :]
    return (jnp.cos(ang_r), jnp.sin(ang_r), jnp.cos(ang_c), jnp.sin(ang_c))


def rotate(z, cos, sin):
    n = z.shape[-1] // 2
    cos = cos[:, None, :]
    sin = sin[:, None, :]
    z1, z2 = z[..., :n], z[..., n:]
    return jnp.concatenate([z1 * cos - z2 * sin, z2 * cos + z1 * sin], axis=-1)


def rope_2d(z, tabs):
    cos_r, sin_r, cos_c, sin_c = tabs
    half = HEAD_DIM // 2
    zf = z.astype(jnp.float32)
    out = jnp.concatenate([rotate(zf[..., :half], cos_r, sin_r),
                           rotate(zf[..., half:], cos_c, sin_c)], axis=-1)
    return out.astype(z.dtype)


def window_gqa(q, k, v, q_ctx, k_ctx, v_ctx, sink, with_ctx_out):
    bsz, seq = q.shape[0], q.shape[1]
    n_ctx = k_ctx.shape[1]
    grp = A_HEADS // A_KV_HEADS
    nb = seq // A_BLOCK
    scale = HEAD_DIM ** -0.5
    qb = q.reshape(bsz, nb, A_BLOCK, A_KV_HEADS, grp, HEAD_DIM)
    pad = ((0, 0), (A_BLOCK, A_BLOCK), (0, 0), (0, 0))
    kp = jnp.pad(k, pad).reshape(bsz, nb + 2, A_BLOCK, A_KV_HEADS, HEAD_DIM)
    vp = jnp.pad(v, pad).reshape(bsz, nb + 2, A_BLOCK, A_KV_HEADS, HEAD_DIM)

    def band(z):
        return jnp.concatenate([z[:, 0:nb], z[:, 1:nb + 1], z[:, 2:nb + 2]], axis=2)

    kw, vw = band(kp), band(vp)
    s_win = jnp.einsum('bnqkgd,bnjkd->bkgnqj', qb, kw) * scale
    s_ctx = jnp.einsum('bnqkgd,bckd->bkgnqc', qb, k_ctx) * scale
    blk = jnp.arange(nb)[:, None, None] * A_BLOCK
    qpos = blk + jnp.arange(A_BLOCK)[None, :, None]
    kpos = blk - A_BLOCK + jnp.arange(3 * A_BLOCK)[None, None, :]
    valid = (jnp.abs(qpos - kpos) <= A_WINDOW) & (kpos >= 0) & (kpos < seq)
    sink_f = sink.astype(jnp.float32)
    sink_l = jnp.broadcast_to(sink_f.reshape(1, A_KV_HEADS, grp, 1, 1, 1), s_ctx.shape[:-1] + (1,))
    logits = jnp.concatenate([sink_l, s_ctx.astype(jnp.float32),
                              jnp.where(valid, s_win.astype(jnp.float32), -jnp.inf)], axis=-1)
    p = jax.nn.softmax(logits, axis=-1).astype(v.dtype)
    o = (jnp.einsum('bkgnqc,bckd->bnqkgd', p[..., 1:1 + n_ctx], v_ctx)
         + jnp.einsum('bkgnqj,bnjkd->bnqkgd', p[..., 1 + n_ctx:], vw))
    out = o.reshape(bsz, seq, A_Q)
    out_ctx = None
    if with_ctx_out:
        qc = q_ctx.reshape(bsz, n_ctx, A_KV_HEADS, grp, HEAD_DIM)
        sc = jnp.einsum('bqkgd,bckd->bkgqc', qc, k_ctx) * scale
        sink_c = jnp.broadcast_to(sink_f.reshape(1, A_KV_HEADS, grp, 1, 1), sc.shape[:-1] + (1,))
        pc = jax.nn.softmax(jnp.concatenate([sink_c, sc.astype(jnp.float32)], axis=-1), axis=-1)
        pc = pc[..., 1:].astype(v.dtype)
        out_ctx = jnp.einsum('bkgqc,bckd->bqkgd', pc, v_ctx).reshape(bsz, n_ctx, A_Q)
    return out, out_ctx


def neighbourhood_attn(q, k, v, q_ctx, k_ctx, v_ctx, rpb, with_ctx_out):
    bsz, seq = q.shape[0], q.shape[1]
    n_ctx = k_ctx.shape[1]
    rows = seq // GRID_W
    kr = min(NB_ROWS, rows)
    kc = min(NB_COLS, GRID_W)
    scale = HEAD_DIM ** -0.5
    qg = q.reshape(bsz, rows, GRID_W, C_HEADS, HEAD_DIM)
    kg_full = k.reshape(bsz, rows, GRID_W, C_HEADS, HEAD_DIM)
    vg_full = v.reshape(bsz, rows, GRID_W, C_HEADS, HEAD_DIM)
    r = jnp.arange(rows)
    start_r = jnp.clip(r - kr // 2, 0, rows - kr)
    rows_idx = start_r[:, None] + jnp.arange(kr)[None, :]
    kg = kg_full[:, rows_idx]
    vg = vg_full[:, rows_idx]
    s = jnp.einsum('brqhd,brkwhd->bhrqkw', qg, kg) * scale
    cols = jnp.arange(GRID_W)
    start_c = jnp.clip(cols - kc // 2, 0, GRID_W - kc)
    col_ok = (cols[None, :] >= start_c[:, None]) & (cols[None, :] < start_c[:, None] + kc)
    dr_idx = rows_idx - r[:, None] + NB_ROWS - 1
    dc_idx = jnp.clip(cols[None, :] - cols[:, None] + NB_COLS - 1, 0, 2 * NB_COLS - 2)
    bias = rpb[:, dr_idx[:, None, :, None], dc_idx[None, :, None, :]]
    s = s.astype(jnp.float32) + bias.astype(jnp.float32)[None]
    s = jnp.where(col_ok[:, None, :], s, -jnp.inf)
    s = s.reshape(bsz, C_HEADS, rows, GRID_W, kr * GRID_W)
    s_ctx = jnp.einsum('brqhd,bchd->bhrqc', qg, k_ctx) * scale
    p = jax.nn.softmax(jnp.concatenate([s_ctx.astype(jnp.float32), s], axis=-1), axis=-1).astype(v.dtype)
    p_nb = p[..., n_ctx:].reshape(bsz, C_HEADS, rows, GRID_W, kr, GRID_W)
    o = (jnp.einsum('bhrqc,bchd->brqhd', p[..., :n_ctx], v_ctx)
         + jnp.einsum('bhrqkw,brkwhd->brqhd', p_nb, vg))
    out = o.reshape(bsz, seq, C_W)
    out_ctx = None
    if with_ctx_out:
        sc = jnp.einsum('bqhd,bchd->bhqc', q_ctx, k_ctx) * scale
        pc = jax.nn.softmax(sc.astype(jnp.float32), axis=-1).astype(v.dtype)
        out_ctx = jnp.einsum('bhqc,bchd->bqhd', pc, v_ctx).reshape(bsz, n_ctx, C_W)
    return out, out_ctx


def short_conv(z, w):
    n = z.shape[1]
    pad = CONV_W // 2
    zp = jnp.pad(z, ((0, 0), (pad, pad), (0, 0)))
    out = zp[:, 0:n] * w[0]
    for j in range(1, CONV_W):
        out = out + zp[:, j:j + n] * w[j]
    return out


def mlstm_chunk_scan(q, k, v, ig, logf, state):
    bsz, nh, n_tok, hd = q.shape
    nc = n_tok // MLSTM_CHUNK

    def chunks(z):
        return jnp.moveaxis(z.reshape(z.shape[:2] + (nc, MLSTM_CHUNK) + z.shape[3:]), 2, 0)

    lower = jnp.tril(jnp.ones((MLSTM_CHUNK, MLSTM_CHUNK), dtype=bool))

    def step(carry, inp):
        c_mat, n_vec, m_prev = carry
        qc, kc, vc, ic, fc = inp
        b = jnp.cumsum(fc, axis=-1)
        log_d = jnp.where(lower, b[..., :, None] - b[..., None, :] + ic[..., None, :], -jnp.inf)
        log_inter = b + m_prev[..., None]
        m_t = jnp.maximum(log_inter, jnp.max(log_d, axis=-1))
        w_intra = jnp.exp(log_d - m_t[..., None])
        w_inter = jnp.exp(log_inter - m_t)
        s = jnp.einsum('bhtd,bhsd->bhts', qc, kc) * w_intra
        num = (jnp.einsum('bhts,bhsv->bhtv', s, vc)
               + w_inter[..., None] * jnp.einsum('bhtk,bhkv->bhtv', qc, c_mat))
        den = jnp.sum(s, axis=-1) + w_inter * jnp.einsum('bhtk,bhk->bht', qc, n_vec)
        h = num / jnp.maximum(jnp.abs(den), jnp.exp(-m_t))[..., None]
        b_end = b[..., -1]
        log_w = b_end[..., None] - b + ic
        m_new = jnp.maximum(b_end + m_prev, jnp.max(log_w, axis=-1))
        decay = jnp.exp(b_end + m_prev - m_new)
        w = jnp.exp(log_w - m_new[..., None])
        c_new = decay[..., None, None] * c_mat + jnp.einsum('bhs,bhsk,bhsv->bhkv', w, kc, vc)
        n_new = decay[..., None] * n_vec + jnp.einsum('bhs,bhsk->bhk', w, kc)
        return (c_new, n_new, m_new), h

    state, hs = lax.scan(step, state, (chunks(q), chunks(k), chunks(v), chunks(ig), chunks(logf)))
    h = jnp.moveaxis(hs, 0, 2).reshape(bsz, nh, n_tok, hd)
    return h, state


def mlstm_mixer(q, k, v, o, g, q_c, k_c, v_c, o_c, g_c, conv_w, b_gates, with_ctx_out):
    def prep(qq, kk, vv, gg):
        bsz, n_tok = qq.shape[0], qq.shape[1]
        qq = jax.nn.silu(short_conv(qq, conv_w[:, :B_W]))
        kk = jax.nn.silu(short_conv(kk, conv_w[:, B_W:])) * (HEAD_DIM ** -0.5)

        def heads(z):
            return z.astype(jnp.float32).reshape(bsz, n_tok, B_HEADS, HEAD_DIM).transpose(0, 2, 1, 3)

        gg = (gg + b_gates).astype(jnp.float32).reshape(bsz, n_tok, 4, B_HEADS).transpose(2, 0, 3, 1)
        return (heads(qq), heads(kk), heads(vv), gg[0], jax.nn.log_sigmoid(gg[1]),
                gg[2], jax.nn.log_sigmoid(gg[3]))

    ql, kl, vl, il_fw, fl_fw, il_bw, fl_bw = prep(q, k, v, g)
    qc, kc, vc, ic_fw, fc_fw, ic_bw, fc_bw = prep(q_c, k_c, v_c, g_c)
    bsz = q.shape[0]
    zero = (jnp.zeros((bsz, B_HEADS, HEAD_DIM, HEAD_DIM), jnp.float32),
            jnp.zeros((bsz, B_HEADS, HEAD_DIM), jnp.float32),
            jnp.zeros((bsz, B_HEADS), jnp.float32))

    def rev(z):
        return jnp.flip(z, axis=2)

    hc_fw, st_fw = mlstm_chunk_scan(qc, kc, vc, ic_fw, fc_fw, zero)
    h_fw, _ = mlstm_chunk_scan(ql, kl, vl, il_fw, fl_fw, st_fw)
    hc_bw, st_bw = mlstm_chunk_scan(rev(qc), rev(kc), rev(vc), rev(ic_bw), rev(fc_bw), zero)
    h_bw, _ = mlstm_chunk_scan(rev(ql), rev(kl), rev(vl), rev(il_bw), rev(fl_bw), st_bw)

    def merge(hf, hb, oo):
        hsum = (hf + rev(hb)).transpose(0, 2, 1, 3).reshape(oo.shape)
        return (jax.nn.sigmoid(oo.astype(jnp.float32)) * hsum).astype(oo.dtype)

    y = merge(h_fw, h_bw, o)
    y_c = merge(hc_fw, hc_bw, o_c) if with_ctx_out else None
    return y, y_c


def hybrid_mixer(h, hc, w_in, b_gates, conv_qk, sink, rpb, tabs, with_ctx_out):
    aq, ak, av, bq, bk, bv, bo, bg, cq, ck, cv = split_cols(h @ w_in)
    aq_c, ak_c, av_c, bq_c, bk_c, bv_c, bo_c, bg_c, cq_c, ck_c, cv_c = split_cols(hc @ w_in)
    y_a, y_a_c = window_gqa(rope_2d(to_heads(aq, A_HEADS), tabs), rope_2d(to_heads(ak, A_KV_HEADS), tabs),
                            to_heads(av, A_KV_HEADS), to_heads(aq_c, A_HEADS), to_heads(ak_c, A_KV_HEADS),
                            to_heads(av_c, A_KV_HEADS), sink, with_ctx_out)
    y_b, y_b_c = mlstm_mixer(bq, bk, bv, bo, bg, bq_c, bk_c, bv_c, bo_c, bg_c, conv_qk, b_gates, with_ctx_out)
    y_n, y_n_c = neighbourhood_attn(to_heads(cq, C_HEADS), to_heads(ck, C_HEADS), to_heads(cv, C_HEADS),
                                    to_heads(cq_c, C_HEADS), to_heads(ck_c, C_HEADS), to_heads(cv_c, C_HEADS),
                                    rpb, with_ctx_out)
    y = jnp.concatenate([y_a, y_b, y_n], axis=-1)
    y_ctx = jnp.concatenate([y_a_c, y_b_c, y_n_c], axis=-1) if with_ctx_out else None
    return y, y_ctx


def moe_ffn(h, w_router, b_router, w1, w3, w2):
    n_tok = h.shape[0]
    per_group = N_EXPERTS // N_GROUPS
    scores = jax.nn.sigmoid((h @ w_router).astype(jnp.float32))
    biased = scores + b_router.astype(jnp.float32)
    group_score = jnp.sum(lax.top_k(biased.reshape(n_tok, N_GROUPS, per_group), TOP_K)[0], axis=-1)
    sel_group = jnp.argmax(group_score, axis=-1)
    in_group = (jnp.arange(N_EXPERTS) // per_group)[None, :] == sel_group[:, None]
    _, idx = lax.top_k(jnp.where(in_group, biased, -jnp.inf), TOP_K)
    w_sel = jnp.take_along_axis(scores, idx, axis=-1)
    w_sel = w_sel / jnp.sum(w_sel, axis=-1, keepdims=True)
    combine = jnp.sum(jax.nn.one_hot(idx, N_EXPERTS, dtype=jnp.float32) * w_sel[..., None], axis=1)
    out = jnp.zeros(h.shape, jnp.float32)
    for e in range(N_EXPERTS):
        act = jax.nn.silu(h @ w1[e]) * (h @ w3[e])
        out = out + combine[:, e:e + 1] * (act @ w2[e]).astype(jnp.float32)
    return out.astype(h.dtype)


def setup_inputs(seed: int = 0) -> dict:
    key = jax.random.key(seed)
    ks = jax.random.split(key, 20)
    f32 = jnp.float32

    def nrm(k, shape, scale):
        return jax.random.normal(k, shape, f32) * scale

    f_bias = jnp.linspace(3.0, 6.0, B_HEADS, dtype=f32)
    zb = jnp.zeros((B_HEADS,), f32)
    gate_base = jnp.stack([zb, f_bias, zb, f_bias])
    b_gates = (gate_base[None] + nrm(ks[9], (DEPTH, 4, B_HEADS), 0.1)).reshape(DEPTH, B_G)
    return {
        'x': nrm(ks[0], (BATCH, SEQ, D_MODEL), 1.0),
        'c': nrm(ks[1], (BATCH, D_MODEL), 1.0),
        'ctx': nrm(ks[2], (BATCH, CTX_LEN, D_MODEL), 1.0),
        'c_ctx': nrm(ks[3], (D_MODEL,), 1.0),
        'w_ada': nrm(ks[4], (DEPTH, D_MODEL, ADA_CHUNKS * D_MODEL), 0.5 * D_MODEL ** -0.5),
        'b_ada': nrm(ks[5], (DEPTH, ADA_CHUNKS * D_MODEL), 0.02),
        'norm_mix': 1.0 + nrm(ks[6], (DEPTH, D_MODEL), 0.05),
        'norm_ffn': 1.0 + nrm(ks[7], (DEPTH, D_MODEL), 0.05),
        'w_in': nrm(ks[8], (DEPTH, D_MODEL, D_IN), D_MODEL ** -0.5),
        'b_gates': b_gates,
        'conv_qk': nrm(ks[10], (DEPTH, CONV_W, 2 * B_W), CONV_W ** -0.5),
        'sink': nrm(ks[11], (DEPTH, A_HEADS), 1.0),
        'rpb': nrm(ks[12], (DEPTH, C_HEADS, 2 * NB_ROWS - 1, 2 * NB_COLS - 1), 0.5),
        'w_out': nrm(ks[13], (DEPTH, D_MODEL, D_MODEL), D_MODEL ** -0.5),
        'w_router': nrm(ks[14], (D_MODEL, N_EXPERTS), D_MODEL ** -0.5),
        'b_router': nrm(ks[15], (N_EXPERTS,), 0.01),
        'w1': nrm(ks[16], (DEPTH, N_EXPERTS, D_MODEL, EXPERT_FF), D_MODEL ** -0.5),
        'w3': nrm(ks[17], (DEPTH, N_EXPERTS, D_MODEL, EXPERT_FF), D_MODEL ** -0.5),
        'w2': nrm(ks[18], (DEPTH, N_EXPERTS, EXPERT_FF, D_MODEL), EXPERT_FF ** -0.5),
        'norm_final': 1.0 + nrm(ks[19], (D_MODEL,), 0.05),
    }


def reference(x, c, ctx, c_ctx, w_ada, b_ada, norm_mix, norm_ffn, w_in, b_gates, conv_qk, sink, rpb,
              w_out, w_router, b_router, w1, w3, w2, norm_final):
    bsz, seq, dm = x.shape
    n_ctx = ctx.shape[1]
    tabs = axial_rope_tables(seq)
    silu_c = jax.nn.silu(c)
    silu_cc = jax.nn.silu(c_ctx)
    xc = ctx
    for l in range(DEPTH):
        with_ctx_out = l < DEPTH - 1
        mod = (silu_c @ w_ada[l] + b_ada[l]).reshape(bsz, 1, ADA_CHUNKS, dm)
        mod_c = (silu_cc @ w_ada[l] + b_ada[l]).reshape(ADA_CHUNKS, dm)
        h = modulate(rmsnorm(x, norm_mix[l]), mod[:, :, 0], mod[:, :, 1])
        hc = modulate(rmsnorm(xc, norm_mix[l]), mod_c[0], mod_c[1])
        y, y_ctx = hybrid_mixer(h, hc, w_in[l], b_gates[l], conv_qk[l], sink[l], rpb[l], tabs, with_ctx_out)
        x = x + mod[:, :, 2] * (y @ w_out[l])
        h2 = modulate(rmsnorm(x, norm_ffn[l]), mod[:, :, 3], mod[:, :, 4])
        if with_ctx_out:
            xc = xc + mod_c[2] * (y_ctx @ w_out[l])
            h2c = modulate(rmsnorm(xc, norm_ffn[l]), mod_c[3], mod_c[4])
            tokens = jnp.concatenate([h2.reshape(-1, dm), h2c.reshape(-1, dm)], axis=0)
            f_all = moe_ffn(tokens, w_router, b_router, w1[l], w3[l], w2[l])
            x = x + mod[:, :, 5] * f_all[:bsz * seq].reshape(bsz, seq, dm)
            xc = xc + mod_c[5] * f_all[bsz * seq:].reshape(bsz, n_ctx, dm)
        else:
            f_lat = moe_ffn(h2.reshape(-1, dm), w_router, b_router, w1[l], w3[l], w2[l])
            x = x + mod[:, :, 5] * f_lat.reshape(bsz, seq, dm)
    return rmsnorm(x, norm_final)
```

```python
import functools

import jax
import jax.numpy as jnp
from jax import lax
from jax.experimental import pallas as pl
from jax.experimental.pallas import tpu as pltpu

F32 = jnp.float32
BF16 = jnp.bfloat16
I32 = jnp.int32

D = 2048
BATCH = 4
SEQ = 2048
CTX = 256
SEG = CTX + SEQ
T = BATCH * SEG
DEPTH = 4
GRID_W = 64
HD = 128
A_HEADS, A_KV, A_GRP = 6, 2, 3
B_HEADS = 4
C_HEADS = 6
A_BLOCK = 128
NB_ROWS, NB_COLS = 8, 16
CHUNK = 256
N_EXPERTS, N_GROUPS, PER_GROUP = 16, 4, 4
FF = 1024
EPS = 1e-6
ROPE_BASE = 10000.0
NEG = -1e30
ATT_SCALE = HD ** -0.5

N_MAIN = 5632
COL_AQ, COL_AK, COL_AV = 0, 6, 8
COL_BQ, COL_BK, COL_BV, COL_BO = 10, 14, 18, 22
COL_CQ, COL_CK, COL_CV = 26, 32, 38

ETILE = 256
N_ETILES = (2 * T) // ETILE + N_EXPERTS
P_ROWS = N_ETILES * ETILE

VMEM_LIMIT = 56 * 1024 * 1024


def _cparams(sem):
    return pltpu.CompilerParams(dimension_semantics=sem, vmem_limit_bytes=VMEM_LIMIT)


def _silu(v):
    return v * jax.nn.sigmoid(v)


def _log_sigmoid(v):
    return jnp.minimum(v, 0.0) - jnp.log1p(jnp.exp(-jnp.abs(v)))


ADA_TN = 1024


def _ada_kernel(s_ref, w_ref, b_ref, o_ref):
    s = _silu(s_ref[...]).astype(BF16)
    o_ref[0] = jnp.dot(s, w_ref[0].astype(BF16), preferred_element_type=F32) + b_ref[0]


def ada_all(cond8, w_ada, b_ada):
    n = w_ada.shape[-1]
    return pl.pallas_call(
        _ada_kernel,
        grid=(DEPTH, n // ADA_TN),
        in_specs=[pl.BlockSpec((8, D), lambda l, j: (0, 0)),
                  pl.BlockSpec((1, D, ADA_TN), lambda l, j: (l, 0, j)),
                  pl.BlockSpec((1, 1, ADA_TN), lambda l, j: (l, 0, j))],
        out_specs=pl.BlockSpec((1, 8, ADA_TN), lambda l, j: (l, 0, j)),
        out_shape=jax.ShapeDtypeStruct((DEPTH, 8, n), F32),
        name="ada_mod",
        compiler_params=_cparams(("arbitrary", "arbitrary")),
    )(cond8, w_ada, b_ada.reshape(DEPTH, 1, n))


def _row_mod(mod_ref, chunk, batch, is_ctx):
    lat = mod_ref[pl.ds(batch, 1), chunk * D:(chunk + 1) * D]
    ctx = mod_ref[4:5, chunk * D:(chunk + 1) * D]
    return jnp.where(is_ctx, ctx, lat)


def _rms(x):
    return x * lax.rsqrt(jnp.mean(x * x, axis=-1, keepdims=True) + EPS)


IN_TM = SEG // 2
IN_TN = 1408


def _in_kernel(x_ref, g_ref, mod_ref, w_ref, wg_ref, z_ref, zg_ref, h_scr):
    i = pl.program_id(0)
    j = pl.program_id(1)

    @pl.when(j == 0)
    def _():
        batch = i // 2
        rows = lax.broadcasted_iota(I32, (IN_TM, 1), 0) + (i % 2) * IN_TM
        is_ctx = rows < CTX
        xn = _rms(x_ref[...]) * g_ref[...]
        h = xn * (1.0 + _row_mod(mod_ref, 1, batch, is_ctx)) + _row_mod(mod_ref, 0, batch, is_ctx)
        hb = h.astype(BF16)
        h_scr[...] = hb
        zg_ref[...] = jnp.dot(hb, wg_ref[...], preferred_element_type=F32)

    z_ref[...] = jnp.dot(h_scr[...], w_ref[...], preferred_element_type=F32).astype(BF16)


def in_proj(xa, g, mod_l, w_main, w_gate):
    return pl.pallas_call(
        _in_kernel,
        grid=(T // IN_TM, N_MAIN // IN_TN),
        in_specs=[pl.BlockSpec((IN_TM, D), lambda i, j: (i, 0)),
                  pl.BlockSpec((1, D), lambda i, j: (0, 0)),
                  pl.BlockSpec((8, 6 * D), lambda i, j: (0, 0)),
                  pl.BlockSpec((D, IN_TN), lambda i, j: (0, j)),
                  pl.BlockSpec((D, HD), lambda i, j: (0, 0))],
        out_specs=[pl.BlockSpec((IN_TM, IN_TN), lambda i, j: (i, j)),
                   pl.BlockSpec((IN_TM, HD), lambda i, j: (i, 0))],
        out_shape=[jax.ShapeDtypeStruct((T, N_MAIN), BF16),
                   jax.ShapeDtypeStruct((T, HD), F32)],
        scratch_shapes=[pltpu.VMEM((IN_TM, D), BF16)],
        name="in_proj",
        compiler_params=_cparams(("arbitrary", "arbitrary")),
    )(xa, g, mod_l, w_main, w_gate)


NBLK = SEQ // A_BLOCK
KPAD = SEG + A_BLOCK


def _dot_nt(a, b):
    return lax.dot_general(a, b, (((1,), (1,)), ((), ())), preferred_element_type=F32)


def _softmax_parts(parts, extra=None):
    m = parts[0].max(axis=-1, keepdims=True)
    for p in parts[1:]:
        m = jnp.maximum(m, p.max(axis=-1, keepdims=True))
    if extra is not None:
        m = jnp.maximum(m, extra)
    es = [jnp.exp(p - m) for p in parts]
    den = es[0].sum(axis=-1, keepdims=True)
    for e in es[1:]:
        den = den + e.sum(axis=-1, keepdims=True)
    if extra is not None:
        den = den + jnp.exp(extra - m)
    return [e / den for e in es]


def _win_kernel(sink_ref, q_ref, k_ref, v_ref, cos_ref, sin_ref, o_ref, qs, ks, vs):
    kv = pl.program_id(1)
    lane = lax.broadcasted_iota(I32, (A_BLOCK, HD), 1)
    first_half = (lane % 64) < 32

    def rope(zf, cos, sin):
        zr = jnp.where(first_half, pltpu.roll(zf, 96, 1), pltpu.roll(zf, 32, 1))
        return zf * cos + zr * sin

    ks[0:CTX, :] = k_ref[0:CTX, :]
    ks[SEG:KPAD, :] = jnp.zeros((A_BLOCK, HD), BF16)
    vs[0:SEG, :] = v_ref[...]
    vs[SEG:KPAD, :] = jnp.zeros((A_BLOCK, HD), BF16)

    def rope_blk(i, c):
        r0 = pl.multiple_of(i * A_BLOCK, A_BLOCK)
        cos = cos_ref[pl.ds(r0, A_BLOCK), :]
        sin = sin_ref[pl.ds(r0, A_BLOCK), :]
        ks[pl.ds(CTX + r0, A_BLOCK), :] = rope(k_ref[pl.ds(CTX + r0, A_BLOCK), :].astype(F32), cos, sin).astype(BF16)
        for g in range(A_GRP):
            zf = q_ref[pl.ds(CTX + r0, A_BLOCK), g * HD:(g + 1) * HD].astype(F32)
            qs[i, g * A_BLOCK:(g + 1) * A_BLOCK, :] = rope(zf, cos, sin).astype(BF16)
        return c

    lax.fori_loop(0, NBLK, rope_blk, 0)

    nq = A_GRP * A_BLOCK
    row = lax.broadcasted_iota(I32, (nq, 1), 0)
    sink = jnp.where(row < A_BLOCK, sink_ref[kv * A_GRP],
                     jnp.where(row < 2 * A_BLOCK, sink_ref[kv * A_GRP + 1], sink_ref[kv * A_GRP + 2]))
    r = lax.broadcasted_iota(I32, (nq, 3 * A_BLOCK), 0) % A_BLOCK
    c = lax.broadcasted_iota(I32, (nq, 3 * A_BLOCK), 1)
    band = (c >= r) & (c <= r + 2 * A_BLOCK)

    def blk(i, carry):
        w0 = pl.multiple_of(CTX - A_BLOCK + i * A_BLOCK, A_BLOCK)
        q = qs[i]
        s_c = _dot_nt(q, ks[0:CTX, :]) * ATT_SCALE
        s_w = _dot_nt(q, ks[pl.ds(w0, 3 * A_BLOCK), :]) * ATT_SCALE
        kpos = (i - 1) * A_BLOCK + c
        s_w = jnp.where(band & (kpos >= 0) & (kpos < SEQ), s_w, NEG)
        p_c, p_w = _softmax_parts([s_c, s_w], sink)
        o = (jnp.dot(p_c.astype(BF16), vs[0:CTX, :], preferred_element_type=F32)
             + jnp.dot(p_w.astype(BF16), vs[pl.ds(w0, 3 * A_BLOCK), :], preferred_element_type=F32))
        o0 = pl.multiple_of(CTX + i * A_BLOCK, A_BLOCK)
        for g in range(A_GRP):
            o_ref[pl.ds(o0, A_BLOCK), g * HD:(g + 1) * HD] = o[g * A_BLOCK:(g + 1) * A_BLOCK].astype(BF16)
        return carry

    lax.fori_loop(0, NBLK, blk, 0)

    for g in range(A_GRP):
        s = _dot_nt(q_ref[0:CTX, g * HD:(g + 1) * HD], k_ref[0:CTX, :]) * ATT_SCALE
        (p,) = _softmax_parts([s], jnp.full((CTX, 1), sink_ref[kv * A_GRP + g], F32))
        o_ref[0:CTX, g * HD:(g + 1) * HD] = jnp.dot(
            p.astype(BF16), v_ref[0:CTX, :], preferred_element_type=F32).astype(BF16)


def window_attn(z, sink_l, cos_t, sin_t):
    return pl.pallas_call(
        _win_kernel,
        grid=(BATCH, A_KV),
        in_specs=[pl.BlockSpec(memory_space=pltpu.SMEM),
                  pl.BlockSpec((SEG, A_GRP * HD), lambda b, kv: (b, kv)),
                  pl.BlockSpec((SEG, HD), lambda b, kv: (b, COL_AK + kv)),
                  pl.BlockSpec((SEG, HD), lambda b, kv: (b, COL_AV + kv)),
                  pl.BlockSpec((SEQ, HD), lambda b, kv: (0, 0)),
                  pl.BlockSpec((SEQ, HD), lambda b, kv: (0, 0))],
        out_specs=pl.BlockSpec((SEG, A_GRP * HD), lambda b, kv: (b, kv)),
        out_shape=jax.ShapeDtypeStruct((T, A_HEADS * HD), BF16),
        scratch_shapes=[pltpu.VMEM((NBLK, A_GRP * A_BLOCK, HD), BF16),
                        pltpu.VMEM((KPAD, HD), BF16),
                        pltpu.VMEM((KPAD, HD), BF16)],
        name="window_attn",
        compiler_params=_cparams(("arbitrary", "arbitrary")),
    )(sink_l, z, z, z, cos_t, sin_t)


GRID_ROWS = SEQ // GRID_W
NB_KEYS = NB_ROWS * GRID_W


def _nbr_kernel(q_ref, k_ref, v_ref, tab_ref, o_ref):
    def row(r, carry):
        start = jnp.clip(r - NB_ROWS // 2, 0, GRID_ROWS - NB_ROWS)
        dr0 = start - r + NB_ROWS - 1
        q0 = pl.multiple_of(CTX + r * GRID_W, GRID_W)
        k0 = pl.multiple_of(CTX + start * GRID_W, GRID_W)
        q = q_ref[pl.ds(q0, GRID_W), :]
        s_c = _dot_nt(q, k_ref[0:CTX, :]) * ATT_SCALE
        s_n = _dot_nt(q, k_ref[pl.ds(k0, NB_KEYS), :]) * ATT_SCALE
        bias = jnp.concatenate([tab_ref[0, dr0 + 2 * p] for p in range(NB_ROWS // 2)], axis=1)
        p_c, p_n = _softmax_parts([s_c, s_n + bias])
        o = (jnp.dot(p_c.astype(BF16), v_ref[0:CTX, :], preferred_element_type=F32)
             + jnp.dot(p_n.astype(BF16), v_ref[pl.ds(k0, NB_KEYS), :], preferred_element_type=F32))
        o_ref[pl.ds(q0, GRID_W), :] = o.astype(BF16)
        return carry

    lax.fori_loop(0, GRID_ROWS, row, 0)

    s = _dot_nt(q_ref[0:CTX, :], k_ref[0:CTX, :]) * ATT_SCALE
    (p,) = _softmax_parts([s])
    o_ref[0:CTX, :] = jnp.dot(p.astype(BF16), v_ref[0:CTX, :], preferred_element_type=F32).astype(BF16)


def nbr_attn(z, tab_l):
    return pl.pallas_call(
        _nbr_kernel,
        grid=(BATCH, C_HEADS),
        in_specs=[pl.BlockSpec((SEG, HD), lambda b, h: (b, COL_CQ + h)),
                  pl.BlockSpec((SEG, HD), lambda b, h: (b, COL_CK + h)),
                  pl.BlockSpec((SEG, HD), lambda b, h: (b, COL_CV + h)),
                  pl.BlockSpec((1, 2 * NB_ROWS - 2, GRID_W, 2 * GRID_W), lambda b, h: (h, 0, 0, 0))],
        out_specs=pl.BlockSpec((SEG, HD), lambda b, h: (b, h)),
        out_shape=jax.ShapeDtypeStruct((T, C_HEADS * HD), BF16),
        name="nbr_attn",
        compiler_params=_cparams(("arbitrary", "arbitrary")),
    )(z, z, z, tab_l)


def nbr_bias_table(rpb_l):
    cols = jnp.arange(GRID_W)
    start_c = jnp.clip(cols - NB_COLS // 2, 0, GRID_W - NB_COLS)
    col_ok = (cols[None, :] >= start_c[:, None]) & (cols[None, :] < start_c[:, None] + NB_COLS)
    dc = jnp.clip(cols[None, :] - cols[:, None] + NB_COLS - 1, 0, 2 * NB_COLS - 2)
    tab = jnp.where(col_ok[None, None], rpb_l[:, :, dc].astype(F32), NEG)
    return jnp.concatenate([tab[:, :-1], tab[:, 1:]], axis=-1)


N_LCHUNK = SEQ // CHUNK
PADR = 8
CONV_ROWS = PADR + CTX + PADR + SEQ + PADR


def _conv_off(r0):
    return jnp.where(r0 < CTX, r0 + PADR, r0 + 2 * PADR)


def _mlstm_kernel(bg_ref, q_ref, k_ref, v_ref, og_ref, gc_ref, gr_ref, cwq_ref, cwk_ref, y_ref,
                  qs, ks, stage, hf, hb):
    h = pl.program_id(1)
    rowi = lax.broadcasted_iota(I32, (CHUNK, 1), 0)

    def conv_silu(src_ref, cw_ref, dst, post):
        stage[...] = jnp.zeros((CONV_ROWS, HD), F32)
        stage[PADR:PADR + CTX, :] = src_ref[0:CTX, :].astype(F32)
        stage[2 * PADR + CTX:2 * PADR + SEG, :] = src_ref[CTX:SEG, :].astype(F32)

        def blk(c, carry):
            r0 = pl.multiple_of(c * CHUNK, CHUNK)
            s0 = pl.multiple_of(_conv_off(r0), PADR)
            cur = stage[pl.ds(s0, CHUNK), :]
            prev_last = stage[pl.ds(s0 - PADR, PADR), :][PADR - 1:PADR, :]
            next_first = stage[pl.ds(s0 + CHUNK, PADR), :][0:1, :]
            zm = jnp.where(rowi == 0, prev_last, pltpu.roll(cur, 1, 0))
            zp = jnp.where(rowi == CHUNK - 1, next_first, pltpu.roll(cur, CHUNK - 1, 0))
            y = zm * cw_ref[0:1, :] + cur * cw_ref[1:2, :] + zp * cw_ref[2:3, :]
            dst[pl.ds(r0, CHUNK), :] = (_silu(y) * post).astype(BF16)
            return carry

        lax.fori_loop(0, SEG // CHUNK, blk, 0)

    conv_silu(q_ref, cwq_ref, qs, 1.0)
    conv_silu(k_ref, cwk_ref, ks, ATT_SCALE)

    t_idx = lax.broadcasted_iota(I32, (CHUNK, CHUNK), 0)
    s_idx = lax.broadcasted_iota(I32, (CHUNK, CHUNK), 1)

    def chunk(r0, bwd, state, dst):
        c_mat, n_vec, m_prev = state
        ki, kf = (2, 3) if bwd else (0, 1)
        bi = bg_ref[ki * B_HEADS + h]
        bf = bg_ref[kf * B_HEADS + h]
        qc = qs[pl.ds(r0, CHUNK), :]
        kc = ks[pl.ds(r0, CHUNK), :]
        vc = v_ref[pl.ds(r0, CHUNK), :]
        i_row = gr_ref[pl.ds(ki * B_HEADS + h, 1), pl.ds(r0, CHUNK)] + bi
        lf_row = _log_sigmoid(gr_ref[pl.ds(kf * B_HEADS + h, 1), pl.ds(r0, CHUNK)] + bf)
        i_col = gc_ref[0, pl.ds(r0, CHUNK), ki:ki + 1] + bi
        lf_col = _log_sigmoid(gc_ref[0, pl.ds(r0, CHUNK), kf:kf + 1] + bf)
        causal = (s_idx >= t_idx) if bwd else (s_idx <= t_idx)
        causal_t = (t_idx >= s_idx) if bwd else (t_idx <= s_idx)
        b_col = jnp.sum(jnp.where(causal, lf_row, 0.0), axis=1, keepdims=True)
        b_row = jnp.sum(jnp.where(causal_t, lf_col, 0.0), axis=0, keepdims=True)
        a_row = i_row - b_row
        a_col = i_col - b_col
        mx = jnp.maximum(m_prev, jnp.max(jnp.where(causal, a_row, NEG), axis=1, keepdims=True))
        dm = jnp.exp(jnp.where(causal, a_row - mx, NEG))
        s = _dot_nt(qc, kc) * dm
        w_int = jnp.exp(m_prev - mx)
        num = (jnp.dot(s.astype(BF16), vc, preferred_element_type=F32)
               + w_int * jnp.dot(qc, c_mat.astype(BF16), preferred_element_type=F32))
        den = (jnp.sum(s, axis=1, keepdims=True)
               + w_int * jnp.sum(qc.astype(F32) * n_vec, axis=1, keepdims=True))
        m_t = b_col + mx
        dst[pl.ds(r0, CHUNK), :] = num / jnp.maximum(jnp.abs(den), jnp.exp(-m_t))
        b_end = jnp.sum(lf_row, axis=1, keepdims=True)
        m_end = jnp.maximum(m_prev, jnp.max(a_row, axis=1, keepdims=True))
        decay = jnp.exp(m_prev - m_end)
        w_col = jnp.exp(a_col - m_end)
        kv = lax.dot_general(kc, (w_col * vc.astype(F32)).astype(BF16), (((0,), (0,)), ((), ())),
                             preferred_element_type=F32)
        c_new = decay * c_mat + kv
        n_new = decay * n_vec + jnp.sum(w_col * kc.astype(F32), axis=0, keepdims=True)
        return c_new, n_new, b_end + m_end

    zero = (jnp.zeros((HD, HD), F32), jnp.zeros((1, HD), F32), jnp.zeros((1, 1), F32))
    st_f = chunk(0, False, zero, hf)
    st_b = chunk(0, True, zero, hb)

    def body(j, carry):
        sf, sb = carry
        rf = pl.multiple_of(CTX + j * CHUNK, CHUNK)
        rb = pl.multiple_of(CTX + (N_LCHUNK - 1 - j) * CHUNK, CHUNK)
        return chunk(rf, False, sf, hf), chunk(rb, True, sb, hb)

    lax.fori_loop(0, N_LCHUNK, body, (st_f, st_b))

    def fin(c, carry):
        r0 = pl.multiple_of(c * CHUNK, CHUNK)
        gate = jax.nn.sigmoid(og_ref[pl.ds(r0, CHUNK), :].astype(F32))
        y_ref[pl.ds(r0, CHUNK), :] = (gate * (hf[pl.ds(r0, CHUNK), :] + hb[pl.ds(r0, CHUNK), :])).astype(BF16)
        return carry

    lax.fori_loop(0, SEG // CHUNK, fin, 0)


def mlstm(z, g_col, g_row, b_gates_l, conv_l):
    return pl.pallas_call(
        _mlstm_kernel,
        grid=(BATCH, B_HEADS),
        in_specs=[pl.BlockSpec(memory_space=pltpu.SMEM),
                  pl.BlockSpec((SEG, HD), lambda b, h: (b, COL_BQ + h)),
                  pl.BlockSpec((SEG, HD), lambda b, h: (b, COL_BK + h)),
                  pl.BlockSpec((SEG, HD), lambda b, h: (b, COL_BV + h)),
                  pl.BlockSpec((SEG, HD), lambda b, h: (b, COL_BO + h)),
                  pl.BlockSpec((1, SEG, 4), lambda b, h: (h, b, 0)),
                  pl.BlockSpec((4 * B_HEADS, SEG), lambda b, h: (0, b)),
                  pl.BlockSpec((3, HD), lambda b, h: (0, h)),
                  pl.BlockSpec((3, HD), lambda b, h: (0, B_HEADS + h))],
        out_specs=pl.BlockSpec((SEG, HD), lambda b, h: (b, h)),
        out_shape=jax.ShapeDtypeStruct((T, B_HEADS * HD), BF16),
        scratch_shapes=[pltpu.VMEM((SEG, HD), BF16), pltpu.VMEM((SEG, HD), BF16),
                        pltpu.VMEM((CONV_ROWS, HD), F32),
                        pltpu.VMEM((SEG, HD), F32), pltpu.VMEM((SEG, HD), F32)],
        name="mlstm",
        compiler_params=_cparams(("arbitrary", "arbitrary")),
    )(b_gates_l, z, z, z, z, g_col, g_row, conv_l, conv_l)


OUT_TM = SEG // 6


def _top2_routing(biased, scores, comb_ref, sel_ref):
    rows = [biased[e:e + 1, :] for e in range(N_EXPERTS)]
    gscore = []
    for g in range(N_GROUPS):
        v0, v1, v2, v3 = rows[4 * g:4 * g + 4]
        hi01, lo01 = jnp.maximum(v0, v1), jnp.minimum(v0, v1)
        hi23, lo23 = jnp.maximum(v2, v3), jnp.minimum(v2, v3)
        gscore.append(jnp.maximum(hi01, hi23) + jnp.maximum(jnp.minimum(hi01, hi23), jnp.maximum(lo01, lo23)))
    picked = []
    total = None
    for e in range(N_EXPERTS):
        g = e // PER_GROUP
        ok = None
        for o in range(N_GROUPS):
            if o != g:
                c = (gscore[g] > gscore[o]) if o < g else (gscore[g] >= gscore[o])
                ok = c if ok is None else (ok & c)
        ahead = jnp.zeros_like(rows[e])
        for o in range(4 * g, 4 * g + 4):
            if o != e:
                c = (rows[o] > rows[e]) if o > e else (rows[o] >= rows[e])
                ahead = ahead + jnp.where(c, 1.0, 0.0)
        sel = jnp.where(ok & (ahead < 2.0), 1.0, 0.0)
        sel_ref[e:e + 1, :] = sel
        picked.append(sel * scores[e:e + 1, :])
        total = picked[-1] if total is None else total + picked[-1]
    for e in range(N_EXPERTS):
        comb_ref[e:e + 1, :] = picked[e] / total


def _out_kernel(ya_ref, yb_ref, yn_ref, w_ref, x_ref, mod_ref, g_ref, wr_ref, br_ref,
                x1_ref, h2_ref, comb_ref, sel_ref, ycat):
    i = pl.program_id(0)
    batch = i // 6
    rows = lax.broadcasted_iota(I32, (OUT_TM, 1), 0) + (i % 6) * OUT_TM
    is_ctx = rows < CTX
    ycat[:, 0:A_HEADS * HD] = ya_ref[...]
    ycat[:, A_HEADS * HD:(A_HEADS + B_HEADS) * HD] = yb_ref[...]
    ycat[:, (A_HEADS + B_HEADS) * HD:D] = yn_ref[...]
    x1 = x_ref[...] + _row_mod(mod_ref, 2, batch, is_ctx) * jnp.dot(
        ycat[...], w_ref[...], preferred_element_type=F32)
    x1_ref[...] = x1
    h2 = (_rms(x1) * g_ref[...]) * (1.0 + _row_mod(mod_ref, 4, batch, is_ctx)) + _row_mod(mod_ref, 3, batch, is_ctx)
    h2_ref[...] = h2
    logits = lax.dot_general(wr_ref[...], h2, (((1,), (1,)), ((), ())), preferred_element_type=F32,
                             precision=lax.Precision.HIGHEST)
    scores = jax.nn.sigmoid(logits)
    _top2_routing(scores + br_ref[...], scores, comb_ref, sel_ref)


def out_proj(ya, yb, yn, w_out_b, xa, mod_l, g, w_router_t, b_router_c):
    row = lambda i: (i, 0)
    fixed = lambda i: (0, 0)
    return pl.pallas_call(
        _out_kernel,
        grid=(T // OUT_TM,),
        in_specs=[pl.BlockSpec((OUT_TM, A_HEADS * HD), row),
                  pl.BlockSpec((OUT_TM, B_HEADS * HD), row),
                  pl.BlockSpec((OUT_TM, C_HEADS * HD), row),
                  pl.BlockSpec((D, D), fixed),
                  pl.BlockSpec((OUT_TM, D), row),
                  pl.BlockSpec((8, 6 * D), fixed),
                  pl.BlockSpec((1, D), fixed),
                  pl.BlockSpec((N_EXPERTS, D), fixed),
                  pl.BlockSpec((N_EXPERTS, 1), fixed)],
        out_specs=[pl.BlockSpec((OUT_TM, D), row),
                   pl.BlockSpec((OUT_TM, D), row),
                   pl.BlockSpec((N_EXPERTS, OUT_TM), lambda i: (0, i)),
                   pl.BlockSpec((N_EXPERTS, OUT_TM), lambda i: (0, i))],
        out_shape=[jax.ShapeDtypeStruct((T, D), F32),
                   jax.ShapeDtypeStruct((T, D), F32),
                   jax.ShapeDtypeStruct((N_EXPERTS, T), F32),
                   jax.ShapeDtypeStruct((N_EXPERTS, T), F32)],
        scratch_shapes=[pltpu.VMEM((OUT_TM, D), BF16)],
        name="out_proj",
        compiler_params=_cparams(("arbitrary",)),
    )(ya, yb, yn, w_out_b, xa, mod_l, g, w_router_t, b_router_c)


SC_TM = 512


def _scatter_kernel(pos0_ref, pos1_ref, ztile_ref, h_ref, hs_ref, zero_scr, sem, zsem):
    i = pl.program_id(0)

    @pl.when(i == 0)
    def _():
        zero_scr[...] = jnp.zeros((ETILE, D), F32)

        def fill(e, c):
            t = ztile_ref[e]

            @pl.when(t >= 0)
            def _():
                cp = pltpu.make_async_copy(zero_scr, hs_ref.at[pl.ds(pl.multiple_of(t * ETILE, ETILE), ETILE), :], zsem)
                cp.start()
                cp.wait()
            return c

        lax.fori_loop(0, 2 * N_EXPERTS, fill, 0)

    def row_copy(r, p):
        return pltpu.make_async_copy(h_ref.at[pl.ds(r, 1), :], hs_ref.at[pl.ds(p, 1), :], sem)

    def issue(r, c):
        t = i * SC_TM + r
        row_copy(r, pos0_ref[t]).start()
        row_copy(r, pos1_ref[t]).start()
        return c

    lax.fori_loop(0, SC_TM, issue, 0)

    def drain(r, c):
        row_copy(0, 0).wait()
        row_copy(0, 0).wait()
        return c

    lax.fori_loop(0, SC_TM, drain, 0)


def scatter_rows(pos0, pos1, ztile, h2):
    return pl.pallas_call(
        _scatter_kernel,
        grid_spec=pltpu.PrefetchScalarGridSpec(
            num_scalar_prefetch=3,
            grid=(T // SC_TM,),
            in_specs=[pl.BlockSpec((SC_TM, D), lambda i, *_: (i, 0))],
            out_specs=pl.BlockSpec(memory_space=pl.ANY),
            scratch_shapes=[pltpu.VMEM((ETILE, D), F32), pltpu.SemaphoreType.DMA, pltpu.SemaphoreType.DMA]),
        out_shape=jax.ShapeDtypeStruct((P_ROWS, D), F32),
        name="scatter_rows",
        compiler_params=_cparams(("arbitrary",)),
    )(pos0, pos1, ztile, h2)


CAST_ROWS = 256


def _cast_into(dst, src, n_rows):
    def body(c, carry):
        r0 = pl.multiple_of(c * CAST_ROWS, CAST_ROWS)
        dst[pl.ds(r0, CAST_ROWS), :] = src[0, 0, pl.ds(r0, CAST_ROWS), :].astype(BF16)
        return carry

    lax.fori_loop(0, n_rows // CAST_ROWS, body, 0)


def _new_expert(te_ref, j):
    return (j == 0) | (te_ref[j] != te_ref[jnp.maximum(j - 1, 0)])


def _ffn_up_kernel(te_ref, tv_ref, hs_ref, w1_ref, w3_ref, act_ref, w1b, w3b):
    j = pl.program_id(0)

    @pl.when(_new_expert(te_ref, j))
    def _():
        _cast_into(w1b, w1_ref, D)
        _cast_into(w3b, w3_ref, D)

    @pl.when(tv_ref[j] > 0)
    def _():
        xb = hs_ref[...].astype(BF16)
        a = jnp.dot(xb, w1b[...], preferred_element_type=F32)
        b = jnp.dot(xb, w3b[...], preferred_element_type=F32)
        act_ref[...] = (_silu(a) * b).astype(BF16)

    @pl.when(tv_ref[j] == 0)
    def _():
        act_ref[...] = jnp.zeros((ETILE, FF), BF16)


def _ffn_down_kernel(te_ref, tv_ref, act_ref, w2_ref, ys_ref, w2b):
    j = pl.program_id(0)

    @pl.when(_new_expert(te_ref, j))
    def _():
        _cast_into(w2b, w2_ref, FF)

    @pl.when(tv_ref[j] > 0)
    def _():
        ys_ref[...] = jnp.dot(act_ref[...], w2b[...], preferred_element_type=F32)

    @pl.when(tv_ref[j] == 0)
    def _():
        ys_ref[...] = jnp.zeros((ETILE, D), F32)


def _tile_or_first(j, te, tv):
    return jnp.where(tv[j] > 0, j, 0)


def ffn_up(te, tv, hs, w1, w3, layer):
    wmap = lambda j, te, tv: (layer, te[j], 0, 0)
    return pl.pallas_call(
        _ffn_up_kernel,
        grid_spec=pltpu.PrefetchScalarGridSpec(
            num_scalar_prefetch=2,
            grid=(N_ETILES,),
            in_specs=[pl.BlockSpec((ETILE, D), lambda j, te, tv: (_tile_or_first(j, te, tv), 0)),
                      pl.BlockSpec((1, 1, D, FF), wmap),
                      pl.BlockSpec((1, 1, D, FF), wmap)],
            out_specs=pl.BlockSpec((ETILE, FF), lambda j, te, tv: (j, 0)),
            scratch_shapes=[pltpu.VMEM((D, FF), BF16), pltpu.VMEM((D, FF), BF16)]),
        out_shape=jax.ShapeDtypeStruct((P_ROWS, FF), BF16),
        name="ffn_up",
        compiler_params=_cparams(("arbitrary",)),
    )(te, tv, hs, w1, w3)


def ffn_down(te, tv, act, w2, layer):
    return pl.pallas_call(
        _ffn_down_kernel,
        grid_spec=pltpu.PrefetchScalarGridSpec(
            num_scalar_prefetch=2,
            grid=(N_ETILES,),
            in_specs=[pl.BlockSpec((ETILE, FF), lambda j, te, tv: (j, 0)),
                      pl.BlockSpec((1, 1, FF, D), lambda j, te, tv: (layer, te[j], 0, 0))],
            out_specs=pl.BlockSpec((ETILE, D), lambda j, te, tv: (j, 0)),
            scratch_shapes=[pltpu.VMEM((FF, D), BF16)]),
        out_shape=jax.ShapeDtypeStruct((P_ROWS, D), F32),
        name="ffn_down",
        compiler_params=_cparams(("arbitrary",)),
    )(te, tv, act, w2)


CB_TM = SEG // 9


def _combine_kernel(pos0_ref, pos1_ref, ys_ref, x_ref, w_ref, mod_ref, o_ref, buf, sem):
    i = pl.program_id(0)
    batch = i // 9
    is_ctx = (i % 9) == 0

    def row_copy(k, r, p):
        return pltpu.make_async_copy(ys_ref.at[pl.ds(p, 1), :], buf.at[k, pl.ds(r, 1), :], sem)

    def issue(r, c):
        t = i * CB_TM + r
        row_copy(0, r, pos0_ref[t]).start()
        row_copy(1, r, pos1_ref[t]).start()
        return c

    lax.fori_loop(0, CB_TM, issue, 0)

    def drain(r, c):
        row_copy(0, 0, 0).wait()
        row_copy(1, 0, 0).wait()
        return c

    lax.fori_loop(0, CB_TM, drain, 0)
    f = w_ref[:, 0:1] * buf[0] + w_ref[:, 1:2] * buf[1]
    o_ref[...] = x_ref[...] + _row_mod(mod_ref, 5, batch, is_ctx) * f


def combine_rows(pos0, pos1, ys, x1, w01, mod_l):
    return pl.pallas_call(
        _combine_kernel,
        grid_spec=pltpu.PrefetchScalarGridSpec(
            num_scalar_prefetch=2,
            grid=(T // CB_TM,),
            in_specs=[pl.BlockSpec(memory_space=pl.ANY),
                      pl.BlockSpec((CB_TM, D), lambda i, *_: (i, 0)),
                      pl.BlockSpec((CB_TM, 2), lambda i, *_: (i, 0)),
                      pl.BlockSpec((8, 6 * D), lambda i, *_: (0, 0))],
            out_specs=pl.BlockSpec((CB_TM, D), lambda i, *_: (i, 0)),
            scratch_shapes=[pltpu.VMEM((2, CB_TM, D), F32), pltpu.SemaphoreType.DMA]),
        out_shape=jax.ShapeDtypeStruct((T, D), F32),
        name="combine_rows",
        compiler_params=_cparams(("arbitrary",)),
    )(pos0, pos1, ys, x1, w01, mod_l)


def route_plan(comb_t, sel_t):
    sel = sel_t > 0.5
    cnt = jnp.sum(sel, axis=1).astype(I32)
    tiles = (cnt + ETILE - 1) // ETILE
    tend = jnp.cumsum(tiles)
    toff = tend - tiles
    rank = jnp.cumsum(sel.astype(I32), axis=1) - 1
    pos = toff[:, None] * ETILE + rank
    pos0 = jnp.min(jnp.where(sel, pos, P_ROWS), axis=0).astype(I32)
    pos1 = jnp.max(jnp.where(sel, pos, -1), axis=0).astype(I32)
    w0 = jnp.sum(jnp.where(sel & (pos == pos0[None]), comb_t, 0.0), axis=0)
    w1 = jnp.sum(jnp.where(sel & (pos == pos1[None]), comb_t, 0.0), axis=0)
    n_used = tend[-1]
    tidx = jnp.arange(N_ETILES, dtype=I32)
    te_raw = jnp.sum((tend[None, :] <= tidx[:, None]).astype(I32), axis=1)
    te_last = jnp.sum((tend <= n_used - 1).astype(I32))
    te = jnp.where(tidx < n_used, te_raw, te_last).astype(I32)
    te = jnp.minimum(te, N_EXPERTS - 1)
    tv = jnp.clip(cnt[te] - (tidx - toff[te]) * ETILE, 0, ETILE)
    tv = jnp.where(tidx < n_used, tv, 0).astype(I32)
    tail = n_used + jnp.arange(N_EXPERTS, dtype=I32)
    ztile = jnp.concatenate([jnp.where(tiles > 0, tend - 1, -1), jnp.where(tail < N_ETILES, tail, -1)]).astype(I32)
    return pos0, pos1, jnp.stack([w0, w1], axis=1), te, tv, ztile


def _final_kernel(x_ref, g_ref, o_ref):
    o_ref[...] = _rms(x_ref[...]) * g_ref[...]


def final_norm(xa, g):
    nb = SEQ // CTX
    return pl.pallas_call(
        _final_kernel,
        grid=(BATCH, nb),
        in_specs=[pl.BlockSpec((CTX, D), lambda b, j: (b * (nb + 1) + 1 + j, 0)),
                  pl.BlockSpec((1, D), lambda b, j: (0, 0))],
        out_specs=pl.BlockSpec((CTX, D), lambda b, j: (b * nb + j, 0)),
        out_shape=jax.ShapeDtypeStruct((BATCH * SEQ, D), F32),
        name="final_norm",
        compiler_params=_cparams(("arbitrary", "arbitrary")),
    )(xa, g)


def _rope_tables():
    t = jnp.arange(SEQ, dtype=I32)
    row = (t // GRID_W).astype(F32)
    col = (t % GRID_W).astype(F32)
    n_freq = HD // 4
    inv_freq = ROPE_BASE ** (-jnp.arange(n_freq, dtype=F32) / n_freq)
    ar = row[:, None] * inv_freq[None, :]
    ac = col[:, None] * inv_freq[None, :]
    cos_t = jnp.concatenate([jnp.cos(ar), jnp.cos(ar), jnp.cos(ac), jnp.cos(ac)], axis=1)
    sin_t = jnp.concatenate([-jnp.sin(ar), jnp.sin(ar), -jnp.sin(ac), jnp.sin(ac)], axis=1)
    return cos_t, sin_t


GATE_LO, GATE_HI = 3328, 3344


def kernel(x, c, ctx, c_ctx, w_ada, b_ada, norm_mix, norm_ffn, w_in, b_gates, conv_qk, sink, rpb, w_out,
           w_router, b_router, w1, w3, w2, norm_final):
    xa = jnp.concatenate([ctx, x], axis=1).reshape(T, D)
    cond8 = jnp.concatenate([c, c_ctx[None], jnp.zeros((3, D), F32)], axis=0)
    mod = ada_all(cond8, w_ada, b_ada)
    cos_t, sin_t = _rope_tables()
    w_main = jnp.concatenate([w_in[:, :, :GATE_LO], w_in[:, :, GATE_HI:]], axis=2).astype(BF16)
    w_gate = jnp.pad(w_in[:, :, GATE_LO:GATE_HI], ((0, 0), (0, 0), (0, HD - 4 * B_HEADS))).astype(BF16)
    w_out_b = w_out.astype(BF16)
    w_router_t = w_router.T
    b_router_c = b_router.reshape(N_EXPERTS, 1)

    for l in range(DEPTH):
        z, zg = in_proj(xa, norm_mix[l].reshape(1, D), mod[l], w_main[l], w_gate[l])
        gates = zg[:, :4 * B_HEADS]
        g_col = gates.reshape(T, 4, B_HEADS).transpose(2, 0, 1)
        g_row = gates.T
        ya = window_attn(z, sink[l], cos_t, sin_t)
        yb = mlstm(z, g_col, g_row, b_gates[l], conv_qk[l])
        yn = nbr_attn(z, nbr_bias_table(rpb[l]))
        x1, h2, comb_t, sel_t = out_proj(ya, yb, yn, w_out_b[l], xa, mod[l], norm_ffn[l].reshape(1, D),
                                         w_router_t, b_router_c)
        pos0, pos1, w01, te, tv, ztile = route_plan(comb_t, sel_t)
        hs = scatter_rows(pos0, pos1, ztile, h2)
        act = ffn_up(te, tv, hs, w1, w3, l)
        ys = ffn_down(te, tv, act, w2, l)
        xa = combine_rows(pos0, pos1, ys, x1, w01, mod[l])
    return final_norm(xa, norm_final.reshape(1, D)).reshape(BATCH, SEQ, D)
```

```python
import functools

import jax
import jax.numpy as jnp
from jax import lax
from jax.experimental import pallas as pl
from jax.experimental.pallas import tpu as pltpu

F32 = jnp.float32
BF16 = jnp.bfloat16
I32 = jnp.int32

D = 2048
BATCH = 4
SEQ = 2048
CTX = 256
SEG = CTX + SEQ
T = BATCH * SEG
DEPTH = 4
GRID_W = 64
HD = 128
A_HEADS, A_KV, A_GRP = 6, 2, 3
B_HEADS = 4
C_HEADS = 6
A_BLOCK = 128
NB_ROWS, NB_COLS = 8, 16
CHUNK = 256
N_EXPERTS, N_GROUPS, PER_GROUP = 16, 4, 4
FF = 1024
EPS = 1e-6
ROPE_BASE = 10000.0
NEG = -1e30
ATT_SCALE = HD ** -0.5

N_MAIN = 5632
COL_AQ, COL_AK, COL_AV = 0, 6, 8
COL_BQ, COL_BK, COL_BV, COL_BO = 10, 14, 18, 22
COL_CQ, COL_CK, COL_CV = 26, 32, 38

ETILE = 256
N_ETILES = (2 * T) // ETILE + N_EXPERTS
P_ROWS = N_ETILES * ETILE

VMEM_LIMIT = 56 * 1024 * 1024


def _cparams(sem):
    return pltpu.CompilerParams(dimension_semantics=sem, vmem_limit_bytes=VMEM_LIMIT)


def _silu(v):
    return v * jax.nn.sigmoid(v)


def _log_sigmoid(v):
    return jnp.minimum(v, 0.0) - jnp.log1p(jnp.exp(-jnp.abs(v)))


ADA_TN = 1024


def _ada_kernel(s_ref, w_ref, b_ref, o_ref):
    s = _silu(s_ref[...]).astype(BF16)
    o_ref[0] = jnp.dot(s, w_ref[0].astype(BF16), preferred_element_type=F32) + b_ref[0]


def ada_all(cond8, w_ada, b_ada):
    n = w_ada.shape[-1]
    return pl.pallas_call(
        _ada_kernel,
        grid=(DEPTH, n // ADA_TN),
        in_specs=[pl.BlockSpec((8, D), lambda l, j: (0, 0)),
                  pl.BlockSpec((1, D, ADA_TN), lambda l, j: (l, 0, j)),
                  pl.BlockSpec((1, 1, ADA_TN), lambda l, j: (l, 0, j))],
        out_specs=pl.BlockSpec((1, 8, ADA_TN), lambda l, j: (l, 0, j)),
        out_shape=jax.ShapeDtypeStruct((DEPTH, 8, n), F32),
        name="ada_mod",
        compiler_params=_cparams(("arbitrary", "arbitrary")),
    )(cond8, w_ada, b_ada.reshape(DEPTH, 1, n))


def _row_mod(mod_ref, chunk, batch, is_ctx):
    lat = mod_ref[pl.ds(batch, 1), chunk * D:(chunk + 1) * D]
    ctx = mod_ref[4:5, chunk * D:(chunk + 1) * D]
    return jnp.where(is_ctx, ctx, lat)


def _rms(x):
    return x * lax.rsqrt(jnp.mean(x * x, axis=-1, keepdims=True) + EPS)


IN_TM = SEG // 2
IN_TN = 1408


def _in_kernel(x_ref, g_ref, mod_ref, w_ref, wg_ref, z_ref, zg_ref, h_scr):
    i = pl.program_id(0)
    j = pl.program_id(1)

    @pl.when(j == 0)
    def _():
        batch = i // 2
        rows = lax.broadcasted_iota(I32, (IN_TM, 1), 0) + (i % 2) * IN_TM
        is_ctx = rows < CTX
        xn = _rms(x_ref[...]) * g_ref[...]
        h = xn * (1.0 + _row_mod(mod_ref, 1, batch, is_ctx)) + _row_mod(mod_ref, 0, batch, is_ctx)
        hb = h.astype(BF16)
        h_scr[...] = hb
        zg_ref[...] = jnp.dot(hb, wg_ref[...], preferred_element_type=F32)

    z_ref[...] = jnp.dot(h_scr[...], w_ref[...], preferred_element_type=F32).astype(BF16)


def in_proj(xa, g, mod_l, w_main, w_gate):
    return pl.pallas_call(
        _in_kernel,
        grid=(T // IN_TM, N_MAIN // IN_TN),
        in_specs=[pl.BlockSpec((IN_TM, D), lambda i, j: (i, 0)),
                  pl.BlockSpec((1, D), lambda i, j: (0, 0)),
                  pl.BlockSpec((8, 6 * D), lambda i, j: (0, 0)),
                  pl.BlockSpec((D, IN_TN), lambda i, j: (0, j)),
                  pl.BlockSpec((D, HD), lambda i, j: (0, 0))],
        out_specs=[pl.BlockSpec((IN_TM, IN_TN), lambda i, j: (i, j)),
                   pl.BlockSpec((IN_TM, HD), lambda i, j: (i, 0))],
        out_shape=[jax.ShapeDtypeStruct((T, N_MAIN), BF16),
                   jax.ShapeDtypeStruct((T, HD), F32)],
        scratch_shapes=[pltpu.VMEM((IN_TM, D), BF16)],
        name="in_proj",
        compiler_params=_cparams(("arbitrary", "arbitrary")),
    )(xa, g, mod_l, w_main, w_gate)


NBLK = SEQ // A_BLOCK
KPAD = SEG + A_BLOCK


def _dot_nt(a, b):
    return lax.dot_general(a, b, (((1,), (1,)), ((), ())), preferred_element_type=F32)


def _attend(scores, values, extra=None):
    m = scores[0].max(axis=-1, keepdims=True)
    for s in scores[1:]:
        m = jnp.maximum(m, s.max(axis=-1, keepdims=True))
    if extra is not None:
        m = jnp.maximum(m, extra)
    den = None if extra is None else jnp.exp(extra - m)
    out = None
    for s, v in zip(scores, values):
        e = jnp.exp(s - m)
        d = e.sum(axis=-1, keepdims=True)
        den = d if den is None else den + d
        o = jnp.dot(e.astype(BF16), v, preferred_element_type=F32)
        out = o if out is None else out + o
    return out / den


def _win_kernel(sink_ref, q_ref, k_ref, v_ref, cos_ref, sin_ref, o_ref, qs, ks, vs):
    kv = pl.program_id(1)
    lane = lax.broadcasted_iota(I32, (A_BLOCK, HD), 1)
    first_half = (lane % 64) < 32

    def rope(zf, cos, sin):
        zr = jnp.where(first_half, pltpu.roll(zf, 96, 1), pltpu.roll(zf, 32, 1))
        return zf * cos + zr * sin

    ks[0:CTX, :] = k_ref[0:CTX, :]
    ks[SEG:KPAD, :] = jnp.zeros((A_BLOCK, HD), BF16)
    vs[0:SEG, :] = v_ref[...]
    vs[SEG:KPAD, :] = jnp.zeros((A_BLOCK, HD), BF16)

    def rope_blk(i, c):
        r0 = pl.multiple_of(i * A_BLOCK, A_BLOCK)
        cos = cos_ref[pl.ds(r0, A_BLOCK), :]
        sin = sin_ref[pl.ds(r0, A_BLOCK), :]
        ks[pl.ds(CTX + r0, A_BLOCK), :] = rope(k_ref[pl.ds(CTX + r0, A_BLOCK), :].astype(F32), cos, sin).astype(BF16)
        for g in range(A_GRP):
            zf = q_ref[pl.ds(CTX + r0, A_BLOCK), g * HD:(g + 1) * HD].astype(F32)
            qs[i, g * A_BLOCK:(g + 1) * A_BLOCK, :] = rope(zf, cos, sin).astype(BF16)
        return c

    lax.fori_loop(0, NBLK, rope_blk, 0)

    nq = A_GRP * A_BLOCK
    row = lax.broadcasted_iota(I32, (nq, 1), 0)
    sink = jnp.where(row < A_BLOCK, sink_ref[kv * A_GRP],
                     jnp.where(row < 2 * A_BLOCK, sink_ref[kv * A_GRP + 1], sink_ref[kv * A_GRP + 2]))
    r = lax.broadcasted_iota(I32, (nq, 3 * A_BLOCK), 0) % A_BLOCK
    c = lax.broadcasted_iota(I32, (nq, 3 * A_BLOCK), 1)
    band = (c >= r) & (c <= r + 2 * A_BLOCK)

    def blk(i, carry):
        w0 = pl.multiple_of(CTX - A_BLOCK + i * A_BLOCK, A_BLOCK)
        q = qs[i]
        s_c = _dot_nt(q, ks[0:CTX, :]) * ATT_SCALE
        s_w = _dot_nt(q, ks[pl.ds(w0, 3 * A_BLOCK), :]) * ATT_SCALE
        kpos = (i - 1) * A_BLOCK + c
        s_w = jnp.where(band & (kpos >= 0) & (kpos < SEQ), s_w, NEG)
        o = _attend([s_c, s_w], [vs[0:CTX, :], vs[pl.ds(w0, 3 * A_BLOCK), :]], sink)
        o0 = pl.multiple_of(CTX + i * A_BLOCK, A_BLOCK)
        for g in range(A_GRP):
            o_ref[pl.ds(o0, A_BLOCK), g * HD:(g + 1) * HD] = o[g * A_BLOCK:(g + 1) * A_BLOCK].astype(BF16)
        return carry

    lax.fori_loop(0, NBLK, blk, 0)

    for g in range(A_GRP):
        s = _dot_nt(q_ref[0:CTX, g * HD:(g + 1) * HD], k_ref[0:CTX, :]) * ATT_SCALE
        o = _attend([s], [v_ref[0:CTX, :]], jnp.full((CTX, 1), sink_ref[kv * A_GRP + g], F32))
        o_ref[0:CTX, g * HD:(g + 1) * HD] = o.astype(BF16)


def window_attn(z, sink_l, cos_t, sin_t):
    return pl.pallas_call(
        _win_kernel,
        grid=(BATCH, A_KV),
        in_specs=[pl.BlockSpec(memory_space=pltpu.SMEM),
                  pl.BlockSpec((SEG, A_GRP * HD), lambda b, kv: (b, kv)),
                  pl.BlockSpec((SEG, HD), lambda b, kv: (b, COL_AK + kv)),
                  pl.BlockSpec((SEG, HD), lambda b, kv: (b, COL_AV + kv)),
                  pl.BlockSpec((SEQ, HD), lambda b, kv: (0, 0)),
                  pl.BlockSpec((SEQ, HD), lambda b, kv: (0, 0))],
        out_specs=pl.BlockSpec((SEG, A_GRP * HD), lambda b, kv: (b, kv)),
        out_shape=jax.ShapeDtypeStruct((T, A_HEADS * HD), BF16),
        scratch_shapes=[pltpu.VMEM((NBLK, A_GRP * A_BLOCK, HD), BF16),
                        pltpu.VMEM((KPAD, HD), BF16),
                        pltpu.VMEM((KPAD, HD), BF16)],
        name="window_attn",
        compiler_params=_cparams(("arbitrary", "arbitrary")),
    )(sink_l, z, z, z, cos_t, sin_t)


GRID_ROWS = SEQ // GRID_W
NBQ_ROWS = 4
NBK_ROWS = 12
NBQ, NBK = NBQ_ROWS * GRID_W, NBK_ROWS * GRID_W
N_NBLK = GRID_ROWS // NBQ_ROWS
NB_BASE_MAX = GRID_ROWS - NBK_ROWS


def _nbr_key_base(first_row):
    return jnp.clip(first_row - NB_ROWS // 2, 0, NB_BASE_MAX)


def _nbr_kernel(q_ref, k_ref, v_ref, bias_ref, o_ref):
    def block(i):
        kind = jnp.where(i == 0, 0, jnp.where(i == N_NBLK - 1, 2, 1))
        q0 = pl.multiple_of(CTX + i * NBQ, NBQ)
        k0 = pl.multiple_of(CTX + _nbr_key_base(i * NBQ_ROWS) * GRID_W, GRID_W)
        q = q_ref[pl.ds(q0, NBQ), :]
        s_c = _dot_nt(q, k_ref[0:CTX, :]) * ATT_SCALE
        s_n = _dot_nt(q, k_ref[pl.ds(k0, NBK), :]) * ATT_SCALE + bias_ref[0, kind]
        o = _attend([s_c, s_n], [v_ref[0:CTX, :], v_ref[pl.ds(k0, NBK), :]])
        o_ref[pl.ds(q0, NBQ), :] = o.astype(BF16)

    def blocks(i, carry):
        block(2 * i)
        block(2 * i + 1)
        return carry

    lax.fori_loop(0, N_NBLK // 2, blocks, 0)

    s = _dot_nt(q_ref[0:CTX, :], k_ref[0:CTX, :]) * ATT_SCALE
    o_ref[0:CTX, :] = _attend([s], [v_ref[0:CTX, :]]).astype(BF16)


def nbr_attn(z, bias_l):
    return pl.pallas_call(
        _nbr_kernel,
        grid=(BATCH, C_HEADS),
        in_specs=[pl.BlockSpec((SEG, HD), lambda b, h: (b, COL_CQ + h)),
                  pl.BlockSpec((SEG, HD), lambda b, h: (b, COL_CK + h)),
                  pl.BlockSpec((SEG, HD), lambda b, h: (b, COL_CV + h)),
                  pl.BlockSpec((1, 3, NBQ, NBK), lambda b, h: (h, 0, 0, 0))],
        out_specs=pl.BlockSpec((SEG, HD), lambda b, h: (b, h)),
        out_shape=jax.ShapeDtypeStruct((T, C_HEADS * HD), BF16),
        name="nbr_attn",
        compiler_params=_cparams(("arbitrary", "arbitrary")),
    )(z, z, z, bias_l)


def nbr_bias_table(rpb_l):
    cols = jnp.arange(GRID_W)
    start_c = jnp.clip(cols - NB_COLS // 2, 0, GRID_W - NB_COLS)
    col_ok = (cols[None, :] >= start_c[:, None]) & (cols[None, :] < start_c[:, None] + NB_COLS)
    dc = jnp.clip(cols[None, :] - cols[:, None] + NB_COLS - 1, 0, 2 * NB_COLS - 2)
    tab = jnp.where(col_ok[None, None], rpb_l[:, :, dc].astype(F32), NEG)
    first_row = jnp.array([0, NBQ_ROWS, GRID_ROWS - NBQ_ROWS])
    r = first_row[:, None, None] + jnp.arange(NBQ_ROWS)[None, :, None]
    kr = _nbr_key_base(first_row)[:, None, None] + jnp.arange(NBK_ROWS)[None, None, :]
    start_r = jnp.clip(r - NB_ROWS // 2, 0, GRID_ROWS - NB_ROWS)
    row_ok = (kr >= start_r) & (kr < start_r + NB_ROWS)
    dr = jnp.clip(kr - r + NB_ROWS - 1, 0, 2 * NB_ROWS - 2)
    blk = jnp.where(row_ok[None, :, :, :, None, None], tab[:, dr], NEG)
    return blk.transpose(0, 1, 2, 4, 3, 5).reshape(C_HEADS, 3, NBQ, NBK)


N_LCHUNK = SEQ // CHUNK
PADR = 8
CONV_ROWS = PADR + CTX + PADR + SEQ + PADR


def _conv_off(r0):
    return jnp.where(r0 < CTX, r0 + PADR, r0 + 2 * PADR)


def _mlstm_kernel(bg_ref, q_ref, k_ref, v_ref, og_ref, gc_ref, gr_ref, cwq_ref, cwk_ref, y_ref,
                  qs, ks, stage, hf, hb):
    h = pl.program_id(1)
    rowi = lax.broadcasted_iota(I32, (CHUNK, 1), 0)

    def conv_silu(src_ref, cw_ref, dst, post):
        stage[...] = jnp.zeros((CONV_ROWS, HD), F32)
        stage[PADR:PADR + CTX, :] = src_ref[0:CTX, :].astype(F32)
        stage[2 * PADR + CTX:2 * PADR + SEG, :] = src_ref[CTX:SEG, :].astype(F32)

        def blk(c, carry):
            r0 = pl.multiple_of(c * CHUNK, CHUNK)
            s0 = pl.multiple_of(_conv_off(r0), PADR)
            cur = stage[pl.ds(s0, CHUNK), :]
            prev_last = stage[pl.ds(s0 - PADR, PADR), :][PADR - 1:PADR, :]
            next_first = stage[pl.ds(s0 + CHUNK, PADR), :][0:1, :]
            zm = jnp.where(rowi == 0, prev_last, pltpu.roll(cur, 1, 0))
            zp = jnp.where(rowi == CHUNK - 1, next_first, pltpu.roll(cur, CHUNK - 1, 0))
            y = zm * cw_ref[0:1, :] + cur * cw_ref[1:2, :] + zp * cw_ref[2:3, :]
            dst[pl.ds(r0, CHUNK), :] = (_silu(y) * post).astype(BF16)
            return carry

        lax.fori_loop(0, SEG // CHUNK, blk, 0)

    conv_silu(q_ref, cwq_ref, qs, 1.0)
    conv_silu(k_ref, cwk_ref, ks, ATT_SCALE)

    t_idx = lax.broadcasted_iota(I32, (CHUNK, CHUNK), 0)
    s_idx = lax.broadcasted_iota(I32, (CHUNK, CHUNK), 1)

    def chunk(r0, bwd, state, dst):
        c_mat, n_vec, m_prev = state
        ki, kf = (2, 3) if bwd else (0, 1)
        bi = bg_ref[ki * B_HEADS + h]
        bf = bg_ref[kf * B_HEADS + h]
        qc = qs[pl.ds(r0, CHUNK), :]
        kc = ks[pl.ds(r0, CHUNK), :]
        vc = v_ref[pl.ds(r0, CHUNK), :]
        i_row = gr_ref[pl.ds(ki * B_HEADS + h, 1), pl.ds(r0, CHUNK)] + bi
        lf_row = _log_sigmoid(gr_ref[pl.ds(kf * B_HEADS + h, 1), pl.ds(r0, CHUNK)] + bf)
        i_col = gc_ref[0, pl.ds(r0, CHUNK), ki:ki + 1] + bi
        lf_col = _log_sigmoid(gc_ref[0, pl.ds(r0, CHUNK), kf:kf + 1] + bf)
        causal = (s_idx >= t_idx) if bwd else (s_idx <= t_idx)
        causal_t = (t_idx >= s_idx) if bwd else (t_idx <= s_idx)
        b_col = jnp.sum(jnp.where(causal, lf_row, 0.0), axis=1, keepdims=True)
        b_row = jnp.sum(jnp.where(causal_t, lf_col, 0.0), axis=0, keepdims=True)
        a_row = i_row - b_row
        a_col = i_col - b_col
        mx = jnp.maximum(m_prev, jnp.max(jnp.where(causal, a_row, NEG), axis=1, keepdims=True))
        dm = jnp.exp(jnp.where(causal, a_row - mx, NEG))
        s = _dot_nt(qc, kc) * dm
        w_int = jnp.exp(m_prev - mx)
        num = (jnp.dot(s.astype(BF16), vc, preferred_element_type=F32)
               + w_int * jnp.dot(qc, c_mat.astype(BF16), preferred_element_type=F32))
        den = (jnp.sum(s, axis=1, keepdims=True)
               + w_int * jnp.sum(qc.astype(F32) * n_vec, axis=1, keepdims=True))
        m_t = b_col + mx
        dst[pl.ds(r0, CHUNK), :] = num / jnp.maximum(jnp.abs(den), jnp.exp(-m_t))
        b_end = jnp.sum(lf_row, axis=1, keepdims=True)
        m_end = jnp.maximum(m_prev, jnp.max(a_row, axis=1, keepdims=True))
        decay = jnp.exp(m_prev - m_end)
        w_col = jnp.exp(a_col - m_end)
        kv = lax.dot_general(kc, (w_col * vc.astype(F32)).astype(BF16), (((0,), (0,)), ((), ())),
                             preferred_element_type=F32)
        c_new = decay * c_mat + kv
        n_new = decay * n_vec + jnp.sum(w_col * kc.astype(F32), axis=0, keepdims=True)
        return c_new, n_new, b_end + m_end

    zero = (jnp.zeros((HD, HD), F32), jnp.zeros((1, HD), F32), jnp.zeros((1, 1), F32))
    st_f = chunk(0, False, zero, hf)
    st_b = chunk(0, True, zero, hb)

    def body(j, carry):
        sf, sb = carry
        rf = pl.multiple_of(CTX + j * CHUNK, CHUNK)
        rb = pl.multiple_of(CTX + (N_LCHUNK - 1 - j) * CHUNK, CHUNK)
        return chunk(rf, False, sf, hf), chunk(rb, True, sb, hb)

    lax.fori_loop(0, N_LCHUNK, body, (st_f, st_b))

    def fin(c, carry):
        r0 = pl.multiple_of(c * CHUNK, CHUNK)
        gate = jax.nn.sigmoid(og_ref[pl.ds(r0, CHUNK), :].astype(F32))
        y_ref[pl.ds(r0, CHUNK), :] = (gate * (hf[pl.ds(r0, CHUNK), :] + hb[pl.ds(r0, CHUNK), :])).astype(BF16)
        return carry

    lax.fori_loop(0, SEG // CHUNK, fin, 0)


def mlstm(z, g_col, g_row, b_gates_l, conv_l):
    return pl.pallas_call(
        _mlstm_kernel,
        grid=(BATCH, B_HEADS),
        in_specs=[pl.BlockSpec(memory_space=pltpu.SMEM),
                  pl.BlockSpec((SEG, HD), lambda b, h: (b, COL_BQ + h)),
                  pl.BlockSpec((SEG, HD), lambda b, h: (b, COL_BK + h)),
                  pl.BlockSpec((SEG, HD), lambda b, h: (b, COL_BV + h)),
                  pl.BlockSpec((SEG, HD), lambda b, h: (b, COL_BO + h)),
                  pl.BlockSpec((1, SEG, 4), lambda b, h: (h, b, 0)),
                  pl.BlockSpec((4 * B_HEADS, SEG), lambda b, h: (0, b)),
                  pl.BlockSpec((3, HD), lambda b, h: (0, h)),
                  pl.BlockSpec((3, HD), lambda b, h: (0, B_HEADS + h))],
        out_specs=pl.BlockSpec((SEG, HD), lambda b, h: (b, h)),
        out_shape=jax.ShapeDtypeStruct((T, B_HEADS * HD), BF16),
        scratch_shapes=[pltpu.VMEM((SEG, HD), BF16), pltpu.VMEM((SEG, HD), BF16),
                        pltpu.VMEM((CONV_ROWS, HD), F32),
                        pltpu.VMEM((SEG, HD), F32), pltpu.VMEM((SEG, HD), F32)],
        name="mlstm",
        compiler_params=_cparams(("arbitrary", "arbitrary")),
    )(b_gates_l, z, z, z, z, g_col, g_row, conv_l, conv_l)


OUT_TM = SEG // 6


def _top2_routing(biased, scores, comb_ref, sel_ref):
    rows = [biased[e:e + 1, :] for e in range(N_EXPERTS)]
    gscore = []
    for g in range(N_GROUPS):
        v0, v1, v2, v3 = rows[4 * g:4 * g + 4]
        hi01, lo01 = jnp.maximum(v0, v1), jnp.minimum(v0, v1)
        hi23, lo23 = jnp.maximum(v2, v3), jnp.minimum(v2, v3)
        gscore.append(jnp.maximum(hi01, hi23) + jnp.maximum(jnp.minimum(hi01, hi23), jnp.maximum(lo01, lo23)))
    picked = []
    total = None
    for e in range(N_EXPERTS):
        g = e // PER_GROUP
        ok = None
        for o in range(N_GROUPS):
            if o != g:
                c = (gscore[g] > gscore[o]) if o < g else (gscore[g] >= gscore[o])
                ok = c if ok is None else (ok & c)
        ahead = jnp.zeros_like(rows[e])
        for o in range(4 * g, 4 * g + 4):
            if o != e:
                c = (rows[o] > rows[e]) if o > e else (rows[o] >= rows[e])
                ahead = ahead + jnp.where(c, 1.0, 0.0)
        sel = jnp.where(ok & (ahead < 2.0), 1.0, 0.0)
        sel_ref[e:e + 1, :] = sel
        picked.append(sel * scores[e:e + 1, :])
        total = picked[-1] if total is None else total + picked[-1]
    for e in range(N_EXPERTS):
        comb_ref[e:e + 1, :] = picked[e] / total


OUT_RC = 16


def _out_kernel(ya_ref, yb_ref, yn_ref, w_ref, x_ref, mod_ref, g_ref, wrh_ref, wrl_ref, br_ref,
                x1_ref, h2_ref, comb_ref, sel_ref, ycat, acc, lg):
    i = pl.program_id(0)
    batch = i // 6
    ycat[:, 0:A_HEADS * HD] = ya_ref[...]
    ycat[:, A_HEADS * HD:(A_HEADS + B_HEADS) * HD] = yb_ref[...]
    ycat[:, (A_HEADS + B_HEADS) * HD:D] = yn_ref[...]
    acc[...] = jnp.dot(ycat[...], w_ref[...], preferred_element_type=F32)

    def chunk(c, carry):
        r0 = pl.multiple_of(c * OUT_RC, OUT_RC)
        mrow = jnp.where((i % 6) * OUT_TM + r0 < CTX, 4, batch)

        def mod(k):
            return mod_ref[pl.ds(mrow, 1), k * D:(k + 1) * D]

        x1 = x_ref[pl.ds(r0, OUT_RC), :] + mod(2) * acc[pl.ds(r0, OUT_RC), :]
        x1_ref[pl.ds(r0, OUT_RC), :] = x1
        h2 = (_rms(x1) * g_ref[...]) * (1.0 + mod(4)) + mod(3)
        h2_ref[pl.ds(r0, OUT_RC), :] = h2
        hi = h2.astype(BF16)
        lo = (h2 - hi.astype(F32)).astype(BF16)
        lg[pl.ds(r0, OUT_RC), :] = (jnp.dot(hi, wrh_ref[...], preferred_element_type=F32)
                                    + jnp.dot(hi, wrl_ref[...], preferred_element_type=F32)
                                    + jnp.dot(lo, wrh_ref[...], preferred_element_type=F32))
        return carry

    lax.fori_loop(0, OUT_TM // OUT_RC, chunk, 0)
    scores = jax.nn.sigmoid(lg[...].T[0:N_EXPERTS, :])
    _top2_routing(scores + br_ref[...], scores, comb_ref, sel_ref)


def out_proj(ya, yb, yn, w_out_b, xa, mod_l, g, w_router_hi, w_router_lo, b_router_c):
    row = lambda i: (i, 0)
    fixed = lambda i: (0, 0)
    return pl.pallas_call(
        _out_kernel,
        grid=(T // OUT_TM,),
        in_specs=[pl.BlockSpec((OUT_TM, A_HEADS * HD), row),
                  pl.BlockSpec((OUT_TM, B_HEADS * HD), row),
                  pl.BlockSpec((OUT_TM, C_HEADS * HD), row),
                  pl.BlockSpec((D, D), fixed),
                  pl.BlockSpec((OUT_TM, D), row),
                  pl.BlockSpec((8, 6 * D), fixed),
                  pl.BlockSpec((1, D), fixed),
                  pl.BlockSpec((D, HD), fixed),
                  pl.BlockSpec((D, HD), fixed),
                  pl.BlockSpec((N_EXPERTS, 1), fixed)],
        out_specs=[pl.BlockSpec((OUT_TM, D), row),
                   pl.BlockSpec((OUT_TM, D), row),
                   pl.BlockSpec((N_EXPERTS, OUT_TM), lambda i: (0, i)),
                   pl.BlockSpec((N_EXPERTS, OUT_TM), lambda i: (0, i))],
        out_shape=[jax.ShapeDtypeStruct((T, D), F32),
                   jax.ShapeDtypeStruct((T, D), F32),
                   jax.ShapeDtypeStruct((N_EXPERTS, T), F32),
                   jax.ShapeDtypeStruct((N_EXPERTS, T), F32)],
        scratch_shapes=[pltpu.VMEM((OUT_TM, D), BF16), pltpu.VMEM((OUT_TM, D), F32),
                        pltpu.VMEM((OUT_TM, HD), F32)],
        name="out_proj",
        compiler_params=_cparams(("arbitrary",)),
    )(ya, yb, yn, w_out_b, xa, mod_l, g, w_router_hi, w_router_lo, b_router_c)


SC_TM = 512
ROW_DMA_UNROLL = 8


def _scatter_kernel(pos0_ref, pos1_ref, ztile_ref, h_ref, hs_ref, zero_scr, sem, zsem):
    i = pl.program_id(0)

    @pl.when(i == 0)
    def _():
        zero_scr[...] = jnp.zeros((ETILE, D), F32)

        def fill_copy(t):
            return pltpu.make_async_copy(zero_scr, hs_ref.at[pl.ds(pl.multiple_of(t * ETILE, ETILE), ETILE), :], zsem)

        def fill_start(e, c):
            @pl.when(ztile_ref[e] >= 0)
            def _():
                fill_copy(ztile_ref[e]).start()
            return c

        def fill_wait(e, c):
            @pl.when(ztile_ref[e] >= 0)
            def _():
                fill_copy(ztile_ref[e]).wait()
            return c

        lax.fori_loop(0, 2 * N_EXPERTS, fill_start, 0)
        lax.fori_loop(0, 2 * N_EXPERTS, fill_wait, 0)

    def row_copy(r, p):
        return pltpu.make_async_copy(h_ref.at[pl.ds(r, 1), :], hs_ref.at[pl.ds(p, 1), :], sem)

    def issue(r, c):
        t = i * SC_TM + r
        row_copy(r, pos0_ref[t]).start()
        row_copy(r, pos1_ref[t]).start()
        return c

    lax.fori_loop(0, SC_TM, issue, 0, unroll=ROW_DMA_UNROLL)

    def drain(r, c):
        row_copy(0, 0).wait()
        row_copy(0, 0).wait()
        return c

    lax.fori_loop(0, SC_TM, drain, 0, unroll=ROW_DMA_UNROLL)


def scatter_rows(pos0, pos1, ztile, h2):
    return pl.pallas_call(
        _scatter_kernel,
        grid_spec=pltpu.PrefetchScalarGridSpec(
            num_scalar_prefetch=3,
            grid=(T // SC_TM,),
            in_specs=[pl.BlockSpec((SC_TM, D), lambda i, *_: (i, 0))],
            out_specs=pl.BlockSpec(memory_space=pl.ANY),
            scratch_shapes=[pltpu.VMEM((ETILE, D), F32), pltpu.SemaphoreType.DMA, pltpu.SemaphoreType.DMA]),
        out_shape=jax.ShapeDtypeStruct((P_ROWS, D), F32),
        name="scatter_rows",
        compiler_params=_cparams(("arbitrary",)),
    )(pos0, pos1, ztile, h2)


CAST_ROWS = 256


def _cast_into(dst, src, n_rows):
    def body(c, carry):
        r0 = pl.multiple_of(c * CAST_ROWS, CAST_ROWS)
        dst[pl.ds(r0, CAST_ROWS), :] = src[pl.ds(r0, CAST_ROWS), :].astype(BF16)
        return carry

    lax.fori_loop(0, n_rows // CAST_ROWS, body, 0)


def _expert_weights(plan, j, fetch, on_ready):
    te_ref, _, first_ref, slot_ref, nxt_ref = plan

    @pl.when(j == 0)
    def _():
        for cp in fetch(te_ref[0], 0):
            cp.start()

    @pl.when(first_ref[j] == 1)
    def _():
        s = slot_ref[j]
        for cp in fetch(te_ref[j], s):
            cp.wait()

        @pl.when(nxt_ref[j] >= 0)
        def _():
            for cp in fetch(nxt_ref[j], 1 - s):
                cp.start()

        on_ready(s)


def _ffn_up_kernel(te_ref, tv_ref, first_ref, slot_ref, nxt_ref, hs_ref, w1_hbm, w3_hbm, act_ref,
                   wst, w1b, w3b, sem, *, layer):
    j = pl.program_id(0)

    def fetch(e, s):
        return (pltpu.make_async_copy(w1_hbm.at[layer, e], wst.at[s, 0], sem.at[s, 0]),
                pltpu.make_async_copy(w3_hbm.at[layer, e], wst.at[s, 1], sem.at[s, 1]))

    def on_ready(s):
        _cast_into(w1b, wst.at[s, 0], D)
        _cast_into(w3b, wst.at[s, 1], D)

    _expert_weights((te_ref, tv_ref, first_ref, slot_ref, nxt_ref), j, fetch, on_ready)

    @pl.when(tv_ref[j] > 0)
    def _():
        xb = hs_ref[...].astype(BF16)
        a = jnp.dot(xb, w1b[...], preferred_element_type=F32)
        b = jnp.dot(xb, w3b[...], preferred_element_type=F32)
        act_ref[...] = (_silu(a) * b).astype(BF16)

    @pl.when(tv_ref[j] == 0)
    def _():
        act_ref[...] = jnp.zeros((ETILE, FF), BF16)


def _ffn_down_kernel(te_ref, tv_ref, first_ref, slot_ref, nxt_ref, act_ref, w2_hbm, ys_ref, wst, w2b, sem, *, layer):
    j = pl.program_id(0)

    def fetch(e, s):
        return (pltpu.make_async_copy(w2_hbm.at[layer, e], wst.at[s], sem.at[s]),)

    def on_ready(s):
        _cast_into(w2b, wst.at[s], FF)

    _expert_weights((te_ref, tv_ref, first_ref, slot_ref, nxt_ref), j, fetch, on_ready)

    @pl.when(tv_ref[j] > 0)
    def _():
        ys_ref[...] = jnp.dot(act_ref[...], w2b[...], preferred_element_type=F32)

    @pl.when(tv_ref[j] == 0)
    def _():
        ys_ref[...] = jnp.zeros((ETILE, D), F32)


def _tile_or_first(j, te, tv, *_):
    return jnp.where(tv[j] > 0, j, 0)


def ffn_up(plan, hs, w1, w3, layer):
    return pl.pallas_call(
        functools.partial(_ffn_up_kernel, layer=layer),
        grid_spec=pltpu.PrefetchScalarGridSpec(
            num_scalar_prefetch=5,
            grid=(N_ETILES,),
            in_specs=[pl.BlockSpec((ETILE, D), lambda j, *p: (_tile_or_first(j, *p), 0)),
                      pl.BlockSpec(memory_space=pl.ANY),
                      pl.BlockSpec(memory_space=pl.ANY)],
            out_specs=pl.BlockSpec((ETILE, FF), lambda j, *p: (j, 0)),
            scratch_shapes=[pltpu.VMEM((2, 2, D, FF), F32), pltpu.VMEM((D, FF), BF16), pltpu.VMEM((D, FF), BF16),
                            pltpu.SemaphoreType.DMA((2, 2))]),
        out_shape=jax.ShapeDtypeStruct((P_ROWS, FF), BF16),
        name="ffn_up",
        compiler_params=_cparams(("arbitrary",)),
    )(*plan, hs, w1, w3)


def ffn_down(plan, act, w2, layer):
    return pl.pallas_call(
        functools.partial(_ffn_down_kernel, layer=layer),
        grid_spec=pltpu.PrefetchScalarGridSpec(
            num_scalar_prefetch=5,
            grid=(N_ETILES,),
            in_specs=[pl.BlockSpec((ETILE, FF), lambda j, *p: (j, 0)),
                      pl.BlockSpec(memory_space=pl.ANY)],
            out_specs=pl.BlockSpec((ETILE, D), lambda j, *p: (j, 0)),
            scratch_shapes=[pltpu.VMEM((2, FF, D), F32), pltpu.VMEM((FF, D), BF16), pltpu.SemaphoreType.DMA((2,))]),
        out_shape=jax.ShapeDtypeStruct((P_ROWS, D), F32),
        name="ffn_down",
        compiler_params=_cparams(("arbitrary",)),
    )(*plan, act, w2)


CB_TM = SEG // 9


def _combine_kernel(pos0_ref, pos1_ref, ys_ref, x_ref, w_ref, mod_ref, o_ref, buf, sem):
    i = pl.program_id(0)
    batch = i // 9
    is_ctx = (i % 9) == 0

    def row_copy(k, r, p):
        return pltpu.make_async_copy(ys_ref.at[pl.ds(p, 1), :], buf.at[k, pl.ds(r, 1), :], sem)

    def issue(r, c):
        t = i * CB_TM + r
        row_copy(0, r, pos0_ref[t]).start()
        row_copy(1, r, pos1_ref[t]).start()
        return c

    lax.fori_loop(0, CB_TM, issue, 0, unroll=ROW_DMA_UNROLL)

    def drain(r, c):
        row_copy(0, 0, 0).wait()
        row_copy(1, 0, 0).wait()
        return c

    lax.fori_loop(0, CB_TM, drain, 0, unroll=ROW_DMA_UNROLL)
    f = w_ref[:, 0:1] * buf[0] + w_ref[:, 1:2] * buf[1]
    o_ref[...] = x_ref[...] + _row_mod(mod_ref, 5, batch, is_ctx) * f


def combine_rows(pos0, pos1, ys, x1, w01, mod_l):
    return pl.pallas_call(
        _combine_kernel,
        grid_spec=pltpu.PrefetchScalarGridSpec(
            num_scalar_prefetch=2,
            grid=(T // CB_TM,),
            in_specs=[pl.BlockSpec(memory_space=pl.ANY),
                      pl.BlockSpec((CB_TM, D), lambda i, *_: (i, 0)),
                      pl.BlockSpec((CB_TM, 2), lambda i, *_: (i, 0)),
                      pl.BlockSpec((8, 6 * D), lambda i, *_: (0, 0))],
            out_specs=pl.BlockSpec((CB_TM, D), lambda i, *_: (i, 0)),
            scratch_shapes=[pltpu.VMEM((2, CB_TM, D), F32), pltpu.SemaphoreType.DMA]),
        out_shape=jax.ShapeDtypeStruct((T, D), F32),
        name="combine_rows",
        compiler_params=_cparams(("arbitrary",)),
    )(pos0, pos1, ys, x1, w01, mod_l)


def route_plan(comb_t, sel_t):
    sel = sel_t > 0.5
    cnt = jnp.sum(sel, axis=1).astype(I32)
    tiles = (cnt + ETILE - 1) // ETILE
    tend = jnp.cumsum(tiles)
    toff = tend - tiles
    rank = jnp.cumsum(sel.astype(I32), axis=1) - 1
    pos = toff[:, None] * ETILE + rank
    pos0 = jnp.min(jnp.where(sel, pos, P_ROWS), axis=0).astype(I32)
    pos1 = jnp.max(jnp.where(sel, pos, -1), axis=0).astype(I32)
    w0 = jnp.sum(jnp.where(sel & (pos == pos0[None]), comb_t, 0.0), axis=0)
    w1 = jnp.sum(jnp.where(sel & (pos == pos1[None]), comb_t, 0.0), axis=0)
    n_used = tend[-1]
    tidx = jnp.arange(N_ETILES, dtype=I32)
    te_raw = jnp.sum((tend[None, :] <= tidx[:, None]).astype(I32), axis=1)
    te_last = jnp.sum((tend <= n_used - 1).astype(I32))
    te = jnp.where(tidx < n_used, te_raw, te_last).astype(I32)
    te = jnp.minimum(te, N_EXPERTS - 1)
    tv = jnp.clip(cnt[te] - (tidx - toff[te]) * ETILE, 0, ETILE)
    tv = jnp.where(tidx < n_used, tv, 0).astype(I32)
    tail = n_used + jnp.arange(N_EXPERTS, dtype=I32)
    ztile = jnp.concatenate([jnp.where(tiles > 0, tend - 1, -1), jnp.where(tail < N_ETILES, tail, -1)]).astype(I32)
    used = tidx < n_used
    first = (used & ((tidx == 0) | (te != jnp.roll(te, 1)))).astype(I32)
    slot = ((jnp.cumsum(first) - 1) % 2).astype(I32)
    nxt_tile = tend[te]
    nxt = jnp.where(nxt_tile < n_used, te[jnp.minimum(nxt_tile, N_ETILES - 1)], -1).astype(I32)
    return pos0, pos1, jnp.stack([w0, w1], axis=1), (te, tv, first, slot, nxt), ztile


def _final_kernel(x_ref, g_ref, o_ref):
    o_ref[...] = _rms(x_ref[...]) * g_ref[...]


def final_norm(xa, g):
    nb = SEQ // CTX
    return pl.pallas_call(
        _final_kernel,
        grid=(BATCH, nb),
        in_specs=[pl.BlockSpec((CTX, D), lambda b, j: (b * (nb + 1) + 1 + j, 0)),
                  pl.BlockSpec((1, D), lambda b, j: (0, 0))],
        out_specs=pl.BlockSpec((CTX, D), lambda b, j: (b * nb + j, 0)),
        out_shape=jax.ShapeDtypeStruct((BATCH * SEQ, D), F32),
        name="final_norm",
        compiler_params=_cparams(("arbitrary", "arbitrary")),
    )(xa, g)


def _rope_tables():
    t = jnp.arange(SEQ, dtype=I32)
    row = (t // GRID_W).astype(F32)
    col = (t % GRID_W).astype(F32)
    n_freq = HD // 4
    inv_freq = ROPE_BASE ** (-jnp.arange(n_freq, dtype=F32) / n_freq)
    ar = row[:, None] * inv_freq[None, :]
    ac = col[:, None] * inv_freq[None, :]
    cos_t = jnp.concatenate([jnp.cos(ar), jnp.cos(ar), jnp.cos(ac), jnp.cos(ac)], axis=1)
    sin_t = jnp.concatenate([-jnp.sin(ar), jnp.sin(ar), -jnp.sin(ac), jnp.sin(ac)], axis=1)
    return cos_t, sin_t


GATE_LO, GATE_HI = 3328, 3344


def kernel(x, c, ctx, c_ctx, w_ada, b_ada, norm_mix, norm_ffn, w_in, b_gates, conv_qk, sink, rpb, w_out,
           w_router, b_router, w1, w3, w2, norm_final):
    xa = jnp.concatenate([ctx, x], axis=1).reshape(T, D)
    cond8 = jnp.concatenate([c, c_ctx[None], jnp.zeros((3, D), F32)], axis=0)
    mod = ada_all(cond8, w_ada, b_ada)
    cos_t, sin_t = _rope_tables()
    w_main = jnp.concatenate([w_in[:, :, :GATE_LO].astype(BF16), w_in[:, :, GATE_HI:].astype(BF16)], axis=2)
    w_gate = jnp.pad(w_in[:, :, GATE_LO:GATE_HI], ((0, 0), (0, 0), (0, HD - 4 * B_HEADS))).astype(BF16)
    w_out_b = w_out.astype(BF16)
    w_router_p = jnp.pad(w_router, ((0, 0), (0, HD - N_EXPERTS)))
    w_router_hi = w_router_p.astype(BF16)
    w_router_lo = (w_router_p - w_router_hi.astype(F32)).astype(BF16)
    b_router_c = b_router.reshape(N_EXPERTS, 1)

    for l in range(DEPTH):
        z, zg = in_proj(xa, norm_mix[l].reshape(1, D), mod[l], w_main[l], w_gate[l])
        gates = zg[:, :4 * B_HEADS]
        g_col = gates.reshape(T, 4, B_HEADS).transpose(2, 0, 1)
        g_row = gates.T
        ya = window_attn(z, sink[l], cos_t, sin_t)
        yb = mlstm(z, g_col, g_row, b_gates[l], conv_qk[l])
        yn = nbr_attn(z, nbr_bias_table(rpb[l]))
        x1, h2, comb_t, sel_t = out_proj(ya, yb, yn, w_out_b[l], xa, mod[l], norm_ffn[l].reshape(1, D),
                                         w_router_hi, w_router_lo, b_router_c)
        pos0, pos1, w01, plan, ztile = route_plan(comb_t, sel_t)
        hs = scatter_rows(pos0, pos1, ztile, h2)
        act = ffn_up(plan, hs, w1, w3, l)
        ys = ffn_down(plan, act, w2, l)
        xa = combine_rows(pos0, pos1, ys, x1, w01, mod[l])
    return final_norm(xa, norm_final.reshape(1, D)).reshape(BATCH, SEQ, D)
```

```python
import functools

import jax
import jax.numpy as jnp
from jax import lax
from jax.experimental import pallas as pl
from jax.experimental.pallas import tpu as pltpu

F32 = jnp.float32
BF16 = jnp.bfloat16
I32 = jnp.int32

D = 2048
BATCH = 4
SEQ = 2048
CTX = 256
SEG = CTX + SEQ
T = BATCH * SEG
DEPTH = 4
GRID_W = 64
HD = 128
A_HEADS, A_KV, A_GRP = 6, 2, 3
B_HEADS = 4
C_HEADS = 6
A_BLOCK = 128
NB_ROWS, NB_COLS = 8, 16
CHUNK = 256
N_EXPERTS, N_GROUPS, PER_GROUP = 16, 4, 4
FF = 1024
EPS = 1e-6
ROPE_BASE = 10000.0
NEG = -1e30
ATT_SCALE = HD ** -0.5

N_MAIN = 5632
COL_AQ, COL_AK, COL_AV = 0, 6, 8
COL_BQ, COL_BK, COL_BV, COL_BO = 10, 14, 18, 22
COL_CQ, COL_CK, COL_CV = 26, 32, 38

ETILE = 256
N_ETILES = (2 * T) // ETILE + N_EXPERTS
P_ROWS = N_ETILES * ETILE

VMEM_LIMIT = 56 * 1024 * 1024


def _cparams(sem):
    return pltpu.CompilerParams(dimension_semantics=sem, vmem_limit_bytes=VMEM_LIMIT)


def _silu(v):
    return v * jax.nn.sigmoid(v)


def _log_sigmoid(v):
    return jnp.minimum(v, 0.0) - jnp.log1p(jnp.exp(-jnp.abs(v)))


ADA_TN = 1024


def _ada_kernel(s_ref, w_ref, b_ref, o_ref):
    s = _silu(s_ref[...]).astype(BF16)
    o_ref[0] = jnp.dot(s, w_ref[0].astype(BF16), preferred_element_type=F32) + b_ref[0]


def ada_all(cond8, w_ada, b_ada):
    n = w_ada.shape[-1]
    return pl.pallas_call(
        _ada_kernel,
        grid=(DEPTH, n // ADA_TN),
        in_specs=[pl.BlockSpec((8, D), lambda l, j: (0, 0)),
                  pl.BlockSpec((1, D, ADA_TN), lambda l, j: (l, 0, j)),
                  pl.BlockSpec((1, 1, ADA_TN), lambda l, j: (l, 0, j))],
        out_specs=pl.BlockSpec((1, 8, ADA_TN), lambda l, j: (l, 0, j)),
        out_shape=jax.ShapeDtypeStruct((DEPTH, 8, n), F32),
        name="ada_mod",
        compiler_params=_cparams(("arbitrary", "arbitrary")),
    )(cond8, w_ada, b_ada.reshape(DEPTH, 1, n))


def _row_mod(mod_ref, chunk, batch, is_ctx):
    lat = mod_ref[pl.ds(batch, 1), chunk * D:(chunk + 1) * D]
    ctx = mod_ref[4:5, chunk * D:(chunk + 1) * D]
    return jnp.where(is_ctx, ctx, lat)


def _rms(x):
    return x * lax.rsqrt(jnp.mean(x * x, axis=-1, keepdims=True) + EPS)


IN_TM = SEG // 2
IN_TN = 1408


def _in_kernel(x_ref, g_ref, mod_ref, w_ref, wg_ref, z_ref, zg_ref, h_scr):
    i = pl.program_id(0)
    j = pl.program_id(1)

    @pl.when(j == 0)
    def _():
        batch = i // 2
        rows = lax.broadcasted_iota(I32, (IN_TM, 1), 0) + (i % 2) * IN_TM
        is_ctx = rows < CTX
        xn = _rms(x_ref[...]) * g_ref[...]
        h = xn * (1.0 + _row_mod(mod_ref, 1, batch, is_ctx)) + _row_mod(mod_ref, 0, batch, is_ctx)
        hb = h.astype(BF16)
        h_scr[...] = hb
        zg_ref[...] = jnp.dot(hb, wg_ref[...], preferred_element_type=F32)

    z_ref[...] = jnp.dot(h_scr[...], w_ref[...], preferred_element_type=F32).astype(BF16)


def in_proj(xa, g, mod_l, w_main, w_gate):
    return pl.pallas_call(
        _in_kernel,
        grid=(T // IN_TM, N_MAIN // IN_TN),
        in_specs=[pl.BlockSpec((IN_TM, D), lambda i, j: (i, 0)),
                  pl.BlockSpec((1, D), lambda i, j: (0, 0)),
                  pl.BlockSpec((8, 6 * D), lambda i, j: (0, 0)),
                  pl.BlockSpec((D, IN_TN), lambda i, j: (0, j)),
                  pl.BlockSpec((D, HD), lambda i, j: (0, 0))],
        out_specs=[pl.BlockSpec((IN_TM, IN_TN), lambda i, j: (i, j)),
                   pl.BlockSpec((IN_TM, HD), lambda i, j: (i, 0))],
        out_shape=[jax.ShapeDtypeStruct((T, N_MAIN), BF16),
                   jax.ShapeDtypeStruct((T, HD), F32)],
        scratch_shapes=[pltpu.VMEM((IN_TM, D), BF16)],
        name="in_proj",
        compiler_params=_cparams(("arbitrary", "arbitrary")),
    )(xa, g, mod_l, w_main, w_gate)


NBLK = SEQ // A_BLOCK
KPAD = SEG + A_BLOCK


def _dot_nt(a, b):
    return lax.dot_general(a, b, (((1,), (1,)), ((), ())), preferred_element_type=F32)


def _attend(scores, values, extra=None):
    m = scores[0].max(axis=-1, keepdims=True)
    for s in scores[1:]:
        m = jnp.maximum(m, s.max(axis=-1, keepdims=True))
    if extra is not None:
        m = jnp.maximum(m, extra)
    den = None if extra is None else jnp.exp(extra - m)
    out = None
    for s, v in zip(scores, values):
        e = jnp.exp(s - m)
        d = e.sum(axis=-1, keepdims=True)
        den = d if den is None else den + d
        o = jnp.dot(e.astype(BF16), v, preferred_element_type=F32)
        out = o if out is None else out + o
    return out / den


def _win_kernel(sink_ref, q_ref, k_ref, v_ref, cos_ref, sin_ref, o_ref, qs, ks, vs):
    kv = pl.program_id(1)
    lane = lax.broadcasted_iota(I32, (A_BLOCK, HD), 1)
    first_half = (lane % 64) < 32

    def rope(zf, cos, sin):
        zr = jnp.where(first_half, pltpu.roll(zf, 96, 1), pltpu.roll(zf, 32, 1))
        return zf * cos + zr * sin

    ks[0:CTX, :] = k_ref[0:CTX, :]
    ks[SEG:KPAD, :] = jnp.zeros((A_BLOCK, HD), BF16)
    vs[0:SEG, :] = v_ref[...]
    vs[SEG:KPAD, :] = jnp.zeros((A_BLOCK, HD), BF16)

    def rope_blk(i, c):
        r0 = pl.multiple_of(i * A_BLOCK, A_BLOCK)
        cos = cos_ref[pl.ds(r0, A_BLOCK), :]
        sin = sin_ref[pl.ds(r0, A_BLOCK), :]
        ks[pl.ds(CTX + r0, A_BLOCK), :] = rope(k_ref[pl.ds(CTX + r0, A_BLOCK), :].astype(F32), cos, sin).astype(BF16)
        for g in range(A_GRP):
            zf = q_ref[pl.ds(CTX + r0, A_BLOCK), g * HD:(g + 1) * HD].astype(F32)
            qs[i, g * A_BLOCK:(g + 1) * A_BLOCK, :] = rope(zf, cos, sin).astype(BF16)
        return c

    lax.fori_loop(0, NBLK, rope_blk, 0)

    nq = A_GRP * A_BLOCK
    row = lax.broadcasted_iota(I32, (nq, 1), 0)
    sink = jnp.where(row < A_BLOCK, sink_ref[kv * A_GRP],
                     jnp.where(row < 2 * A_BLOCK, sink_ref[kv * A_GRP + 1], sink_ref[kv * A_GRP + 2]))
    r = lax.broadcasted_iota(I32, (nq, 3 * A_BLOCK), 0) % A_BLOCK
    c = lax.broadcasted_iota(I32, (nq, 3 * A_BLOCK), 1)
    band = (c >= r) & (c <= r + 2 * A_BLOCK)

    def blk(i, carry):
        w0 = pl.multiple_of(CTX - A_BLOCK + i * A_BLOCK, A_BLOCK)
        q = qs[i]
        s_c = _dot_nt(q, ks[0:CTX, :]) * ATT_SCALE
        s_w = _dot_nt(q, ks[pl.ds(w0, 3 * A_BLOCK), :]) * ATT_SCALE
        kpos = (i - 1) * A_BLOCK + c
        s_w = jnp.where(band & (kpos >= 0) & (kpos < SEQ), s_w, NEG)
        o = _attend([s_c, s_w], [vs[0:CTX, :], vs[pl.ds(w0, 3 * A_BLOCK), :]], sink)
        o0 = pl.multiple_of(CTX + i * A_BLOCK, A_BLOCK)
        for g in range(A_GRP):
            o_ref[pl.ds(o0, A_BLOCK), g * HD:(g + 1) * HD] = o[g * A_BLOCK:(g + 1) * A_BLOCK].astype(BF16)
        return carry

    lax.fori_loop(0, NBLK, blk, 0)

    for g in range(A_GRP):
        s = _dot_nt(q_ref[0:CTX, g * HD:(g + 1) * HD], k_ref[0:CTX, :]) * ATT_SCALE
        o = _attend([s], [v_ref[0:CTX, :]], jnp.full((CTX, 1), sink_ref[kv * A_GRP + g], F32))
        o_ref[0:CTX, g * HD:(g + 1) * HD] = o.astype(BF16)


def window_attn(z, sink_l, cos_t, sin_t):
    return pl.pallas_call(
        _win_kernel,
        grid=(BATCH, A_KV),
        in_specs=[pl.BlockSpec(memory_space=pltpu.SMEM),
                  pl.BlockSpec((SEG, A_GRP * HD), lambda b, kv: (b, kv)),
                  pl.BlockSpec((SEG, HD), lambda b, kv: (b, COL_AK + kv)),
                  pl.BlockSpec((SEG, HD), lambda b, kv: (b, COL_AV + kv)),
                  pl.BlockSpec((SEQ, HD), lambda b, kv: (0, 0)),
                  pl.BlockSpec((SEQ, HD), lambda b, kv: (0, 0))],
        out_specs=pl.BlockSpec((SEG, A_GRP * HD), lambda b, kv: (b, kv)),
        out_shape=jax.ShapeDtypeStruct((T, A_HEADS * HD), BF16),
        scratch_shapes=[pltpu.VMEM((NBLK, A_GRP * A_BLOCK, HD), BF16),
                        pltpu.VMEM((KPAD, HD), BF16),
                        pltpu.VMEM((KPAD, HD), BF16)],
        name="window_attn",
        compiler_params=_cparams(("arbitrary", "arbitrary")),
    )(sink_l, z, z, z, cos_t, sin_t)


GRID_ROWS = SEQ // GRID_W
NBQ_ROWS = 4
NBK_ROWS = 12
NBQ, NBK = NBQ_ROWS * GRID_W, NBK_ROWS * GRID_W
N_NBLK = GRID_ROWS // NBQ_ROWS
NB_BASE_MAX = GRID_ROWS - NBK_ROWS


def _nbr_key_base(first_row):
    return jnp.clip(first_row - NB_ROWS // 2, 0, NB_BASE_MAX)


N_DR, N_DC = 2 * NB_ROWS - 1, 2 * NB_COLS - 1


def _nbr_block_offsets():
    out = []
    for first_row in (0, NBQ_ROWS, GRID_ROWS - NBQ_ROWS):
        base = min(max(first_row - NB_ROWS // 2, 0), NB_BASE_MAX)
        kind = []
        for qi in range(NBQ_ROWS):
            r = first_row + qi
            start = min(max(r - NB_ROWS // 2, 0), GRID_ROWS - NB_ROWS)
            kind.append([kr - r + NB_ROWS - 1 if start <= kr < start + NB_ROWS else N_DR
                         for kr in range(base, base + NBK_ROWS)])
        out.append(kind)
    return out


def _nbr_build_bias(rpb_ref, h, half, bias):
    qc = lax.broadcasted_iota(I32, (GRID_W, 2 * GRID_W), 0)
    lane = lax.broadcasted_iota(I32, (GRID_W, 2 * GRID_W), 1)
    kc = lane % GRID_W
    start_c = jnp.clip(qc - NB_COLS // 2, 0, GRID_W - NB_COLS)
    col_ok = (kc >= start_c) & (kc < start_c + NB_COLS)
    dc = kc - qc + NB_COLS - 1
    left = lane < GRID_W
    for d in range(N_DR):
        t = jnp.full((GRID_W, 2 * GRID_W), NEG, F32)
        for j in range(N_DC):
            t = jnp.where(dc == j, rpb_ref[(h * N_DR + d) * N_DC + j], t)
        t = jnp.where(col_ok, t, NEG)
        half[d, 0] = jnp.where(left, t, NEG)
        half[d, 1] = jnp.where(left, NEG, t)
    masked = jnp.full((GRID_W, 2 * GRID_W), NEG, F32)
    for kind, per_q in enumerate(_nbr_block_offsets()):
        for qi, dr in enumerate(per_q):
            for p in range(NBK_ROWS // 2):
                lo = half[dr[2 * p], 0] if dr[2 * p] < N_DR else masked
                hi = half[dr[2 * p + 1], 1] if dr[2 * p + 1] < N_DR else masked
                bias[kind, qi * GRID_W:(qi + 1) * GRID_W, p * 2 * GRID_W:(p + 1) * 2 * GRID_W] = jnp.maximum(lo, hi)


def _nbr_kernel(rpb_ref, q_ref, k_ref, v_ref, o_ref, half, bias):
    @pl.when(pl.program_id(1) == 0)
    def _():
        _nbr_build_bias(rpb_ref, pl.program_id(0), half, bias)

    def block(i):
        kind = jnp.where(i == 0, 0, jnp.where(i == N_NBLK - 1, 2, 1))
        q0 = pl.multiple_of(CTX + i * NBQ, NBQ)
        k0 = pl.multiple_of(CTX + _nbr_key_base(i * NBQ_ROWS) * GRID_W, GRID_W)
        q = q_ref[pl.ds(q0, NBQ), :]
        s_c = _dot_nt(q, k_ref[0:CTX, :]) * ATT_SCALE
        s_n = _dot_nt(q, k_ref[pl.ds(k0, NBK), :]) * ATT_SCALE + bias[kind]
        o = _attend([s_c, s_n], [v_ref[0:CTX, :], v_ref[pl.ds(k0, NBK), :]])
        o_ref[pl.ds(q0, NBQ), :] = o.astype(BF16)

    def blocks(i, carry):
        block(2 * i)
        block(2 * i + 1)
        return carry

    lax.fori_loop(0, N_NBLK // 2, blocks, 0)

    s = _dot_nt(q_ref[0:CTX, :], k_ref[0:CTX, :]) * ATT_SCALE
    o_ref[0:CTX, :] = _attend([s], [v_ref[0:CTX, :]]).astype(BF16)


def nbr_attn(z, rpb_l):
    return pl.pallas_call(
        _nbr_kernel,
        grid=(C_HEADS, BATCH),
        in_specs=[pl.BlockSpec(memory_space=pltpu.SMEM),
                  pl.BlockSpec((SEG, HD), lambda h, b: (b, COL_CQ + h)),
                  pl.BlockSpec((SEG, HD), lambda h, b: (b, COL_CK + h)),
                  pl.BlockSpec((SEG, HD), lambda h, b: (b, COL_CV + h))],
        out_specs=pl.BlockSpec((SEG, HD), lambda h, b: (b, h)),
        out_shape=jax.ShapeDtypeStruct((T, C_HEADS * HD), BF16),
        scratch_shapes=[pltpu.VMEM((N_DR, 2, GRID_W, 2 * GRID_W), F32), pltpu.VMEM((3, NBQ, NBK), F32)],
        name="nbr_attn",
        compiler_params=_cparams(("arbitrary", "arbitrary")),
    )(rpb_l.reshape(-1), z, z, z)


N_LCHUNK = SEQ // CHUNK
PADR = 8
CONV_ROWS = PADR + CTX + PADR + SEQ + PADR


def _conv_off(r0):
    return jnp.where(r0 < CTX, r0 + PADR, r0 + 2 * PADR)


def _mlstm_kernel(bg_ref, q_ref, k_ref, v_ref, og_ref, gr_ref, cwq_ref, cwk_ref, y_ref,
                  qs, ks, stage, hf, hb):
    h = pl.program_id(1)
    rowi = lax.broadcasted_iota(I32, (CHUNK, 1), 0)

    def conv_silu(src_ref, cw_ref, dst, post):
        stage[...] = jnp.zeros((CONV_ROWS, HD), F32)
        stage[PADR:PADR + CTX, :] = src_ref[0:CTX, :].astype(F32)
        stage[2 * PADR + CTX:2 * PADR + SEG, :] = src_ref[CTX:SEG, :].astype(F32)

        def blk(c, carry):
            r0 = pl.multiple_of(c * CHUNK, CHUNK)
            s0 = pl.multiple_of(_conv_off(r0), PADR)
            cur = stage[pl.ds(s0, CHUNK), :]
            prev_last = stage[pl.ds(s0 - PADR, PADR), :][PADR - 1:PADR, :]
            next_first = stage[pl.ds(s0 + CHUNK, PADR), :][0:1, :]
            zm = jnp.where(rowi == 0, prev_last, pltpu.roll(cur, 1, 0))
            zp = jnp.where(rowi == CHUNK - 1, next_first, pltpu.roll(cur, CHUNK - 1, 0))
            y = zm * cw_ref[0:1, :] + cur * cw_ref[1:2, :] + zp * cw_ref[2:3, :]
            dst[pl.ds(r0, CHUNK), :] = (_silu(y) * post).astype(BF16)
            return carry

        lax.fori_loop(0, SEG // CHUNK, blk, 0)

    conv_silu(q_ref, cwq_ref, qs, 1.0)
    conv_silu(k_ref, cwk_ref, ks, ATT_SCALE)

    t_idx = lax.broadcasted_iota(I32, (CHUNK, CHUNK), 0)
    s_idx = lax.broadcasted_iota(I32, (CHUNK, CHUNK), 1)

    def chunk(r0, bwd, state, dst):
        c_mat, n_vec, m_prev = state
        ki, kf = (2, 3) if bwd else (0, 1)
        bi = bg_ref[ki * B_HEADS + h]
        bf = bg_ref[kf * B_HEADS + h]
        qc = qs[pl.ds(r0, CHUNK), :]
        kc = ks[pl.ds(r0, CHUNK), :]
        vc = v_ref[pl.ds(r0, CHUNK), :]
        i_row = gr_ref[pl.ds(ki * B_HEADS + h, 1), pl.ds(r0, CHUNK)] + bi
        lf_row = _log_sigmoid(gr_ref[pl.ds(kf * B_HEADS + h, 1), pl.ds(r0, CHUNK)] + bf)
        causal = (s_idx >= t_idx) if bwd else (s_idx <= t_idx)
        diag = s_idx == t_idx
        b_col = jnp.sum(jnp.where(causal, lf_row, 0.0), axis=1, keepdims=True)
        b_row = jnp.sum(jnp.where(diag, b_col, 0.0), axis=0, keepdims=True)
        a_row = i_row - b_row
        a_col = jnp.sum(jnp.where(diag, a_row, 0.0), axis=1, keepdims=True)
        mx = jnp.maximum(m_prev, jnp.max(jnp.where(causal, a_row, NEG), axis=1, keepdims=True))
        dm = jnp.exp(jnp.where(causal, a_row - mx, NEG))
        s = _dot_nt(qc, kc) * dm
        w_int = jnp.exp(m_prev - mx)
        num = (jnp.dot(s.astype(BF16), vc, preferred_element_type=F32)
               + w_int * jnp.dot(qc, c_mat.astype(BF16), preferred_element_type=F32))
        den = (jnp.sum(s, axis=1, keepdims=True)
               + w_int * jnp.sum(qc.astype(F32) * n_vec, axis=1, keepdims=True))
        m_t = b_col + mx
        dst[pl.ds(r0, CHUNK), :] = num / jnp.maximum(jnp.abs(den), jnp.exp(-m_t))
        b_end = jnp.sum(lf_row, axis=1, keepdims=True)
        m_end = jnp.maximum(m_prev, jnp.max(a_row, axis=1, keepdims=True))
        decay = jnp.exp(m_prev - m_end)
        w_col = jnp.exp(a_col - m_end)
        kv = lax.dot_general(kc, (w_col * vc.astype(F32)).astype(BF16), (((0,), (0,)), ((), ())),
                             preferred_element_type=F32)
        c_new = decay * c_mat + kv
        n_new = decay * n_vec + jnp.sum(w_col * kc.astype(F32), axis=0, keepdims=True)
        return c_new, n_new, b_end + m_end

    zero = (jnp.zeros((HD, HD), F32), jnp.zeros((1, HD), F32), jnp.zeros((1, 1), F32))
    st_f = chunk(0, False, zero, hf)
    st_b = chunk(0, True, zero, hb)

    def body(j, carry):
        sf, sb = carry
        rf = pl.multiple_of(CTX + j * CHUNK, CHUNK)
        rb = pl.multiple_of(CTX + (N_LCHUNK - 1 - j) * CHUNK, CHUNK)
        return chunk(rf, False, sf, hf), chunk(rb, True, sb, hb)

    lax.fori_loop(0, N_LCHUNK, body, (st_f, st_b))

    def fin(c, carry):
        r0 = pl.multiple_of(c * CHUNK, CHUNK)
        gate = jax.nn.sigmoid(og_ref[pl.ds(r0, CHUNK), :].astype(F32))
        y_ref[pl.ds(r0, CHUNK), :] = (gate * (hf[pl.ds(r0, CHUNK), :] + hb[pl.ds(r0, CHUNK), :])).astype(BF16)
        return carry

    lax.fori_loop(0, SEG // CHUNK, fin, 0)


def mlstm(z, g_row, b_gates_l, conv_l):
    return pl.pallas_call(
        _mlstm_kernel,
        grid=(BATCH, B_HEADS),
        in_specs=[pl.BlockSpec(memory_space=pltpu.SMEM),
                  pl.BlockSpec((SEG, HD), lambda b, h: (b, COL_BQ + h)),
                  pl.BlockSpec((SEG, HD), lambda b, h: (b, COL_BK + h)),
                  pl.BlockSpec((SEG, HD), lambda b, h: (b, COL_BV + h)),
                  pl.BlockSpec((SEG, HD), lambda b, h: (b, COL_BO + h)),
                  pl.BlockSpec((4 * B_HEADS, SEG), lambda b, h: (0, b)),
                  pl.BlockSpec((3, HD), lambda b, h: (0, h)),
                  pl.BlockSpec((3, HD), lambda b, h: (0, B_HEADS + h))],
        out_specs=pl.BlockSpec((SEG, HD), lambda b, h: (b, h)),
        out_shape=jax.ShapeDtypeStruct((T, B_HEADS * HD), BF16),
        scratch_shapes=[pltpu.VMEM((SEG, HD), BF16), pltpu.VMEM((SEG, HD), BF16),
                        pltpu.VMEM((CONV_ROWS, HD), F32),
                        pltpu.VMEM((SEG, HD), F32), pltpu.VMEM((SEG, HD), F32)],
        name="mlstm",
        compiler_params=_cparams(("arbitrary", "arbitrary")),
    )(b_gates_l, z, z, z, z, g_row, conv_l, conv_l)


OUT_TM = SEG // 6


def _top2_routing(biased, scores, comb_ref, sel_ref):
    rows = [biased[e:e + 1, :] for e in range(N_EXPERTS)]
    gscore = []
    for g in range(N_GROUPS):
        v0, v1, v2, v3 = rows[4 * g:4 * g + 4]
        hi01, lo01 = jnp.maximum(v0, v1), jnp.minimum(v0, v1)
        hi23, lo23 = jnp.maximum(v2, v3), jnp.minimum(v2, v3)
        gscore.append(jnp.maximum(hi01, hi23) + jnp.maximum(jnp.minimum(hi01, hi23), jnp.maximum(lo01, lo23)))
    picked = []
    total = None
    for e in range(N_EXPERTS):
        g = e // PER_GROUP
        ok = None
        for o in range(N_GROUPS):
            if o != g:
                c = (gscore[g] > gscore[o]) if o < g else (gscore[g] >= gscore[o])
                ok = c if ok is None else (ok & c)
        ahead = jnp.zeros_like(rows[e])
        for o in range(4 * g, 4 * g + 4):
            if o != e:
                c = (rows[o] > rows[e]) if o > e else (rows[o] >= rows[e])
                ahead = ahead + jnp.where(c, 1.0, 0.0)
        sel = jnp.where(ok & (ahead < 2.0), 1.0, 0.0)
        sel_ref[e:e + 1, :] = sel
        picked.append(sel * scores[e:e + 1, :])
        total = picked[-1] if total is None else total + picked[-1]
    for e in range(N_EXPERTS):
        comb_ref[e:e + 1, :] = picked[e] / total


OUT_RC = 16


def _out_kernel(ya_ref, yb_ref, yn_ref, w_ref, x_ref, mod_ref, g_ref, wrh_ref, wrl_ref, br_ref,
                x1_ref, h2_ref, comb_ref, sel_ref, ycat, acc, h_hi, h_lo):
    i = pl.program_id(0)
    batch = i // 6
    ycat[:, 0:A_HEADS * HD] = ya_ref[...]
    ycat[:, A_HEADS * HD:(A_HEADS + B_HEADS) * HD] = yb_ref[...]
    ycat[:, (A_HEADS + B_HEADS) * HD:D] = yn_ref[...]
    acc[...] = jnp.dot(ycat[...], w_ref[...], preferred_element_type=F32)

    def chunk(c, carry):
        r0 = pl.multiple_of(c * OUT_RC, OUT_RC)
        mrow = jnp.where((i % 6) * OUT_TM + r0 < CTX, 4, batch)

        def mod(k):
            return mod_ref[pl.ds(mrow, 1), k * D:(k + 1) * D]

        x1 = x_ref[pl.ds(r0, OUT_RC), :] + mod(2) * acc[pl.ds(r0, OUT_RC), :]
        x1_ref[pl.ds(r0, OUT_RC), :] = x1
        h2 = (_rms(x1) * g_ref[...]) * (1.0 + mod(4)) + mod(3)
        h2_ref[pl.ds(r0, OUT_RC), :] = h2
        hi = h2.astype(BF16)
        h_hi[pl.ds(r0, OUT_RC), :] = hi
        h_lo[pl.ds(r0, OUT_RC), :] = (h2 - hi.astype(F32)).astype(BF16)
        return carry

    lax.fori_loop(0, OUT_TM // OUT_RC, chunk, 0, unroll=2)
    scores = jax.nn.sigmoid(_dot_nt(wrh_ref[...], h_hi[...]) + _dot_nt(wrl_ref[...], h_hi[...])
                            + _dot_nt(wrh_ref[...], h_lo[...]))
    _top2_routing(scores + br_ref[...], scores, comb_ref, sel_ref)


def out_proj(ya, yb, yn, w_out_b, xa, mod_l, g, w_router_hi, w_router_lo, b_router_c):
    row = lambda i: (i, 0)
    fixed = lambda i: (0, 0)
    return pl.pallas_call(
        _out_kernel,
        grid=(T // OUT_TM,),
        in_specs=[pl.BlockSpec((OUT_TM, A_HEADS * HD), row),
                  pl.BlockSpec((OUT_TM, B_HEADS * HD), row),
                  pl.BlockSpec((OUT_TM, C_HEADS * HD), row),
                  pl.BlockSpec((D, D), fixed),
                  pl.BlockSpec((OUT_TM, D), row),
                  pl.BlockSpec((8, 6 * D), fixed),
                  pl.BlockSpec((1, D), fixed),
                  pl.BlockSpec((N_EXPERTS, D), fixed),
                  pl.BlockSpec((N_EXPERTS, D), fixed),
                  pl.BlockSpec((N_EXPERTS, 1), fixed)],
        out_specs=[pl.BlockSpec((OUT_TM, D), row),
                   pl.BlockSpec((OUT_TM, D), row),
                   pl.BlockSpec((N_EXPERTS, OUT_TM), lambda i: (0, i)),
                   pl.BlockSpec((N_EXPERTS, OUT_TM), lambda i: (0, i))],
        out_shape=[jax.ShapeDtypeStruct((T, D), F32),
                   jax.ShapeDtypeStruct((T, D), F32),
                   jax.ShapeDtypeStruct((N_EXPERTS, T), F32),
                   jax.ShapeDtypeStruct((N_EXPERTS, T), F32)],
        scratch_shapes=[pltpu.VMEM((OUT_TM, D), BF16), pltpu.VMEM((OUT_TM, D), F32),
                        pltpu.VMEM((OUT_TM, D), BF16), pltpu.VMEM((OUT_TM, D), BF16)],
        name="out_proj",
        compiler_params=_cparams(("arbitrary",)),
    )(ya, yb, yn, w_out_b, xa, mod_l, g, w_router_hi, w_router_lo, b_router_c)


SC_TM = 512
ROW_DMA_UNROLL = 8


def _scatter_kernel(pos0_ref, pos1_ref, ztile_ref, h_ref, hs_ref, zero_scr, sem, zsem):
    i = pl.program_id(0)

    @pl.when(i == 0)
    def _():
        zero_scr[...] = jnp.zeros((ETILE, D), F32)

        def fill_copy(t):
            return pltpu.make_async_copy(zero_scr, hs_ref.at[pl.ds(pl.multiple_of(t * ETILE, ETILE), ETILE), :], zsem)

        def fill_start(e, c):
            @pl.when(ztile_ref[e] >= 0)
            def _():
                fill_copy(ztile_ref[e]).start()
            return c

        def fill_wait(e, c):
            @pl.when(ztile_ref[e] >= 0)
            def _():
                fill_copy(ztile_ref[e]).wait()
            return c

        lax.fori_loop(0, 2 * N_EXPERTS, fill_start, 0)
        lax.fori_loop(0, 2 * N_EXPERTS, fill_wait, 0)

    def row_copy(r, p):
        return pltpu.make_async_copy(h_ref.at[pl.ds(r, 1), :], hs_ref.at[pl.ds(p, 1), :], sem)

    def issue(r, c):
        t = i * SC_TM + r
        row_copy(r, pos0_ref[t]).start()
        row_copy(r, pos1_ref[t]).start()
        return c

    lax.fori_loop(0, SC_TM, issue, 0, unroll=ROW_DMA_UNROLL)

    def drain(r, c):
        row_copy(0, 0).wait()
        row_copy(0, 0).wait()
        return c

    lax.fori_loop(0, SC_TM, drain, 0, unroll=ROW_DMA_UNROLL)


def scatter_rows(pos0, pos1, ztile, h2):
    return pl.pallas_call(
        _scatter_kernel,
        grid_spec=pltpu.PrefetchScalarGridSpec(
            num_scalar_prefetch=3,
            grid=(T // SC_TM,),
            in_specs=[pl.BlockSpec((SC_TM, D), lambda i, *_: (i, 0))],
            out_specs=pl.BlockSpec(memory_space=pl.ANY),
            scratch_shapes=[pltpu.VMEM((ETILE, D), F32), pltpu.SemaphoreType.DMA, pltpu.SemaphoreType.DMA]),
        out_shape=jax.ShapeDtypeStruct((P_ROWS, D), F32),
        name="scatter_rows",
        compiler_params=_cparams(("arbitrary",)),
    )(pos0, pos1, ztile, h2)


CAST_ROWS = 256


def _cast_into(dst, src, n_rows):
    def body(c, carry):
        r0 = pl.multiple_of(c * CAST_ROWS, CAST_ROWS)
        dst[pl.ds(r0, CAST_ROWS), :] = src[pl.ds(r0, CAST_ROWS), :].astype(BF16)
        return carry

    lax.fori_loop(0, n_rows // CAST_ROWS, body, 0)


def _expert_weights(plan, j, fetch, on_ready):
    te_ref, _, first_ref, slot_ref, nxt_ref = plan

    @pl.when(j == 0)
    def _():
        for cp in fetch(te_ref[0], 0):
            cp.start()

    @pl.when(first_ref[j] == 1)
    def _():
        s = slot_ref[j]
        for cp in fetch(te_ref[j], s):
            cp.wait()

        @pl.when(nxt_ref[j] >= 0)
        def _():
            for cp in fetch(nxt_ref[j], 1 - s):
                cp.start()

        on_ready(s)


def _ffn_up_kernel(te_ref, tv_ref, first_ref, slot_ref, nxt_ref, hs_ref, w1_hbm, w3_hbm, act_ref,
                   wst, w1b, w3b, sem, *, layer):
    j = pl.program_id(0)

    def fetch(e, s):
        return (pltpu.make_async_copy(w1_hbm.at[layer, e], wst.at[s, 0], sem.at[s, 0]),
                pltpu.make_async_copy(w3_hbm.at[layer, e], wst.at[s, 1], sem.at[s, 1]))

    def on_ready(s):
        _cast_into(w1b, wst.at[s, 0], D)
        _cast_into(w3b, wst.at[s, 1], D)

    _expert_weights((te_ref, tv_ref, first_ref, slot_ref, nxt_ref), j, fetch, on_ready)

    @pl.when(tv_ref[j] > 0)
    def _():
        xb = hs_ref[...].astype(BF16)
        a = jnp.dot(xb, w1b[...], preferred_element_type=F32)
        b = jnp.dot(xb, w3b[...], preferred_element_type=F32)
        act_ref[...] = (_silu(a) * b).astype(BF16)

    @pl.when(tv_ref[j] == 0)
    def _():
        act_ref[...] = jnp.zeros((ETILE, FF), BF16)


def _ffn_down_kernel(te_ref, tv_ref, first_ref, slot_ref, nxt_ref, act_ref, w2_hbm, ys_ref, wst, w2b, sem, *, layer):
    j = pl.program_id(0)

    def fetch(e, s):
        return (pltpu.make_async_copy(w2_hbm.at[layer, e], wst.at[s], sem.at[s]),)

    def on_ready(s):
        _cast_into(w2b, wst.at[s], FF)

    _expert_weights((te_ref, tv_ref, first_ref, slot_ref, nxt_ref), j, fetch, on_ready)

    @pl.when(tv_ref[j] > 0)
    def _():
        ys_ref[...] = jnp.dot(act_ref[...], w2b[...], preferred_element_type=F32)

    @pl.when(tv_ref[j] == 0)
    def _():
        ys_ref[...] = jnp.zeros((ETILE, D), F32)


def _tile_or_first(j, te, tv, *_):
    return jnp.where(tv[j] > 0, j, 0)


def ffn_up(plan, hs, w1, w3, layer):
    return pl.pallas_call(
        functools.partial(_ffn_up_kernel, layer=layer),
        grid_spec=pltpu.PrefetchScalarGridSpec(
            num_scalar_prefetch=5,
            grid=(N_ETILES,),
            in_specs=[pl.BlockSpec((ETILE, D), lambda j, *p: (_tile_or_first(j, *p), 0)),
                      pl.BlockSpec(memory_space=pl.ANY),
                      pl.BlockSpec(memory_space=pl.ANY)],
            out_specs=pl.BlockSpec((ETILE, FF), lambda j, *p: (j, 0)),
            scratch_shapes=[pltpu.VMEM((2, 2, D, FF), F32), pltpu.VMEM((D, FF), BF16), pltpu.VMEM((D, FF), BF16),
                            pltpu.SemaphoreType.DMA((2, 2))]),
        out_shape=jax.ShapeDtypeStruct((P_ROWS, FF), BF16),
        name="ffn_up",
        compiler_params=_cparams(("arbitrary",)),
    )(*plan, hs, w1, w3)


def ffn_down(plan, act, w2, layer):
    return pl.pallas_call(
        functools.partial(_ffn_down_kernel, layer=layer),
        grid_spec=pltpu.PrefetchScalarGridSpec(
            num_scalar_prefetch=5,
            grid=(N_ETILES,),
            in_specs=[pl.BlockSpec((ETILE, FF), lambda j, *p: (j, 0)),
                      pl.BlockSpec(memory_space=pl.ANY)],
            out_specs=pl.BlockSpec((ETILE, D), lambda j, *p: (j, 0)),
            scratch_shapes=[pltpu.VMEM((2, FF, D), F32), pltpu.VMEM((FF, D), BF16), pltpu.SemaphoreType.DMA((2,))]),
        out_shape=jax.ShapeDtypeStruct((P_ROWS, D), F32),
        name="ffn_down",
        compiler_params=_cparams(("arbitrary",)),
    )(*plan, act, w2)


CB_TM = SEG // 9


def _combine_kernel(pos0_ref, pos1_ref, ys_ref, x_ref, w_ref, mod_ref, o_ref, buf, sem):
    i = pl.program_id(0)
    batch = i // 9
    is_ctx = (i % 9) == 0

    def row_copy(k, r, p):
        return pltpu.make_async_copy(ys_ref.at[pl.ds(p, 1), :], buf.at[k, pl.ds(r, 1), :], sem)

    def issue(r, c):
        t = i * CB_TM + r
        row_copy(0, r, pos0_ref[t]).start()
        row_copy(1, r, pos1_ref[t]).start()
        return c

    lax.fori_loop(0, CB_TM, issue, 0, unroll=ROW_DMA_UNROLL)

    def drain(r, c):
        row_copy(0, 0, 0).wait()
        row_copy(1, 0, 0).wait()
        return c

    lax.fori_loop(0, CB_TM, drain, 0, unroll=ROW_DMA_UNROLL)
    f = w_ref[:, 0:1] * buf[0] + w_ref[:, 1:2] * buf[1]
    o_ref[...] = x_ref[...] + _row_mod(mod_ref, 5, batch, is_ctx) * f


def combine_rows(pos0, pos1, ys, x1, w01, mod_l):
    return pl.pallas_call(
        _combine_kernel,
        grid_spec=pltpu.PrefetchScalarGridSpec(
            num_scalar_prefetch=2,
            grid=(T // CB_TM,),
            in_specs=[pl.BlockSpec(memory_space=pl.ANY),
                      pl.BlockSpec((CB_TM, D), lambda i, *_: (i, 0)),
                      pl.BlockSpec((CB_TM, 2), lambda i, *_: (i, 0)),
                      pl.BlockSpec((8, 6 * D), lambda i, *_: (0, 0))],
            out_specs=pl.BlockSpec((CB_TM, D), lambda i, *_: (i, 0)),
            scratch_shapes=[pltpu.VMEM((2, CB_TM, D), F32), pltpu.SemaphoreType.DMA]),
        out_shape=jax.ShapeDtypeStruct((T, D), F32),
        name="combine_rows",
        compiler_params=_cparams(("arbitrary",)),
    )(pos0, pos1, ys, x1, w01, mod_l)


def route_plan(comb_t, sel_t):
    sel = sel_t > 0.5
    cnt = jnp.sum(sel, axis=1).astype(I32)
    tiles = (cnt + ETILE - 1) // ETILE
    tend = jnp.cumsum(tiles)
    toff = tend - tiles
    rank = jnp.cumsum(sel.astype(I32), axis=1) - 1
    pos = toff[:, None] * ETILE + rank
    pos0 = jnp.min(jnp.where(sel, pos, P_ROWS), axis=0).astype(I32)
    pos1 = jnp.max(jnp.where(sel, pos, -1), axis=0).astype(I32)
    w0 = jnp.sum(jnp.where(sel & (pos == pos0[None]), comb_t, 0.0), axis=0)
    w1 = jnp.sum(jnp.where(sel & (pos == pos1[None]), comb_t, 0.0), axis=0)
    n_used = tend[-1]
    tidx = jnp.arange(N_ETILES, dtype=I32)
    te_raw = jnp.sum((tend[None, :] <= tidx[:, None]).astype(I32), axis=1)
    te_last = jnp.sum((tend <= n_used - 1).astype(I32))
    te = jnp.where(tidx < n_used, te_raw, te_last).astype(I32)
    te = jnp.minimum(te, N_EXPERTS - 1)
    tv = jnp.clip(cnt[te] - (tidx - toff[te]) * ETILE, 0, ETILE)
    tv = jnp.where(tidx < n_used, tv, 0).astype(I32)
    tail = n_used + jnp.arange(N_EXPERTS, dtype=I32)
    ztile = jnp.concatenate([jnp.where(tiles > 0, tend - 1, -1), jnp.where(tail < N_ETILES, tail, -1)]).astype(I32)
    used = tidx < n_used
    first = (used & ((tidx == 0) | (te != jnp.roll(te, 1)))).astype(I32)
    slot = ((jnp.cumsum(first) - 1) % 2).astype(I32)
    nxt_tile = tend[te]
    nxt = jnp.where(nxt_tile < n_used, te[jnp.minimum(nxt_tile, N_ETILES - 1)], -1).astype(I32)
    return pos0, pos1, jnp.stack([w0, w1], axis=1), (te, tv, first, slot, nxt), ztile


def _final_kernel(x_ref, g_ref, o_ref):
    o_ref[...] = _rms(x_ref[...]) * g_ref[...]


def final_norm(xa, g):
    nb = SEQ // CTX
    return pl.pallas_call(
        _final_kernel,
        grid=(BATCH, nb),
        in_specs=[pl.BlockSpec((CTX, D), lambda b, j: (b * (nb + 1) + 1 + j, 0)),
                  pl.BlockSpec((1, D), lambda b, j: (0, 0))],
        out_specs=pl.BlockSpec((CTX, D), lambda b, j: (b * nb + j, 0)),
        out_shape=jax.ShapeDtypeStruct((BATCH * SEQ, D), F32),
        name="final_norm",
        compiler_params=_cparams(("arbitrary", "arbitrary")),
    )(xa, g)


def _rope_tables():
    t = jnp.arange(SEQ, dtype=I32)
    row = (t // GRID_W).astype(F32)
    col = (t % GRID_W).astype(F32)
    n_freq = HD // 4
    inv_freq = ROPE_BASE ** (-jnp.arange(n_freq, dtype=F32) / n_freq)
    ar = row[:, None] * inv_freq[None, :]
    ac = col[:, None] * inv_freq[None, :]
    cos_t = jnp.concatenate([jnp.cos(ar), jnp.cos(ar), jnp.cos(ac), jnp.cos(ac)], axis=1)
    sin_t = jnp.concatenate([-jnp.sin(ar), jnp.sin(ar), -jnp.sin(ac), jnp.sin(ac)], axis=1)
    return cos_t, sin_t


GATE_LO, GATE_HI = 3328, 3344


def kernel(x, c, ctx, c_ctx, w_ada, b_ada, norm_mix, norm_ffn, w_in, b_gates, conv_qk, sink, rpb, w_out,
           w_router, b_router, w1, w3, w2, norm_final):
    xa = jnp.concatenate([ctx, x], axis=1).reshape(T, D)
    cond8 = jnp.concatenate([c, c_ctx[None], jnp.zeros((3, D), F32)], axis=0)
    mod = ada_all(cond8, w_ada, b_ada)
    cos_t, sin_t = _rope_tables()
    w_in_b = w_in.astype(BF16)
    w_main = jnp.concatenate([w_in_b[:, :, :GATE_LO], w_in_b[:, :, GATE_HI:]], axis=2)
    w_gate = jnp.pad(w_in_b[:, :, GATE_LO:GATE_HI], ((0, 0), (0, 0), (0, HD - 4 * B_HEADS)))
    w_out_b = w_out.astype(BF16)
    w_router_hi = w_router.T.astype(BF16)
    w_router_lo = (w_router.T - w_router_hi.astype(F32)).astype(BF16)
    b_router_c = b_router.reshape(N_EXPERTS, 1)

    for l in range(DEPTH):
        z, zg = in_proj(xa, norm_mix[l].reshape(1, D), mod[l], w_main[l], w_gate[l])
        g_row = zg[:, :4 * B_HEADS].T
        ya = window_attn(z, sink[l], cos_t, sin_t)
        yb = mlstm(z, g_row, b_gates[l], conv_qk[l])
        yn = nbr_attn(z, rpb[l])
        x1, h2, comb_t, sel_t = out_proj(ya, yb, yn, w_out_b[l], xa, mod[l], norm_ffn[l].reshape(1, D),
                                         w_router_hi, w_router_lo, b_router_c)
        pos0, pos1, w01, plan, ztile = route_plan(comb_t, sel_t)
        hs = scatter_rows(pos0, pos1, ztile, h2)
        act = ffn_up(plan, hs, w1, w3, l)
        ys = ffn_down(plan, act, w2, l)
        xa = combine_rows(pos0, pos1, ys, x1, w01, mod[l])
    return final_norm(xa, norm_final.reshape(1, D)).reshape(BATCH, SEQ, D)
```

```python
import functools

import jax
import jax.numpy as jnp
from jax import lax
from jax.experimental import pallas as pl
from jax.experimental.pallas import tpu as pltpu

F32 = jnp.float32
BF16 = jnp.bfloat16
I32 = jnp.int32

D = 2048
BATCH = 4
SEQ = 2048
CTX = 256
SEG = CTX + SEQ
T = BATCH * SEG
DEPTH = 4
GRID_W = 64
HD = 128
A_HEADS, A_KV, A_GRP = 6, 2, 3
B_HEADS = 4
C_HEADS = 6
A_BLOCK = 128
NB_ROWS, NB_COLS = 8, 16
CHUNK = 256
N_EXPERTS, N_GROUPS, PER_GROUP = 16, 4, 4
FF = 1024
EPS = 1e-6
ROPE_BASE = 10000.0
NEG = -1e30
ATT_SCALE = HD ** -0.5

N_MAIN = 5632
COL_AQ, COL_AK, COL_AV = 0, 6, 8
COL_BQ, COL_BK, COL_BV, COL_BO = 10, 14, 18, 22
COL_CQ, COL_CK, COL_CV = 26, 32, 38

ETILE = 256
N_ETILES = (2 * T) // ETILE + N_EXPERTS
P_ROWS = N_ETILES * ETILE

VMEM_LIMIT = 56 * 1024 * 1024


def _cparams(sem):
    return pltpu.CompilerParams(dimension_semantics=sem, vmem_limit_bytes=VMEM_LIMIT)


def _silu(v):
    return v * jax.nn.sigmoid(v)


def _log_sigmoid(v):
    return jnp.minimum(v, 0.0) - jnp.log1p(jnp.exp(-jnp.abs(v)))


ADA_TN = 1024


def _ada_kernel(s_ref, w_ref, b_ref, o_ref):
    s = _silu(s_ref[...]).astype(BF16)
    o_ref[0] = jnp.dot(s, w_ref[0].astype(BF16), preferred_element_type=F32) + b_ref[0]


def ada_all(cond8, w_ada, b_ada):
    n = w_ada.shape[-1]
    return pl.pallas_call(
        _ada_kernel,
        grid=(DEPTH, n // ADA_TN),
        in_specs=[pl.BlockSpec((8, D), lambda l, j: (0, 0)),
                  pl.BlockSpec((1, D, ADA_TN), lambda l, j: (l, 0, j)),
                  pl.BlockSpec((1, 1, ADA_TN), lambda l, j: (l, 0, j))],
        out_specs=pl.BlockSpec((1, 8, ADA_TN), lambda l, j: (l, 0, j)),
        out_shape=jax.ShapeDtypeStruct((DEPTH, 8, n), F32),
        name="ada_mod",
        compiler_params=_cparams(("arbitrary", "arbitrary")),
    )(cond8, w_ada, b_ada.reshape(DEPTH, 1, n))


def _row_mod(mod_ref, chunk, batch, is_ctx):
    lat = mod_ref[pl.ds(batch, 1), chunk * D:(chunk + 1) * D]
    ctx = mod_ref[4:5, chunk * D:(chunk + 1) * D]
    return jnp.where(is_ctx, ctx, lat)


def _rms(x):
    return x * lax.rsqrt(jnp.mean(x * x, axis=-1, keepdims=True) + EPS)


IN_TM = SEG // 2
IN_TN = 1408


def _in_kernel(x_ref, g_ref, mod_ref, w_ref, wg_ref, z_ref, zg_ref, h_scr):
    i = pl.program_id(0)
    j = pl.program_id(1)

    @pl.when(j == 0)
    def _():
        batch = i // 2
        rows = lax.broadcasted_iota(I32, (IN_TM, 1), 0) + (i % 2) * IN_TM
        is_ctx = rows < CTX
        xn = _rms(x_ref[...]) * g_ref[...]
        h = xn * (1.0 + _row_mod(mod_ref, 1, batch, is_ctx)) + _row_mod(mod_ref, 0, batch, is_ctx)
        hb = h.astype(BF16)
        h_scr[...] = hb
        zg_ref[...] = jnp.dot(hb, wg_ref[...], preferred_element_type=F32)

    z_ref[...] = jnp.dot(h_scr[...], w_ref[...], preferred_element_type=F32).astype(BF16)


def in_proj(xa, g, mod_l, w_main, w_gate):
    return pl.pallas_call(
        _in_kernel,
        grid=(T // IN_TM, N_MAIN // IN_TN),
        in_specs=[pl.BlockSpec((IN_TM, D), lambda i, j: (i, 0)),
                  pl.BlockSpec((1, D), lambda i, j: (0, 0)),
                  pl.BlockSpec((8, 6 * D), lambda i, j: (0, 0)),
                  pl.BlockSpec((D, IN_TN), lambda i, j: (0, j)),
                  pl.BlockSpec((D, HD), lambda i, j: (0, 0))],
        out_specs=[pl.BlockSpec((IN_TM, IN_TN), lambda i, j: (i, j)),
                   pl.BlockSpec((IN_TM, HD), lambda i, j: (i, 0))],
        out_shape=[jax.ShapeDtypeStruct((T, N_MAIN), BF16),
                   jax.ShapeDtypeStruct((T, HD), F32)],
        scratch_shapes=[pltpu.VMEM((IN_TM, D), BF16)],
        name="in_proj",
        compiler_params=_cparams(("arbitrary", "arbitrary")),
    )(xa, g, mod_l, w_main, w_gate)


NBLK = SEQ // A_BLOCK
KPAD = SEG + A_BLOCK


def _dot_nt(a, b):
    return lax.dot_general(a, b, (((1,), (1,)), ((), ())), preferred_element_type=F32)


def _attend(scores, values, extra=None):
    m = scores[0].max(axis=-1, keepdims=True)
    for s in scores[1:]:
        m = jnp.maximum(m, s.max(axis=-1, keepdims=True))
    if extra is not None:
        m = jnp.maximum(m, extra)
    den = None if extra is None else jnp.exp(extra - m)
    out = None
    for s, v in zip(scores, values):
        e = jnp.exp(s - m)
        d = e.sum(axis=-1, keepdims=True)
        den = d if den is None else den + d
        o = jnp.dot(e.astype(BF16), v, preferred_element_type=F32)
        out = o if out is None else out + o
    return out / den


def _win_kernel(sink_ref, q_ref, k_ref, v_ref, cos_ref, sin_ref, o_ref, qs, ks, vs):
    kv = pl.program_id(1)
    lane = lax.broadcasted_iota(I32, (A_BLOCK, HD), 1)
    first_half = (lane % 64) < 32

    def rope(zf, cos, sin):
        zr = jnp.where(first_half, pltpu.roll(zf, 96, 1), pltpu.roll(zf, 32, 1))
        return zf * cos + zr * sin

    ks[0:CTX, :] = k_ref[0:CTX, :]
    ks[SEG:KPAD, :] = jnp.zeros((A_BLOCK, HD), BF16)
    vs[0:SEG, :] = v_ref[...]
    vs[SEG:KPAD, :] = jnp.zeros((A_BLOCK, HD), BF16)

    def rope_blk(i, c):
        r0 = pl.multiple_of(i * A_BLOCK, A_BLOCK)
        cos = cos_ref[pl.ds(r0, A_BLOCK), :]
        sin = sin_ref[pl.ds(r0, A_BLOCK), :]
        ks[pl.ds(CTX + r0, A_BLOCK), :] = rope(k_ref[pl.ds(CTX + r0, A_BLOCK), :].astype(F32), cos, sin).astype(BF16)
        for g in range(A_GRP):
            zf = q_ref[pl.ds(CTX + r0, A_BLOCK), g * HD:(g + 1) * HD].astype(F32)
            qs[i, g * A_BLOCK:(g + 1) * A_BLOCK, :] = rope(zf, cos, sin).astype(BF16)
        return c

    lax.fori_loop(0, NBLK, rope_blk, 0)

    nq = A_GRP * A_BLOCK
    row = lax.broadcasted_iota(I32, (nq, 1), 0)
    sink = jnp.where(row < A_BLOCK, sink_ref[kv * A_GRP],
                     jnp.where(row < 2 * A_BLOCK, sink_ref[kv * A_GRP + 1], sink_ref[kv * A_GRP + 2]))
    r = lax.broadcasted_iota(I32, (nq, 3 * A_BLOCK), 0) % A_BLOCK
    c = lax.broadcasted_iota(I32, (nq, 3 * A_BLOCK), 1)
    band = (c >= r) & (c <= r + 2 * A_BLOCK)

    def blk(i, carry):
        w0 = pl.multiple_of(CTX - A_BLOCK + i * A_BLOCK, A_BLOCK)
        q = qs[i]
        s_c = _dot_nt(q, ks[0:CTX, :]) * ATT_SCALE
        s_w = _dot_nt(q, ks[pl.ds(w0, 3 * A_BLOCK), :]) * ATT_SCALE
        kpos = (i - 1) * A_BLOCK + c
        s_w = jnp.where(band & (kpos >= 0) & (kpos < SEQ), s_w, NEG)
        o = _attend([s_c, s_w], [vs[0:CTX, :], vs[pl.ds(w0, 3 * A_BLOCK), :]], sink)
        o0 = pl.multiple_of(CTX + i * A_BLOCK, A_BLOCK)
        for g in range(A_GRP):
            o_ref[pl.ds(o0, A_BLOCK), g * HD:(g + 1) * HD] = o[g * A_BLOCK:(g + 1) * A_BLOCK].astype(BF16)
        return carry

    lax.fori_loop(0, NBLK, blk, 0)

    for g in range(A_GRP):
        s = _dot_nt(q_ref[0:CTX, g * HD:(g + 1) * HD], k_ref[0:CTX, :]) * ATT_SCALE
        o = _attend([s], [v_ref[0:CTX, :]], jnp.full((CTX, 1), sink_ref[kv * A_GRP + g], F32))
        o_ref[0:CTX, g * HD:(g + 1) * HD] = o.astype(BF16)


def window_attn(z, sink_l, cos_t, sin_t):
    return pl.pallas_call(
        _win_kernel,
        grid=(BATCH, A_KV),
        in_specs=[pl.BlockSpec(memory_space=pltpu.SMEM),
                  pl.BlockSpec((SEG, A_GRP * HD), lambda b, kv: (b, kv)),
                  pl.BlockSpec((SEG, HD), lambda b, kv: (b, COL_AK + kv)),
                  pl.BlockSpec((SEG, HD), lambda b, kv: (b, COL_AV + kv)),
                  pl.BlockSpec((SEQ, HD), lambda b, kv: (0, 0)),
                  pl.BlockSpec((SEQ, HD), lambda b, kv: (0, 0))],
        out_specs=pl.BlockSpec((SEG, A_GRP * HD), lambda b, kv: (b, kv)),
        out_shape=jax.ShapeDtypeStruct((T, A_HEADS * HD), BF16),
        scratch_shapes=[pltpu.VMEM((NBLK, A_GRP * A_BLOCK, HD), BF16),
                        pltpu.VMEM((KPAD, HD), BF16),
                        pltpu.VMEM((KPAD, HD), BF16)],
        name="window_attn",
        compiler_params=_cparams(("arbitrary", "arbitrary")),
    )(sink_l, z, z, z, cos_t, sin_t)


GRID_ROWS = SEQ // GRID_W
NBQ_ROWS = 4
NBK_ROWS = 12
NBQ, NBK = NBQ_ROWS * GRID_W, NBK_ROWS * GRID_W
N_NBLK = GRID_ROWS // NBQ_ROWS
NB_BASE_MAX = GRID_ROWS - NBK_ROWS


def _nbr_key_base(first_row):
    return jnp.clip(first_row - NB_ROWS // 2, 0, NB_BASE_MAX)


N_DR, N_DC = 2 * NB_ROWS - 1, 2 * NB_COLS - 1


def _nbr_block_offsets():
    out = []
    for first_row in (0, NBQ_ROWS, GRID_ROWS - NBQ_ROWS):
        base = min(max(first_row - NB_ROWS // 2, 0), NB_BASE_MAX)
        kind = []
        for qi in range(NBQ_ROWS):
            r = first_row + qi
            start = min(max(r - NB_ROWS // 2, 0), GRID_ROWS - NB_ROWS)
            kind.append([kr - r + NB_ROWS - 1 if start <= kr < start + NB_ROWS else N_DR
                         for kr in range(base, base + NBK_ROWS)])
        out.append(kind)
    return out


def _nbr_build_bias(rpb_ref, h, half, bias):
    qc = lax.broadcasted_iota(I32, (GRID_W, 2 * GRID_W), 0)
    lane = lax.broadcasted_iota(I32, (GRID_W, 2 * GRID_W), 1)
    kc = lane % GRID_W
    start_c = jnp.clip(qc - NB_COLS // 2, 0, GRID_W - NB_COLS)
    col_ok = (kc >= start_c) & (kc < start_c + NB_COLS)
    dc = kc - qc + NB_COLS - 1
    left = lane < GRID_W
    for d in range(N_DR):
        t = jnp.full((GRID_W, 2 * GRID_W), NEG, F32)
        for j in range(N_DC):
            t = jnp.where(dc == j, rpb_ref[(h * N_DR + d) * N_DC + j], t)
        t = jnp.where(col_ok, t, NEG)
        half[d, 0] = jnp.where(left, t, NEG)
        half[d, 1] = jnp.where(left, NEG, t)
    masked = jnp.full((GRID_W, 2 * GRID_W), NEG, F32)
    for kind, per_q in enumerate(_nbr_block_offsets()):
        for qi, dr in enumerate(per_q):
            for p in range(NBK_ROWS // 2):
                lo = half[dr[2 * p], 0] if dr[2 * p] < N_DR else masked
                hi = half[dr[2 * p + 1], 1] if dr[2 * p + 1] < N_DR else masked
                bias[kind, qi * GRID_W:(qi + 1) * GRID_W, p * 2 * GRID_W:(p + 1) * 2 * GRID_W] = jnp.maximum(lo, hi)


def _nbr_kernel(rpb_ref, q_ref, k_ref, v_ref, o_ref, half, bias):
    @pl.when(pl.program_id(1) == 0)
    def _():
        _nbr_build_bias(rpb_ref, pl.program_id(0), half, bias)

    def block(i):
        kind = jnp.where(i == 0, 0, jnp.where(i == N_NBLK - 1, 2, 1))
        q0 = pl.multiple_of(CTX + i * NBQ, NBQ)
        k0 = pl.multiple_of(CTX + _nbr_key_base(i * NBQ_ROWS) * GRID_W, GRID_W)
        q = q_ref[pl.ds(q0, NBQ), :]
        s_c = _dot_nt(q, k_ref[0:CTX, :]) * ATT_SCALE
        s_n = _dot_nt(q, k_ref[pl.ds(k0, NBK), :]) * ATT_SCALE + bias[kind]
        o = _attend([s_c, s_n], [v_ref[0:CTX, :], v_ref[pl.ds(k0, NBK), :]])
        o_ref[pl.ds(q0, NBQ), :] = o.astype(BF16)

    def blocks(i, carry):
        block(2 * i)
        block(2 * i + 1)
        return carry

    lax.fori_loop(0, N_NBLK // 2, blocks, 0)

    s = _dot_nt(q_ref[0:CTX, :], k_ref[0:CTX, :]) * ATT_SCALE
    o_ref[0:CTX, :] = _attend([s], [v_ref[0:CTX, :]]).astype(BF16)


def nbr_attn(z, rpb_l):
    return pl.pallas_call(
        _nbr_kernel,
        grid=(C_HEADS, BATCH),
        in_specs=[pl.BlockSpec(memory_space=pltpu.SMEM),
                  pl.BlockSpec((SEG, HD), lambda h, b: (b, COL_CQ + h)),
                  pl.BlockSpec((SEG, HD), lambda h, b: (b, COL_CK + h)),
                  pl.BlockSpec((SEG, HD), lambda h, b: (b, COL_CV + h))],
        out_specs=pl.BlockSpec((SEG, HD), lambda h, b: (b, h)),
        out_shape=jax.ShapeDtypeStruct((T, C_HEADS * HD), BF16),
        scratch_shapes=[pltpu.VMEM((N_DR, 2, GRID_W, 2 * GRID_W), F32), pltpu.VMEM((3, NBQ, NBK), F32)],
        name="nbr_attn",
        compiler_params=_cparams(("arbitrary", "arbitrary")),
    )(rpb_l.reshape(-1), z, z, z)


N_LCHUNK = SEQ // CHUNK
PADR = 8
CONV_ROWS = PADR + CTX + PADR + SEQ + PADR


def _conv_off(r0):
    return jnp.where(r0 < CTX, r0 + PADR, r0 + 2 * PADR)


def _mlstm_kernel(bg_ref, q_ref, k_ref, v_ref, og_ref, gr_ref, cwq_ref, cwk_ref, y_ref,
                  qs, ks, stage, hf, hb):
    h = pl.program_id(1)
    rowi = lax.broadcasted_iota(I32, (CHUNK, 1), 0)

    def conv_silu(src_ref, cw_ref, dst, post):
        stage[...] = jnp.zeros((CONV_ROWS, HD), F32)
        stage[PADR:PADR + CTX, :] = src_ref[0:CTX, :].astype(F32)
        stage[2 * PADR + CTX:2 * PADR + SEG, :] = src_ref[CTX:SEG, :].astype(F32)

        def blk(c, carry):
            r0 = pl.multiple_of(c * CHUNK, CHUNK)
            s0 = pl.multiple_of(_conv_off(r0), PADR)
            cur = stage[pl.ds(s0, CHUNK), :]
            prev_last = stage[pl.ds(s0 - PADR, PADR), :][PADR - 1:PADR, :]
            next_first = stage[pl.ds(s0 + CHUNK, PADR), :][0:1, :]
            zm = jnp.where(rowi == 0, prev_last, pltpu.roll(cur, 1, 0))
            zp = jnp.where(rowi == CHUNK - 1, next_first, pltpu.roll(cur, CHUNK - 1, 0))
            y = zm * cw_ref[0:1, :] + cur * cw_ref[1:2, :] + zp * cw_ref[2:3, :]
            dst[pl.ds(r0, CHUNK), :] = (_silu(y) * post).astype(BF16)
            return carry

        lax.fori_loop(0, SEG // CHUNK, blk, 0)

    conv_silu(q_ref, cwq_ref, qs, 1.0)
    conv_silu(k_ref, cwk_ref, ks, ATT_SCALE)

    t_idx = lax.broadcasted_iota(I32, (CHUNK, CHUNK), 0)
    s_idx = lax.broadcasted_iota(I32, (CHUNK, CHUNK), 1)

    def chunk(r0, bwd, state, dst):
        c_mat, n_vec, m_prev = state
        ki, kf = (2, 3) if bwd else (0, 1)
        bi = bg_ref[ki * B_HEADS + h]
        bf = bg_ref[kf * B_HEADS + h]
        qc = qs[pl.ds(r0, CHUNK), :]
        kc = ks[pl.ds(r0, CHUNK), :]
        vc = v_ref[pl.ds(r0, CHUNK), :]
        i_row = gr_ref[pl.ds(ki * B_HEADS + h, 1), pl.ds(r0, CHUNK)] + bi
        lf_row = _log_sigmoid(gr_ref[pl.ds(kf * B_HEADS + h, 1), pl.ds(r0, CHUNK)] + bf)
        causal = (s_idx >= t_idx) if bwd else (s_idx <= t_idx)
        diag = s_idx == t_idx
        b_col = jnp.sum(jnp.where(causal, lf_row, 0.0), axis=1, keepdims=True)
        b_row = jnp.sum(jnp.where(diag, b_col, 0.0), axis=0, keepdims=True)
        a_row = i_row - b_row
        a_col = jnp.sum(jnp.where(diag, a_row, 0.0), axis=1, keepdims=True)
        mx = jnp.maximum(m_prev, jnp.max(jnp.where(causal, a_row, NEG), axis=1, keepdims=True))
        dm = jnp.exp(jnp.where(causal, a_row - mx, NEG))
        s = _dot_nt(qc, kc) * dm
        w_int = jnp.exp(m_prev - mx)
        num = (jnp.dot(s.astype(BF16), vc, preferred_element_type=F32)
               + w_int * jnp.dot(qc, c_mat.astype(BF16), preferred_element_type=F32))
        den = (jnp.sum(s, axis=1, keepdims=True)
               + w_int * jnp.sum(qc.astype(F32) * n_vec, axis=1, keepdims=True))
        m_t = b_col + mx
        dst[pl.ds(r0, CHUNK), :] = num / jnp.maximum(jnp.abs(den), jnp.exp(-m_t))
        b_end = jnp.sum(lf_row, axis=1, keepdims=True)
        m_end = jnp.maximum(m_prev, jnp.max(a_row, axis=1, keepdims=True))
        decay = jnp.exp(m_prev - m_end)
        w_col = jnp.exp(a_col - m_end)
        kv = lax.dot_general(kc, (w_col * vc.astype(F32)).astype(BF16), (((0,), (0,)), ((), ())),
                             preferred_element_type=F32)
        c_new = decay * c_mat + kv
        n_new = decay * n_vec + jnp.sum(w_col * kc.astype(F32), axis=0, keepdims=True)
        return c_new, n_new, b_end + m_end

    zero = (jnp.zeros((HD, HD), F32), jnp.zeros((1, HD), F32), jnp.zeros((1, 1), F32))
    st_f = chunk(0, False, zero, hf)
    st_b = chunk(0, True, zero, hb)

    def body(j, carry):
        sf, sb = carry
        rf = pl.multiple_of(CTX + j * CHUNK, CHUNK)
        rb = pl.multiple_of(CTX + (N_LCHUNK - 1 - j) * CHUNK, CHUNK)
        return chunk(rf, False, sf, hf), chunk(rb, True, sb, hb)

    lax.fori_loop(0, N_LCHUNK, body, (st_f, st_b))

    def fin(c, carry):
        r0 = pl.multiple_of(c * CHUNK, CHUNK)
        gate = jax.nn.sigmoid(og_ref[pl.ds(r0, CHUNK), :].astype(F32))
        y_ref[pl.ds(r0, CHUNK), :] = (gate * (hf[pl.ds(r0, CHUNK), :] + hb[pl.ds(r0, CHUNK), :])).astype(BF16)
        return carry

    lax.fori_loop(0, SEG // CHUNK, fin, 0)


def mlstm(z, g_row, b_gates_l, conv_l):
    return pl.pallas_call(
        _mlstm_kernel,
        grid=(BATCH, B_HEADS),
        in_specs=[pl.BlockSpec(memory_space=pltpu.SMEM),
                  pl.BlockSpec((SEG, HD), lambda b, h: (b, COL_BQ + h)),
                  pl.BlockSpec((SEG, HD), lambda b, h: (b, COL_BK + h)),
                  pl.BlockSpec((SEG, HD), lambda b, h: (b, COL_BV + h)),
                  pl.BlockSpec((SEG, HD), lambda b, h: (b, COL_BO + h)),
                  pl.BlockSpec((4 * B_HEADS, SEG), lambda b, h: (0, b)),
                  pl.BlockSpec((3, HD), lambda b, h: (0, h)),
                  pl.BlockSpec((3, HD), lambda b, h: (0, B_HEADS + h))],
        out_specs=pl.BlockSpec((SEG, HD), lambda b, h: (b, h)),
        out_shape=jax.ShapeDtypeStruct((T, B_HEADS * HD), BF16),
        scratch_shapes=[pltpu.VMEM((SEG, HD), BF16), pltpu.VMEM((SEG, HD), BF16),
                        pltpu.VMEM((CONV_ROWS, HD), F32),
                        pltpu.VMEM((SEG, HD), F32), pltpu.VMEM((SEG, HD), F32)],
        name="mlstm",
        compiler_params=_cparams(("arbitrary", "arbitrary")),
    )(b_gates_l, z, z, z, z, g_row, conv_l, conv_l)


OUT_TM = SEG // 6


def _top2_routing(biased, scores, comb_ref, sel_ref):
    rows = [biased[e:e + 1, :] for e in range(N_EXPERTS)]
    gscore = []
    for g in range(N_GROUPS):
        v0, v1, v2, v3 = rows[4 * g:4 * g + 4]
        hi01, lo01 = jnp.maximum(v0, v1), jnp.minimum(v0, v1)
        hi23, lo23 = jnp.maximum(v2, v3), jnp.minimum(v2, v3)
        gscore.append(jnp.maximum(hi01, hi23) + jnp.maximum(jnp.minimum(hi01, hi23), jnp.maximum(lo01, lo23)))
    picked = []
    total = None
    for e in range(N_EXPERTS):
        g = e // PER_GROUP
        ok = None
        for o in range(N_GROUPS):
            if o != g:
                c = (gscore[g] > gscore[o]) if o < g else (gscore[g] >= gscore[o])
                ok = c if ok is None else (ok & c)
        ahead = jnp.zeros_like(rows[e])
        for o in range(4 * g, 4 * g + 4):
            if o != e:
                c = (rows[o] > rows[e]) if o > e else (rows[o] >= rows[e])
                ahead = ahead + jnp.where(c, 1.0, 0.0)
        sel = jnp.where(ok & (ahead < 2.0), 1.0, 0.0)
        sel_ref[e:e + 1, :] = sel
        picked.append(sel * scores[e:e + 1, :])
        total = picked[-1] if total is None else total + picked[-1]
    for e in range(N_EXPERTS):
        comb_ref[e:e + 1, :] = picked[e] / total


OUT_RC = 16


N_OUT_TILES = T // OUT_TM


def _out_kernel(ya_ref, yb_ref, yn_ref, w_ref, x_ref, mod_ref, g_ref, wrh_ref, wrl_ref, br_ref,
                x1_ref, h2_ref, comb_ref, sel_ref, ycat, acc, acc_next, h_hi, h_lo):
    i = pl.program_id(0)

    def matmul():
        ycat[:, 0:A_HEADS * HD] = ya_ref[...]
        ycat[:, A_HEADS * HD:(A_HEADS + B_HEADS) * HD] = yb_ref[...]
        ycat[:, (A_HEADS + B_HEADS) * HD:D] = yn_ref[...]
        acc_next[...] = jnp.dot(ycat[...], w_ref[...], preferred_element_type=F32)

    def epilogue():
        t = i - 1
        batch = t // 6
        for c in range(OUT_TM // OUT_RC):
            rows = slice(c * OUT_RC, (c + 1) * OUT_RC)
            mrow = jnp.where((t % 6) * OUT_TM + c * OUT_RC < CTX, 4, batch)

            def mod(k):
                return mod_ref[pl.ds(mrow, 1), k * D:(k + 1) * D]

            x1 = x_ref[rows, :] + mod(2) * acc[rows, :]
            x1_ref[rows, :] = x1
            h2 = (_rms(x1) * g_ref[...]) * (1.0 + mod(4)) + mod(3)
            h2_ref[rows, :] = h2
            hi = h2.astype(BF16)
            h_hi[rows, :] = hi
            h_lo[rows, :] = (h2 - hi.astype(F32)).astype(BF16)
        scores = jax.nn.sigmoid(_dot_nt(wrh_ref[...], h_hi[...]) + _dot_nt(wrl_ref[...], h_hi[...])
                                + _dot_nt(wrh_ref[...], h_lo[...]))
        _top2_routing(scores + br_ref[...], scores, comb_ref, sel_ref)

    @pl.when(i == 0)
    def _():
        matmul()

    @pl.when((i > 0) & (i < N_OUT_TILES))
    def _():
        matmul()
        epilogue()

    @pl.when(i == N_OUT_TILES)
    def _():
        epilogue()

    @pl.when(i < N_OUT_TILES)
    def _():
        acc[...] = acc_next[...]


def out_proj(ya, yb, yn, w_out_b, xa, mod_l, g, w_router_hi, w_router_lo, b_router_c):
    cur = lambda i: (jnp.minimum(i, N_OUT_TILES - 1), 0)
    row = lambda i: (jnp.maximum(i - 1, 0), 0)
    fixed = lambda i: (0, 0)
    return pl.pallas_call(
        _out_kernel,
        grid=(N_OUT_TILES + 1,),
        in_specs=[pl.BlockSpec((OUT_TM, A_HEADS * HD), cur),
                  pl.BlockSpec((OUT_TM, B_HEADS * HD), cur),
                  pl.BlockSpec((OUT_TM, C_HEADS * HD), cur),
                  pl.BlockSpec((D, D), fixed),
                  pl.BlockSpec((OUT_TM, D), row),
                  pl.BlockSpec((8, 6 * D), fixed),
                  pl.BlockSpec((1, D), fixed),
                  pl.BlockSpec((N_EXPERTS, D), fixed),
                  pl.BlockSpec((N_EXPERTS, D), fixed),
                  pl.BlockSpec((N_EXPERTS, 1), fixed)],
        out_specs=[pl.BlockSpec((OUT_TM, D), row),
                   pl.BlockSpec((OUT_TM, D), row),
                   pl.BlockSpec((N_EXPERTS, OUT_TM), lambda i: (0, jnp.maximum(i - 1, 0))),
                   pl.BlockSpec((N_EXPERTS, OUT_TM), lambda i: (0, jnp.maximum(i - 1, 0)))],
        out_shape=[jax.ShapeDtypeStruct((T, D), F32),
                   jax.ShapeDtypeStruct((T, D), F32),
                   jax.ShapeDtypeStruct((N_EXPERTS, T), F32),
                   jax.ShapeDtypeStruct((N_EXPERTS, T), F32)],
        scratch_shapes=[pltpu.VMEM((OUT_TM, D), BF16), pltpu.VMEM((OUT_TM, D), F32), pltpu.VMEM((OUT_TM, D), F32),
                        pltpu.VMEM((OUT_TM, D), BF16), pltpu.VMEM((OUT_TM, D), BF16)],
        name="out_proj",
        compiler_params=_cparams(("arbitrary",)),
    )(ya, yb, yn, w_out_b, xa, mod_l, g, w_router_hi, w_router_lo, b_router_c)


SC_TM = 1024
ROW_DMA_UNROLL = 8


def _scatter_kernel(pos0_ref, pos1_ref, ztile_ref, h_ref, hs_ref, zero_scr, sem, zsem):
    i = pl.program_id(0)

    @pl.when(i == 0)
    def _():
        zero_scr[...] = jnp.zeros((ETILE, D), F32)

        def fill_copy(t):
            return pltpu.make_async_copy(zero_scr, hs_ref.at[pl.ds(pl.multiple_of(t * ETILE, ETILE), ETILE), :], zsem)

        def fill_start(e, c):
            @pl.when(ztile_ref[e] >= 0)
            def _():
                fill_copy(ztile_ref[e]).start()
            return c

        def fill_wait(e, c):
            @pl.when(ztile_ref[e] >= 0)
            def _():
                fill_copy(ztile_ref[e]).wait()
            return c

        lax.fori_loop(0, 2 * N_EXPERTS, fill_start, 0)
        lax.fori_loop(0, 2 * N_EXPERTS, fill_wait, 0)

    def row_copy(r, p):
        return pltpu.make_async_copy(h_ref.at[pl.ds(r, 1), :], hs_ref.at[pl.ds(p, 1), :], sem)

    def issue(r, c):
        t = i * SC_TM + r
        row_copy(r, pos0_ref[t]).start()
        row_copy(r, pos1_ref[t]).start()
        return c

    lax.fori_loop(0, SC_TM, issue, 0, unroll=ROW_DMA_UNROLL)

    def drain(r, c):
        row_copy(0, 0).wait()
        row_copy(0, 0).wait()
        return c

    lax.fori_loop(0, SC_TM, drain, 0, unroll=ROW_DMA_UNROLL)


def scatter_rows(pos0, pos1, ztile, h2):
    return pl.pallas_call(
        _scatter_kernel,
        grid_spec=pltpu.PrefetchScalarGridSpec(
            num_scalar_prefetch=3,
            grid=(T // SC_TM,),
            in_specs=[pl.BlockSpec((SC_TM, D), lambda i, *_: (i, 0))],
            out_specs=pl.BlockSpec(memory_space=pl.ANY),
            scratch_shapes=[pltpu.VMEM((ETILE, D), F32), pltpu.SemaphoreType.DMA, pltpu.SemaphoreType.DMA]),
        out_shape=jax.ShapeDtypeStruct((P_ROWS, D), F32),
        name="scatter_rows",
        compiler_params=_cparams(("arbitrary",)),
    )(pos0, pos1, ztile, h2)


CAST_ROWS = 256


def _cast_into(dst, src, n_rows):
    def body(c, carry):
        r0 = pl.multiple_of(c * CAST_ROWS, CAST_ROWS)
        dst[pl.ds(r0, CAST_ROWS), :] = src[pl.ds(r0, CAST_ROWS), :].astype(BF16)
        return carry

    lax.fori_loop(0, n_rows // CAST_ROWS, body, 0)


def _ffn_kernel(te_ref, tv_ref, first_ref, nxt_ref, hs_ref, w1_hbm, w3_hbm, w2_hbm, ys_ref,
                st1, st3, st2, w1b, w3b, w2b, sem, *, layer):
    j = pl.program_id(0)

    def fetch(e):
        return (pltpu.make_async_copy(w1_hbm.at[layer, e], st1, sem.at[0]),
                pltpu.make_async_copy(w3_hbm.at[layer, e], st3, sem.at[1]),
                pltpu.make_async_copy(w2_hbm.at[layer, e], st2, sem.at[2]))

    @pl.when(j == 0)
    def _():
        for cp in fetch(te_ref[0]):
            cp.start()

    @pl.when(first_ref[j] == 1)
    def _():
        for cp in fetch(te_ref[j]):
            cp.wait()
        _cast_into(w1b, st1, D)
        _cast_into(w3b, st3, D)
        _cast_into(w2b, st2, FF)

        @pl.when(nxt_ref[j] >= 0)
        def _():
            for cp in fetch(nxt_ref[j]):
                cp.start()

    @pl.when(tv_ref[j] > 0)
    def _():
        xb = hs_ref[...].astype(BF16)
        a = jnp.dot(xb, w1b[...], preferred_element_type=F32)
        b = jnp.dot(xb, w3b[...], preferred_element_type=F32)
        act = (_silu(a) * b).astype(BF16)
        ys_ref[...] = jnp.dot(act, w2b[...], preferred_element_type=F32)

    @pl.when(tv_ref[j] == 0)
    def _():
        ys_ref[...] = jnp.zeros((ETILE, D), F32)


def _tile_or_first(j, te, tv, *_):
    return jnp.where(tv[j] > 0, j, 0)


def ffn(plan, hs, w1, w3, w2, layer):
    return pl.pallas_call(
        functools.partial(_ffn_kernel, layer=layer),
        grid_spec=pltpu.PrefetchScalarGridSpec(
            num_scalar_prefetch=4,
            grid=(N_ETILES,),
            in_specs=[pl.BlockSpec((ETILE, D), lambda j, *p: (_tile_or_first(j, *p), 0)),
                      pl.BlockSpec(memory_space=pl.ANY),
                      pl.BlockSpec(memory_space=pl.ANY),
                      pl.BlockSpec(memory_space=pl.ANY)],
            out_specs=pl.BlockSpec((ETILE, D), lambda j, *p: (j, 0)),
            scratch_shapes=[pltpu.VMEM((D, FF), F32), pltpu.VMEM((D, FF), F32), pltpu.VMEM((FF, D), F32),
                            pltpu.VMEM((D, FF), BF16), pltpu.VMEM((D, FF), BF16), pltpu.VMEM((FF, D), BF16),
                            pltpu.SemaphoreType.DMA((3,))]),
        out_shape=jax.ShapeDtypeStruct((P_ROWS, D), F32),
        name="ffn",
        compiler_params=_cparams(("arbitrary",)),
    )(*plan, hs, w1, w3, w2)


CB_TM = SEG // 9


CB_RC = 32


def _combine_kernel(pos0_ref, pos1_ref, ys_ref, x_ref, w_ref, mod_ref, o_ref, buf, sem):
    i = pl.program_id(0)
    n = pl.num_programs(0)
    slot = i % 2

    def row_copy(s, k, r, p):
        return pltpu.make_async_copy(ys_ref.at[pl.ds(p, 1), :], buf.at[s, k, pl.ds(r, 1), :], sem.at[s])

    def gather(tile, s):
        def issue(r, c):
            t = tile * CB_TM + r
            row_copy(s, 0, r, pos0_ref[t]).start()
            row_copy(s, 1, r, pos1_ref[t]).start()
            return c

        lax.fori_loop(0, CB_TM, issue, 0, unroll=ROW_DMA_UNROLL)

    @pl.when(i == 0)
    def _():
        gather(0, 0)

    @pl.when(i + 1 < n)
    def _():
        gather(i + 1, 1 - slot)

    def drain(r, c):
        row_copy(slot, 0, 0, 0).wait()
        row_copy(slot, 1, 0, 0).wait()
        return c

    lax.fori_loop(0, CB_TM, drain, 0, unroll=ROW_DMA_UNROLL)
    gate = mod_ref[pl.ds(jnp.where((i % 9) == 0, 4, i // 9), 1), 5 * D:6 * D]

    def chunk(c, carry):
        r0 = pl.multiple_of(c * CB_RC, CB_RC)
        rows = pl.ds(r0, CB_RC)
        f = w_ref[rows, 0:1] * buf[slot, 0, rows, :] + w_ref[rows, 1:2] * buf[slot, 1, rows, :]
        o_ref[rows, :] = x_ref[rows, :] + gate * f
        return carry

    lax.fori_loop(0, CB_TM // CB_RC, chunk, 0)


def combine_rows(pos0, pos1, ys, x1, w01, mod_l):
    return pl.pallas_call(
        _combine_kernel,
        grid_spec=pltpu.PrefetchScalarGridSpec(
            num_scalar_prefetch=2,
            grid=(T // CB_TM,),
            in_specs=[pl.BlockSpec(memory_space=pl.ANY),
                      pl.BlockSpec((CB_TM, D), lambda i, *_: (i, 0)),
                      pl.BlockSpec((CB_TM, 2), lambda i, *_: (i, 0)),
                      pl.BlockSpec((8, 6 * D), lambda i, *_: (0, 0))],
            out_specs=pl.BlockSpec((CB_TM, D), lambda i, *_: (i, 0)),
            scratch_shapes=[pltpu.VMEM((2, 2, CB_TM, D), F32), pltpu.SemaphoreType.DMA((2,))]),
        out_shape=jax.ShapeDtypeStruct((T, D), F32),
        name="combine_rows",
        compiler_params=_cparams(("arbitrary",)),
    )(pos0, pos1, ys, x1, w01, mod_l)


def route_plan(comb_t, sel_t):
    sel = sel_t > 0.5
    cnt = jnp.sum(sel, axis=1).astype(I32)
    tiles = (cnt + ETILE - 1) // ETILE
    tend = jnp.cumsum(tiles)
    toff = tend - tiles
    rank = jnp.cumsum(sel.astype(I32), axis=1) - 1
    pos = toff[:, None] * ETILE + rank
    pos0 = jnp.min(jnp.where(sel, pos, P_ROWS), axis=0).astype(I32)
    pos1 = jnp.max(jnp.where(sel, pos, -1), axis=0).astype(I32)
    w0 = jnp.sum(jnp.where(sel & (pos == pos0[None]), comb_t, 0.0), axis=0)
    w1 = jnp.sum(jnp.where(sel & (pos == pos1[None]), comb_t, 0.0), axis=0)
    n_used = tend[-1]
    tidx = jnp.arange(N_ETILES, dtype=I32)
    te_raw = jnp.sum((tend[None, :] <= tidx[:, None]).astype(I32), axis=1)
    te_last = jnp.sum((tend <= n_used - 1).astype(I32))
    te = jnp.where(tidx < n_used, te_raw, te_last).astype(I32)
    te = jnp.minimum(te, N_EXPERTS - 1)
    tv = jnp.clip(cnt[te] - (tidx - toff[te]) * ETILE, 0, ETILE)
    tv = jnp.where(tidx < n_used, tv, 0).astype(I32)
    tail = n_used + jnp.arange(N_EXPERTS, dtype=I32)
    ztile = jnp.concatenate([jnp.where(tiles > 0, tend - 1, -1), jnp.where(tail < N_ETILES, tail, -1)]).astype(I32)
    used = tidx < n_used
    first = (used & ((tidx == 0) | (te != jnp.roll(te, 1)))).astype(I32)
    nxt_tile = tend[te]
    nxt = jnp.where(nxt_tile < n_used, te[jnp.minimum(nxt_tile, N_ETILES - 1)], -1).astype(I32)
    return pos0, pos1, jnp.stack([w0, w1], axis=1), (te, tv, first, nxt), ztile


def _final_kernel(x_ref, g_ref, o_ref):
    o_ref[...] = _rms(x_ref[...]) * g_ref[...]


def final_norm(xa, g):
    nb = SEQ // CTX
    return pl.pallas_call(
        _final_kernel,
        grid=(BATCH, nb),
        in_specs=[pl.BlockSpec((CTX, D), lambda b, j: (b * (nb + 1) + 1 + j, 0)),
                  pl.BlockSpec((1, D), lambda b, j: (0, 0))],
        out_specs=pl.BlockSpec((CTX, D), lambda b, j: (b * nb + j, 0)),
        out_shape=jax.ShapeDtypeStruct((BATCH * SEQ, D), F32),
        name="final_norm",
        compiler_params=_cparams(("arbitrary", "arbitrary")),
    )(xa, g)


def _rope_tables():
    t = jnp.arange(SEQ, dtype=I32)
    row = (t // GRID_W).astype(F32)
    col = (t % GRID_W).astype(F32)
    n_freq = HD // 4
    inv_freq = ROPE_BASE ** (-jnp.arange(n_freq, dtype=F32) / n_freq)
    ar = row[:, None] * inv_freq[None, :]
    ac = col[:, None] * inv_freq[None, :]
    cos_t = jnp.concatenate([jnp.cos(ar), jnp.cos(ar), jnp.cos(ac), jnp.cos(ac)], axis=1)
    sin_t = jnp.concatenate([-jnp.sin(ar), jnp.sin(ar), -jnp.sin(ac), jnp.sin(ac)], axis=1)
    return cos_t, sin_t


GATE_LO, GATE_HI = 3328, 3344
D_IN = N_MAIN + 4 * B_HEADS
RP_ROWS = 256


def _repack_kernel(w_ref, o_ref, g_ref):
    o_ref[0, :, 0:GATE_LO] = w_ref[0, :, 0:GATE_LO].astype(BF16)
    o_ref[0, :, GATE_LO:N_MAIN] = w_ref[0, :, GATE_HI:D_IN].astype(BF16)
    lane = lax.broadcasted_iota(I32, (RP_ROWS, HD), 1)
    g_ref[0] = jnp.where(lane < 4 * B_HEADS, w_ref[0, :, GATE_LO:GATE_LO + HD], 0.0).astype(BF16)


def repack_w_in(w_in):
    blk = lambda l, r: (l, r, 0)
    return pl.pallas_call(
        _repack_kernel,
        grid=(DEPTH, D // RP_ROWS),
        in_specs=[pl.BlockSpec((1, RP_ROWS, D_IN), blk)],
        out_specs=[pl.BlockSpec((1, RP_ROWS, N_MAIN), blk), pl.BlockSpec((1, RP_ROWS, HD), blk)],
        out_shape=[jax.ShapeDtypeStruct((DEPTH, D, N_MAIN), BF16), jax.ShapeDtypeStruct((DEPTH, D, HD), BF16)],
        name="repack_w_in",
        compiler_params=_cparams(("arbitrary", "arbitrary")),
    )(w_in)


def kernel(x, c, ctx, c_ctx, w_ada, b_ada, norm_mix, norm_ffn, w_in, b_gates, conv_qk, sink, rpb, w_out,
           w_router, b_router, w1, w3, w2, norm_final):
    xa = jnp.concatenate([ctx, x], axis=1).reshape(T, D)
    cond8 = jnp.concatenate([c, c_ctx[None], jnp.zeros((3, D), F32)], axis=0)
    mod = ada_all(cond8, w_ada, b_ada)
    cos_t, sin_t = _rope_tables()
    w_main, w_gate = repack_w_in(w_in)
    w_out_b = w_out.astype(BF16)
    w_router_hi = w_router.T.astype(BF16)
    w_router_lo = (w_router.T - w_router_hi.astype(F32)).astype(BF16)
    b_router_c = b_router.reshape(N_EXPERTS, 1)

    for l in range(DEPTH):
        z, zg = in_proj(xa, norm_mix[l].reshape(1, D), mod[l], w_main[l], w_gate[l])
        g_row = zg[:, :4 * B_HEADS].T
        ya = window_attn(z, sink[l], cos_t, sin_t)
        yb = mlstm(z, g_row, b_gates[l], conv_qk[l])
        yn = nbr_attn(z, rpb[l])
        x1, h2, comb_t, sel_t = out_proj(ya, yb, yn, w_out_b[l], xa, mod[l], norm_ffn[l].reshape(1, D),
                                         w_router_hi, w_router_lo, b_router_c)
        pos0, pos1, w01, plan, ztile = route_plan(comb_t, sel_t)
        hs = scatter_rows(pos0, pos1, ztile, h2)
        ys = ffn(plan, hs, w1, w3, w2, l)
        xa = combine_rows(pos0, pos1, ys, x1, w01, mod[l])
    return final_norm(xa, norm_final.reshape(1, D)).reshape(BATCH, SEQ, D)
```

```python
import functools

import jax
import jax.numpy as jnp
from jax import lax
from jax.experimental import pallas as pl
from jax.experimental.pallas import tpu as pltpu

F32 = jnp.float32
BF16 = jnp.bfloat16
I32 = jnp.int32

D = 2048
BATCH = 4
SEQ = 2048
CTX = 256
SEG = CTX + SEQ
T = BATCH * SEG
DEPTH = 4
GRID_W = 64
HD = 128
A_HEADS, A_KV, A_GRP = 6, 2, 3
B_HEADS = 4
C_HEADS = 6
A_BLOCK = 128
NB_ROWS, NB_COLS = 8, 16
CHUNK = 256
N_EXPERTS, N_GROUPS, PER_GROUP = 16, 4, 4
FF = 1024
EPS = 1e-6
ROPE_BASE = 10000.0
NEG = -1e30
ATT_SCALE = HD ** -0.5

N_MAIN = 5632
COL_AQ, COL_AK, COL_AV = 0, 6, 8
COL_BQ, COL_BK, COL_BV, COL_BO = 10, 14, 18, 22
COL_CQ, COL_CK, COL_CV = 26, 32, 38

ETILE = 256
N_ETILES = (2 * T) // ETILE + N_EXPERTS
P_ROWS = N_ETILES * ETILE

VMEM_LIMIT = 56 * 1024 * 1024


def _cparams(sem):
    return pltpu.CompilerParams(dimension_semantics=sem, vmem_limit_bytes=VMEM_LIMIT)


def _silu(v):
    return v * jax.nn.sigmoid(v)


def _log_sigmoid(v):
    return jnp.minimum(v, 0.0) - jnp.log1p(jnp.exp(-jnp.abs(v)))


ADA_TN = 1024


def _ada_kernel(s_ref, w_ref, b_ref, o_ref):
    s = _silu(s_ref[...]).astype(BF16)
    o_ref[0] = jnp.dot(s, w_ref[0].astype(BF16), preferred_element_type=F32) + b_ref[0]


def ada_all(cond8, w_ada, b_ada):
    n = w_ada.shape[-1]
    return pl.pallas_call(
        _ada_kernel,
        grid=(DEPTH, n // ADA_TN),
        in_specs=[pl.BlockSpec((8, D), lambda l, j: (0, 0)),
                  pl.BlockSpec((1, D, ADA_TN), lambda l, j: (l, 0, j)),
                  pl.BlockSpec((1, 1, ADA_TN), lambda l, j: (l, 0, j))],
        out_specs=pl.BlockSpec((1, 8, ADA_TN), lambda l, j: (l, 0, j)),
        out_shape=jax.ShapeDtypeStruct((DEPTH, 8, n), F32),
        name="ada_mod",
        compiler_params=_cparams(("arbitrary", "arbitrary")),
    )(cond8, w_ada, b_ada.reshape(DEPTH, 1, n))


def _row_mod(mod_ref, chunk, batch, is_ctx):
    lat = mod_ref[pl.ds(batch, 1), chunk * D:(chunk + 1) * D]
    ctx = mod_ref[4:5, chunk * D:(chunk + 1) * D]
    return jnp.where(is_ctx, ctx, lat)


def _rms(x):
    return x * lax.rsqrt(jnp.mean(x * x, axis=-1, keepdims=True) + EPS)


IN_TM = SEG // 2
IN_TN = 1408


def _in_kernel(x_ref, g_ref, mod_ref, w_ref, wg_ref, z_ref, zg_ref, h_scr):
    i = pl.program_id(0)
    j = pl.program_id(1)

    @pl.when(j == 0)
    def _():
        batch = i // 2
        rows = lax.broadcasted_iota(I32, (IN_TM, 1), 0) + (i % 2) * IN_TM
        is_ctx = rows < CTX
        xn = _rms(x_ref[...]) * g_ref[...]
        h = xn * (1.0 + _row_mod(mod_ref, 1, batch, is_ctx)) + _row_mod(mod_ref, 0, batch, is_ctx)
        hb = h.astype(BF16)
        h_scr[...] = hb
        zg_ref[...] = jnp.dot(hb, wg_ref[...], preferred_element_type=F32)

    z_ref[...] = jnp.dot(h_scr[...], w_ref[...], preferred_element_type=F32).astype(BF16)


def in_proj(xa, g, mod_l, w_main, w_gate, layer):
    return pl.pallas_call(
        _in_kernel,
        grid=(T // IN_TM, N_MAIN // IN_TN),
        in_specs=[pl.BlockSpec((IN_TM, D), lambda i, j: (i, 0)),
                  pl.BlockSpec((1, D), lambda i, j: (0, 0)),
                  pl.BlockSpec((8, 6 * D), lambda i, j: (0, 0)),
                  pl.BlockSpec((None, D, IN_TN), lambda i, j: (layer, 0, j)),
                  pl.BlockSpec((None, D, HD), lambda i, j: (layer, 0, 0))],
        out_specs=[pl.BlockSpec((IN_TM, IN_TN), lambda i, j: (i, j)),
                   pl.BlockSpec((IN_TM, HD), lambda i, j: (i, 0))],
        out_shape=[jax.ShapeDtypeStruct((T, N_MAIN), BF16),
                   jax.ShapeDtypeStruct((T, HD), F32)],
        scratch_shapes=[pltpu.VMEM((IN_TM, D), BF16)],
        name="in_proj",
        compiler_params=_cparams(("arbitrary", "arbitrary")),
    )(xa, g, mod_l, w_main, w_gate)


NBLK = SEQ // A_BLOCK
KPAD = SEG + A_BLOCK


def _dot_nt(a, b):
    return lax.dot_general(a, b, (((1,), (1,)), ((), ())), preferred_element_type=F32)


def _attend(scores, values, extra=None):
    m = scores[0].max(axis=-1, keepdims=True)
    for s in scores[1:]:
        m = jnp.maximum(m, s.max(axis=-1, keepdims=True))
    if extra is not None:
        m = jnp.maximum(m, extra)
    den = None if extra is None else jnp.exp(extra - m)
    out = None
    for s, v in zip(scores, values):
        e = jnp.exp(s - m)
        d = e.sum(axis=-1, keepdims=True)
        den = d if den is None else den + d
        o = jnp.dot(e.astype(BF16), v, preferred_element_type=F32)
        out = o if out is None else out + o
    return out / den


def _win_kernel(sink_ref, q_ref, k_ref, v_ref, cos_ref, sin_ref, o_ref, qs, ks, vs):
    kv = pl.program_id(1)
    lane = lax.broadcasted_iota(I32, (A_BLOCK, HD), 1)
    first_half = (lane % 64) < 32

    def rope(zf, cos, sin):
        zr = jnp.where(first_half, pltpu.roll(zf, 96, 1), pltpu.roll(zf, 32, 1))
        return zf * cos + zr * sin

    ks[0:CTX, :] = k_ref[0:CTX, :]
    ks[SEG:KPAD, :] = jnp.zeros((A_BLOCK, HD), BF16)
    vs[0:SEG, :] = v_ref[...]
    vs[SEG:KPAD, :] = jnp.zeros((A_BLOCK, HD), BF16)

    def rope_blk(i, c):
        r0 = pl.multiple_of(i * A_BLOCK, A_BLOCK)
        cos = cos_ref[pl.ds(r0, A_BLOCK), :]
        sin = sin_ref[pl.ds(r0, A_BLOCK), :]
        ks[pl.ds(CTX + r0, A_BLOCK), :] = rope(k_ref[pl.ds(CTX + r0, A_BLOCK), :].astype(F32), cos, sin).astype(BF16)
        for g in range(A_GRP):
            zf = q_ref[pl.ds(CTX + r0, A_BLOCK), g * HD:(g + 1) * HD].astype(F32)
            qs[i, g * A_BLOCK:(g + 1) * A_BLOCK, :] = rope(zf, cos, sin).astype(BF16)
        return c

    lax.fori_loop(0, NBLK, rope_blk, 0)

    nq = A_GRP * A_BLOCK
    row = lax.broadcasted_iota(I32, (nq, 1), 0)
    sink = jnp.where(row < A_BLOCK, sink_ref[kv * A_GRP],
                     jnp.where(row < 2 * A_BLOCK, sink_ref[kv * A_GRP + 1], sink_ref[kv * A_GRP + 2]))
    r = lax.broadcasted_iota(I32, (nq, 3 * A_BLOCK), 0) % A_BLOCK
    c = lax.broadcasted_iota(I32, (nq, 3 * A_BLOCK), 1)
    band = (c >= r) & (c <= r + 2 * A_BLOCK)

    def blk(i, carry):
        w0 = pl.multiple_of(CTX - A_BLOCK + i * A_BLOCK, A_BLOCK)
        q = qs[i]
        s_c = _dot_nt(q, ks[0:CTX, :]) * ATT_SCALE
        s_w = _dot_nt(q, ks[pl.ds(w0, 3 * A_BLOCK), :]) * ATT_SCALE
        kpos = (i - 1) * A_BLOCK + c
        s_w = jnp.where(band & (kpos >= 0) & (kpos < SEQ), s_w, NEG)
        o = _attend([s_c, s_w], [vs[0:CTX, :], vs[pl.ds(w0, 3 * A_BLOCK), :]], sink)
        o0 = pl.multiple_of(CTX + i * A_BLOCK, A_BLOCK)
        for g in range(A_GRP):
            o_ref[pl.ds(o0, A_BLOCK), g * HD:(g + 1) * HD] = o[g * A_BLOCK:(g + 1) * A_BLOCK].astype(BF16)
        return carry

    lax.fori_loop(0, NBLK, blk, 0, unroll=2)

    for g in range(A_GRP):
        s = _dot_nt(q_ref[0:CTX, g * HD:(g + 1) * HD], k_ref[0:CTX, :]) * ATT_SCALE
        o = _attend([s], [v_ref[0:CTX, :]], jnp.full((CTX, 1), sink_ref[kv * A_GRP + g], F32))
        o_ref[0:CTX, g * HD:(g + 1) * HD] = o.astype(BF16)


def window_attn(z, sink_l, cos_t, sin_t):
    return pl.pallas_call(
        _win_kernel,
        grid=(BATCH, A_KV),
        in_specs=[pl.BlockSpec(memory_space=pltpu.SMEM),
                  pl.BlockSpec((SEG, A_GRP * HD), lambda b, kv: (b, kv)),
                  pl.BlockSpec((SEG, HD), lambda b, kv: (b, COL_AK + kv)),
                  pl.BlockSpec((SEG, HD), lambda b, kv: (b, COL_AV + kv)),
                  pl.BlockSpec((SEQ, HD), lambda b, kv: (0, 0)),
                  pl.BlockSpec((SEQ, HD), lambda b, kv: (0, 0))],
        out_specs=pl.BlockSpec((SEG, A_GRP * HD), lambda b, kv: (b, kv)),
        out_shape=jax.ShapeDtypeStruct((T, A_HEADS * HD), BF16),
        scratch_shapes=[pltpu.VMEM((NBLK, A_GRP * A_BLOCK, HD), BF16),
                        pltpu.VMEM((KPAD, HD), BF16),
                        pltpu.VMEM((KPAD, HD), BF16)],
        name="window_attn",
        compiler_params=_cparams(("arbitrary", "arbitrary")),
    )(sink_l, z, z, z, cos_t, sin_t)


GRID_ROWS = SEQ // GRID_W
NBQ_ROWS = 4
NBK_ROWS = 12
NBQ, NBK = NBQ_ROWS * GRID_W, NBK_ROWS * GRID_W
N_NBLK = GRID_ROWS // NBQ_ROWS
NB_BASE_MAX = GRID_ROWS - NBK_ROWS


def _nbr_key_base(first_row):
    return jnp.clip(first_row - NB_ROWS // 2, 0, NB_BASE_MAX)


N_DR, N_DC = 2 * NB_ROWS - 1, 2 * NB_COLS - 1


def _nbr_block_offsets():
    out = []
    for first_row in (0, NBQ_ROWS, GRID_ROWS - NBQ_ROWS):
        base = min(max(first_row - NB_ROWS // 2, 0), NB_BASE_MAX)
        kind = []
        for qi in range(NBQ_ROWS):
            r = first_row + qi
            start = min(max(r - NB_ROWS // 2, 0), GRID_ROWS - NB_ROWS)
            kind.append([kr - r + NB_ROWS - 1 if start <= kr < start + NB_ROWS else N_DR
                         for kr in range(base, base + NBK_ROWS)])
        out.append(kind)
    return out


def _nbr_build_bias(rpb_ref, h, half, bias):
    qc = lax.broadcasted_iota(I32, (GRID_W, 2 * GRID_W), 0)
    lane = lax.broadcasted_iota(I32, (GRID_W, 2 * GRID_W), 1)
    kc = lane % GRID_W
    start_c = jnp.clip(qc - NB_COLS // 2, 0, GRID_W - NB_COLS)
    col_ok = (kc >= start_c) & (kc < start_c + NB_COLS)
    dc = kc - qc + NB_COLS - 1
    left = lane < GRID_W
    for d in range(N_DR):
        t = jnp.full((GRID_W, 2 * GRID_W), NEG, F32)
        for j in range(N_DC):
            t = jnp.where(dc == j, rpb_ref[(h * N_DR + d) * N_DC + j], t)
        t = jnp.where(col_ok, t, NEG)
        half[d, 0] = jnp.where(left, t, NEG)
        half[d, 1] = jnp.where(left, NEG, t)
    masked = jnp.full((GRID_W, 2 * GRID_W), NEG, F32)
    for kind, per_q in enumerate(_nbr_block_offsets()):
        for qi, dr in enumerate(per_q):
            for p in range(NBK_ROWS // 2):
                lo = half[dr[2 * p], 0] if dr[2 * p] < N_DR else masked
                hi = half[dr[2 * p + 1], 1] if dr[2 * p + 1] < N_DR else masked
                bias[kind, qi * GRID_W:(qi + 1) * GRID_W, p * 2 * GRID_W:(p + 1) * 2 * GRID_W] = jnp.maximum(lo, hi)


def _nbr_kernel(rpb_ref, q_ref, k_ref, v_ref, o_ref, half, bias):
    @pl.when(pl.program_id(1) == 0)
    def _():
        _nbr_build_bias(rpb_ref, pl.program_id(0), half, bias)

    def block(i):
        kind = jnp.where(i == 0, 0, jnp.where(i == N_NBLK - 1, 2, 1))
        q0 = pl.multiple_of(CTX + i * NBQ, NBQ)
        k0 = pl.multiple_of(CTX + _nbr_key_base(i * NBQ_ROWS) * GRID_W, GRID_W)
        q = q_ref[pl.ds(q0, NBQ), :]
        s_c = _dot_nt(q, k_ref[0:CTX, :]) * ATT_SCALE
        s_n = _dot_nt(q, k_ref[pl.ds(k0, NBK), :]) * ATT_SCALE + bias[kind]
        o = _attend([s_c, s_n], [v_ref[0:CTX, :], v_ref[pl.ds(k0, NBK), :]])
        o_ref[pl.ds(q0, NBQ), :] = o.astype(BF16)

    def blocks(i, carry):
        block(2 * i)
        block(2 * i + 1)
        return carry

    lax.fori_loop(0, N_NBLK // 2, blocks, 0)

    s = _dot_nt(q_ref[0:CTX, :], k_ref[0:CTX, :]) * ATT_SCALE
    o_ref[0:CTX, :] = _attend([s], [v_ref[0:CTX, :]]).astype(BF16)


def nbr_attn(z, rpb_l):
    return pl.pallas_call(
        _nbr_kernel,
        grid=(C_HEADS, BATCH),
        in_specs=[pl.BlockSpec(memory_space=pltpu.SMEM),
                  pl.BlockSpec((SEG, HD), lambda h, b: (b, COL_CQ + h)),
                  pl.BlockSpec((SEG, HD), lambda h, b: (b, COL_CK + h)),
                  pl.BlockSpec((SEG, HD), lambda h, b: (b, COL_CV + h))],
        out_specs=pl.BlockSpec((SEG, HD), lambda h, b: (b, h)),
        out_shape=jax.ShapeDtypeStruct((T, C_HEADS * HD), BF16),
        scratch_shapes=[pltpu.VMEM((N_DR, 2, GRID_W, 2 * GRID_W), F32), pltpu.VMEM((3, NBQ, NBK), F32)],
        name="nbr_attn",
        compiler_params=_cparams(("arbitrary", "arbitrary")),
    )(rpb_l.reshape(-1), z, z, z)


N_LCHUNK = SEQ // CHUNK
PADR = 8
CONV_ROWS = PADR + CTX + PADR + SEQ + PADR


def _conv_off(r0):
    return jnp.where(r0 < CTX, r0 + PADR, r0 + 2 * PADR)


def _mlstm_kernel(bg_ref, q_ref, k_ref, v_ref, og_ref, gr_ref, cwq_ref, cwk_ref, y_ref,
                  qs, ks, stage, hf, hb):
    h = pl.program_id(1)
    rowi = lax.broadcasted_iota(I32, (CHUNK, 1), 0)

    def conv_silu(src_ref, cw_ref, dst, post):
        stage[...] = jnp.zeros((CONV_ROWS, HD), F32)
        stage[PADR:PADR + CTX, :] = src_ref[0:CTX, :].astype(F32)
        stage[2 * PADR + CTX:2 * PADR + SEG, :] = src_ref[CTX:SEG, :].astype(F32)

        def blk(c, carry):
            r0 = pl.multiple_of(c * CHUNK, CHUNK)
            s0 = pl.multiple_of(_conv_off(r0), PADR)
            cur = stage[pl.ds(s0, CHUNK), :]
            prev_last = stage[pl.ds(s0 - PADR, PADR), :][PADR - 1:PADR, :]
            next_first = stage[pl.ds(s0 + CHUNK, PADR), :][0:1, :]
            zm = jnp.where(rowi == 0, prev_last, pltpu.roll(cur, 1, 0))
            zp = jnp.where(rowi == CHUNK - 1, next_first, pltpu.roll(cur, CHUNK - 1, 0))
            y = zm * cw_ref[0:1, :] + cur * cw_ref[1:2, :] + zp * cw_ref[2:3, :]
            dst[pl.ds(r0, CHUNK), :] = (_silu(y) * post).astype(BF16)
            return carry

        lax.fori_loop(0, SEG // CHUNK, blk, 0)

    conv_silu(q_ref, cwq_ref, qs, 1.0)
    conv_silu(k_ref, cwk_ref, ks, ATT_SCALE)

    t_idx = lax.broadcasted_iota(I32, (CHUNK, CHUNK), 0)
    s_idx = lax.broadcasted_iota(I32, (CHUNK, CHUNK), 1)

    def chunk(r0, bwd, state, dst):
        c_mat, n_vec, m_prev = state
        ki, kf = (2, 3) if bwd else (0, 1)
        bi = bg_ref[ki * B_HEADS + h]
        bf = bg_ref[kf * B_HEADS + h]
        qc = qs[pl.ds(r0, CHUNK), :]
        kc = ks[pl.ds(r0, CHUNK), :]
        vc = v_ref[pl.ds(r0, CHUNK), :]
        i_row = gr_ref[pl.ds(ki * B_HEADS + h, 1), pl.ds(r0, CHUNK)] + bi
        lf_row = _log_sigmoid(gr_ref[pl.ds(kf * B_HEADS + h, 1), pl.ds(r0, CHUNK)] + bf)
        causal = (s_idx >= t_idx) if bwd else (s_idx <= t_idx)
        diag = s_idx == t_idx
        b_col = jnp.sum(jnp.where(causal, lf_row, 0.0), axis=1, keepdims=True)
        b_row = jnp.sum(jnp.where(diag, b_col, 0.0), axis=0, keepdims=True)
        a_row = i_row - b_row
        a_col = jnp.sum(jnp.where(diag, a_row, 0.0), axis=1, keepdims=True)
        mx = jnp.maximum(m_prev, jnp.max(jnp.where(causal, a_row, NEG), axis=1, keepdims=True))
        dm = jnp.exp(jnp.where(causal, a_row - mx, NEG))
        s = _dot_nt(qc, kc) * dm
        w_int = jnp.exp(m_prev - mx)
        num = (jnp.dot(s.astype(BF16), vc, preferred_element_type=F32)
               + w_int * jnp.dot(qc, c_mat.astype(BF16), preferred_element_type=F32))
        den = (jnp.sum(s, axis=1, keepdims=True)
               + w_int * jnp.sum(qc.astype(F32) * n_vec, axis=1, keepdims=True))
        m_t = b_col + mx
        dst[pl.ds(r0, CHUNK), :] = num / jnp.maximum(jnp.abs(den), jnp.exp(-m_t))
        b_end = jnp.sum(lf_row, axis=1, keepdims=True)
        m_end = jnp.maximum(m_prev, jnp.max(a_row, axis=1, keepdims=True))
        decay = jnp.exp(m_prev - m_end)
        w_col = jnp.exp(a_col - m_end)
        kv = lax.dot_general(kc, (w_col * vc.astype(F32)).astype(BF16), (((0,), (0,)), ((), ())),
                             preferred_element_type=F32)
        c_new = decay * c_mat + kv
        n_new = decay * n_vec + jnp.sum(w_col * kc.astype(F32), axis=0, keepdims=True)
        return c_new, n_new, b_end + m_end

    zero = (jnp.zeros((HD, HD), F32), jnp.zeros((1, HD), F32), jnp.zeros((1, 1), F32))
    st_f = chunk(0, False, zero, hf)
    st_b = chunk(0, True, zero, hb)

    def body(j, carry):
        sf, sb = carry
        rf = pl.multiple_of(CTX + j * CHUNK, CHUNK)
        rb = pl.multiple_of(CTX + (N_LCHUNK - 1 - j) * CHUNK, CHUNK)
        return chunk(rf, False, sf, hf), chunk(rb, True, sb, hb)

    lax.fori_loop(0, N_LCHUNK, body, (st_f, st_b))

    def fin(c, carry):
        r0 = pl.multiple_of(c * CHUNK, CHUNK)
        gate = jax.nn.sigmoid(og_ref[pl.ds(r0, CHUNK), :].astype(F32))
        y_ref[pl.ds(r0, CHUNK), :] = (gate * (hf[pl.ds(r0, CHUNK), :] + hb[pl.ds(r0, CHUNK), :])).astype(BF16)
        return carry

    lax.fori_loop(0, SEG // CHUNK, fin, 0)


def mlstm(z, g_row, b_gates_l, conv_l):
    return pl.pallas_call(
        _mlstm_kernel,
        grid=(BATCH, B_HEADS),
        in_specs=[pl.BlockSpec(memory_space=pltpu.SMEM),
                  pl.BlockSpec((SEG, HD), lambda b, h: (b, COL_BQ + h)),
                  pl.BlockSpec((SEG, HD), lambda b, h: (b, COL_BK + h)),
                  pl.BlockSpec((SEG, HD), lambda b, h: (b, COL_BV + h)),
                  pl.BlockSpec((SEG, HD), lambda b, h: (b, COL_BO + h)),
                  pl.BlockSpec((4 * B_HEADS, SEG), lambda b, h: (0, b)),
                  pl.BlockSpec((3, HD), lambda b, h: (0, h)),
                  pl.BlockSpec((3, HD), lambda b, h: (0, B_HEADS + h))],
        out_specs=pl.BlockSpec((SEG, HD), lambda b, h: (b, h)),
        out_shape=jax.ShapeDtypeStruct((T, B_HEADS * HD), BF16),
        scratch_shapes=[pltpu.VMEM((SEG, HD), BF16), pltpu.VMEM((SEG, HD), BF16),
                        pltpu.VMEM((CONV_ROWS, HD), F32),
                        pltpu.VMEM((SEG, HD), F32), pltpu.VMEM((SEG, HD), F32)],
        name="mlstm",
        compiler_params=_cparams(("arbitrary", "arbitrary")),
    )(b_gates_l, z, z, z, z, g_row, conv_l, conv_l)


OUT_TM = SEG // 6


def _top2_routing(biased, scores, comb_ref, sel_ref):
    rows = [biased[e:e + 1, :] for e in range(N_EXPERTS)]
    gscore = []
    for g in range(N_GROUPS):
        v0, v1, v2, v3 = rows[4 * g:4 * g + 4]
        hi01, lo01 = jnp.maximum(v0, v1), jnp.minimum(v0, v1)
        hi23, lo23 = jnp.maximum(v2, v3), jnp.minimum(v2, v3)
        gscore.append(jnp.maximum(hi01, hi23) + jnp.maximum(jnp.minimum(hi01, hi23), jnp.maximum(lo01, lo23)))
    picked = []
    total = None
    for e in range(N_EXPERTS):
        g = e // PER_GROUP
        ok = None
        for o in range(N_GROUPS):
            if o != g:
                c = (gscore[g] > gscore[o]) if o < g else (gscore[g] >= gscore[o])
                ok = c if ok is None else (ok & c)
        ahead = jnp.zeros_like(rows[e])
        for o in range(4 * g, 4 * g + 4):
            if o != e:
                c = (rows[o] > rows[e]) if o > e else (rows[o] >= rows[e])
                ahead = ahead + jnp.where(c, 1.0, 0.0)
        sel = jnp.where(ok & (ahead < 2.0), 1.0, 0.0)
        sel_ref[e:e + 1, :] = sel
        picked.append(sel * scores[e:e + 1, :])
        total = picked[-1] if total is None else total + picked[-1]
    for e in range(N_EXPERTS):
        comb_ref[e:e + 1, :] = picked[e] / total


OUT_RC = 16


N_OUT_TILES = T // OUT_TM


def _out_kernel(ya_ref, yb_ref, yn_ref, w_ref, x_ref, mod_ref, g_ref, wrh_ref, wrl_ref, br_ref,
                x1_ref, h2_ref, comb_ref, sel_ref, ycat, acc, acc_next, h_hi, h_lo):
    i = pl.program_id(0)

    def matmul():
        ycat[:, 0:A_HEADS * HD] = ya_ref[...]
        ycat[:, A_HEADS * HD:(A_HEADS + B_HEADS) * HD] = yb_ref[...]
        ycat[:, (A_HEADS + B_HEADS) * HD:D] = yn_ref[...]
        acc_next[...] = jnp.dot(ycat[...], w_ref[...], preferred_element_type=F32)

    def epilogue():
        t = i - 1
        batch = t // 6
        for c in range(OUT_TM // OUT_RC):
            rows = slice(c * OUT_RC, (c + 1) * OUT_RC)
            mrow = jnp.where((t % 6) * OUT_TM + c * OUT_RC < CTX, 4, batch)

            def mod(k):
                return mod_ref[pl.ds(mrow, 1), k * D:(k + 1) * D]

            x1 = x_ref[rows, :] + mod(2) * acc[rows, :]
            x1_ref[rows, :] = x1
            h2 = (_rms(x1) * g_ref[...]) * (1.0 + mod(4)) + mod(3)
            h2_ref[rows, :] = h2
            hi = h2.astype(BF16)
            h_hi[rows, :] = hi
            h_lo[rows, :] = (h2 - hi.astype(F32)).astype(BF16)
        scores = jax.nn.sigmoid(_dot_nt(wrh_ref[...], h_hi[...]) + _dot_nt(wrl_ref[...], h_hi[...])
                                + _dot_nt(wrh_ref[...], h_lo[...]))
        _top2_routing(scores + br_ref[...], scores, comb_ref, sel_ref)

    @pl.when(i == 0)
    def _():
        matmul()

    @pl.when((i > 0) & (i < N_OUT_TILES))
    def _():
        matmul()
        epilogue()

    @pl.when(i == N_OUT_TILES)
    def _():
        epilogue()

    @pl.when(i < N_OUT_TILES)
    def _():
        acc[...] = acc_next[...]


def out_proj(ya, yb, yn, w_out_b, xa, mod_l, g, w_router_hi, w_router_lo, b_router_c, layer):
    cur = lambda i: (jnp.minimum(i, N_OUT_TILES - 1), 0)
    row = lambda i: (jnp.maximum(i - 1, 0), 0)
    fixed = lambda i: (0, 0)
    return pl.pallas_call(
        _out_kernel,
        grid=(N_OUT_TILES + 1,),
        in_specs=[pl.BlockSpec((OUT_TM, A_HEADS * HD), cur),
                  pl.BlockSpec((OUT_TM, B_HEADS * HD), cur),
                  pl.BlockSpec((OUT_TM, C_HEADS * HD), cur),
                  pl.BlockSpec((None, D, D), lambda i: (layer, 0, 0)),
                  pl.BlockSpec((OUT_TM, D), row),
                  pl.BlockSpec((8, 6 * D), fixed),
                  pl.BlockSpec((1, D), fixed),
                  pl.BlockSpec((N_EXPERTS, D), fixed),
                  pl.BlockSpec((N_EXPERTS, D), fixed),
                  pl.BlockSpec((N_EXPERTS, 1), fixed)],
        out_specs=[pl.BlockSpec((OUT_TM, D), row),
                   pl.BlockSpec((OUT_TM, D), row),
                   pl.BlockSpec((N_EXPERTS, OUT_TM), lambda i: (0, jnp.maximum(i - 1, 0))),
                   pl.BlockSpec((N_EXPERTS, OUT_TM), lambda i: (0, jnp.maximum(i - 1, 0)))],
        out_shape=[jax.ShapeDtypeStruct((T, D), F32),
                   jax.ShapeDtypeStruct((T, D), F32),
                   jax.ShapeDtypeStruct((N_EXPERTS, T), F32),
                   jax.ShapeDtypeStruct((N_EXPERTS, T), F32)],
        scratch_shapes=[pltpu.VMEM((OUT_TM, D), BF16), pltpu.VMEM((OUT_TM, D), F32), pltpu.VMEM((OUT_TM, D), F32),
                        pltpu.VMEM((OUT_TM, D), BF16), pltpu.VMEM((OUT_TM, D), BF16)],
        name="out_proj",
        compiler_params=_cparams(("arbitrary",)),
    )(ya, yb, yn, w_out_b, xa, mod_l, g, w_router_hi, w_router_lo, b_router_c)


SC_TM = 1024
ROW_DMA_UNROLL = 8


def _scatter_kernel(pos0_ref, pos1_ref, ztile_ref, h_ref, hs_ref, zero_scr, sem, zsem):
    i = pl.program_id(0)

    @pl.when(i == 0)
    def _():
        zero_scr[...] = jnp.zeros((ETILE, D), F32)

        def fill_copy(t):
            return pltpu.make_async_copy(zero_scr, hs_ref.at[pl.ds(pl.multiple_of(t * ETILE, ETILE), ETILE), :], zsem)

        def fill_start(e, c):
            @pl.when(ztile_ref[e] >= 0)
            def _():
                fill_copy(ztile_ref[e]).start()
            return c

        def fill_wait(e, c):
            @pl.when(ztile_ref[e] >= 0)
            def _():
                fill_copy(ztile_ref[e]).wait()
            return c

        lax.fori_loop(0, 2 * N_EXPERTS, fill_start, 0)
        lax.fori_loop(0, 2 * N_EXPERTS, fill_wait, 0)

    def row_copy(r, p):
        return pltpu.make_async_copy(h_ref.at[pl.ds(r, 1), :], hs_ref.at[pl.ds(p, 1), :], sem)

    def issue(g, c):
        r0 = pl.multiple_of(g * ROW_DMA_UNROLL, ROW_DMA_UNROLL)
        for u in range(ROW_DMA_UNROLL):
            t = i * SC_TM + r0 + u
            row_copy(r0 + u, pos0_ref[t]).start()
            row_copy(r0 + u, pos1_ref[t]).start()
        return c

    lax.fori_loop(0, SC_TM // ROW_DMA_UNROLL, issue, 0)

    def drain(r, c):
        row_copy(0, 0).wait()
        row_copy(0, 0).wait()
        return c

    lax.fori_loop(0, SC_TM, drain, 0, unroll=ROW_DMA_UNROLL)


def scatter_rows(pos0, pos1, ztile, h2):
    return pl.pallas_call(
        _scatter_kernel,
        grid_spec=pltpu.PrefetchScalarGridSpec(
            num_scalar_prefetch=3,
            grid=(T // SC_TM,),
            in_specs=[pl.BlockSpec((SC_TM, D), lambda i, *_: (i, 0))],
            out_specs=pl.BlockSpec(memory_space=pl.ANY),
            scratch_shapes=[pltpu.VMEM((ETILE, D), F32), pltpu.SemaphoreType.DMA, pltpu.SemaphoreType.DMA]),
        out_shape=jax.ShapeDtypeStruct((P_ROWS, D), F32),
        name="scatter_rows",
        compiler_params=_cparams(("arbitrary",)),
    )(pos0, pos1, ztile, h2)


CAST_ROWS = 256


def _cast_into(dst, src, n_rows):
    def body(c, carry):
        r0 = pl.multiple_of(c * CAST_ROWS, CAST_ROWS)
        dst[pl.ds(r0, CAST_ROWS), :] = src[pl.ds(r0, CAST_ROWS), :].astype(BF16)
        return carry

    lax.fori_loop(0, n_rows // CAST_ROWS, body, 0)


def _ffn_kernel(te_ref, tv_ref, first_ref, nxt_ref, hs_ref, w1_hbm, w3_hbm, w2_hbm, ys_ref,
                st1, st3, st2, w1b, w3b, w2b, sem, *, layer):
    j = pl.program_id(0)

    def fetch(e):
        return (pltpu.make_async_copy(w1_hbm.at[layer, e], st1, sem.at[0]),
                pltpu.make_async_copy(w3_hbm.at[layer, e], st3, sem.at[1]),
                pltpu.make_async_copy(w2_hbm.at[layer, e], st2, sem.at[2]))

    @pl.when(j == 0)
    def _():
        for cp in fetch(te_ref[0]):
            cp.start()

    @pl.when(first_ref[j] == 1)
    def _():
        for cp in fetch(te_ref[j]):
            cp.wait()
        _cast_into(w1b, st1, D)
        _cast_into(w3b, st3, D)
        _cast_into(w2b, st2, FF)

        @pl.when(nxt_ref[j] >= 0)
        def _():
            for cp in fetch(nxt_ref[j]):
                cp.start()

    @pl.when(tv_ref[j] > 0)
    def _():
        xb = hs_ref[...].astype(BF16)
        a = jnp.dot(xb, w1b[...], preferred_element_type=F32)
        b = jnp.dot(xb, w3b[...], preferred_element_type=F32)
        act = (_silu(a) * b).astype(BF16)
        ys_ref[...] = jnp.dot(act, w2b[...], preferred_element_type=F32)

    @pl.when(tv_ref[j] == 0)
    def _():
        ys_ref[...] = jnp.zeros((ETILE, D), F32)


def _tile_or_first(j, te, tv, *_):
    return jnp.where(tv[j] > 0, j, 0)


def ffn(plan, hs, w1, w3, w2, layer):
    return pl.pallas_call(
        functools.partial(_ffn_kernel, layer=layer),
        grid_spec=pltpu.PrefetchScalarGridSpec(
            num_scalar_prefetch=4,
            grid=(N_ETILES,),
            in_specs=[pl.BlockSpec((ETILE, D), lambda j, *p: (_tile_or_first(j, *p), 0)),
                      pl.BlockSpec(memory_space=pl.ANY),
                      pl.BlockSpec(memory_space=pl.ANY),
                      pl.BlockSpec(memory_space=pl.ANY)],
            out_specs=pl.BlockSpec((ETILE, D), lambda j, *p: (j, 0)),
            scratch_shapes=[pltpu.VMEM((D, FF), F32), pltpu.VMEM((D, FF), F32), pltpu.VMEM((FF, D), F32),
                            pltpu.VMEM((D, FF), BF16), pltpu.VMEM((D, FF), BF16), pltpu.VMEM((FF, D), BF16),
                            pltpu.SemaphoreType.DMA((3,))]),
        out_shape=jax.ShapeDtypeStruct((P_ROWS, D), F32),
        name="ffn",
        compiler_params=_cparams(("arbitrary",)),
    )(*plan, hs, w1, w3, w2)


CB_TM = SEG // 9


CB_RC = 32


def _combine_kernel(pos0_ref, pos1_ref, ys_ref, x_ref, w_ref, mod_ref, o_ref, buf, sem):
    i = pl.program_id(0)
    n = pl.num_programs(0)
    slot = i % 2

    def row_copy(s, k, r, p):
        return pltpu.make_async_copy(ys_ref.at[pl.ds(p, 1), :], buf.at[s, k, pl.ds(r, 1), :], sem.at[s])

    def gather(tile, s):
        def issue(g, c):
            r0 = pl.multiple_of(g * ROW_DMA_UNROLL, ROW_DMA_UNROLL)
            for u in range(ROW_DMA_UNROLL):
                t = tile * CB_TM + r0 + u
                row_copy(s, 0, r0 + u, pos0_ref[t]).start()
                row_copy(s, 1, r0 + u, pos1_ref[t]).start()
            return c

        lax.fori_loop(0, CB_TM // ROW_DMA_UNROLL, issue, 0)

    @pl.when(i == 0)
    def _():
        gather(0, 0)

    @pl.when(i + 1 < n)
    def _():
        gather(i + 1, 1 - slot)

    def drain(r, c):
        row_copy(slot, 0, 0, 0).wait()
        row_copy(slot, 1, 0, 0).wait()
        return c

    lax.fori_loop(0, CB_TM, drain, 0, unroll=ROW_DMA_UNROLL)
    gate = mod_ref[pl.ds(jnp.where((i % 9) == 0, 4, i // 9), 1), 5 * D:6 * D]

    def chunk(c, carry):
        r0 = pl.multiple_of(c * CB_RC, CB_RC)
        rows = pl.ds(r0, CB_RC)
        f = w_ref[rows, 0:1] * buf[slot, 0, rows, :] + w_ref[rows, 1:2] * buf[slot, 1, rows, :]
        o_ref[rows, :] = x_ref[rows, :] + gate * f
        return carry

    lax.fori_loop(0, CB_TM // CB_RC, chunk, 0)


def combine_rows(pos0, pos1, ys, x1, w01, mod_l):
    return pl.pallas_call(
        _combine_kernel,
        grid_spec=pltpu.PrefetchScalarGridSpec(
            num_scalar_prefetch=2,
            grid=(T // CB_TM,),
            in_specs=[pl.BlockSpec(memory_space=pl.ANY),
                      pl.BlockSpec((CB_TM, D), lambda i, *_: (i, 0)),
                      pl.BlockSpec((CB_TM, 2), lambda i, *_: (i, 0)),
                      pl.BlockSpec((8, 6 * D), lambda i, *_: (0, 0))],
            out_specs=pl.BlockSpec((CB_TM, D), lambda i, *_: (i, 0)),
            scratch_shapes=[pltpu.VMEM((2, 2, CB_TM, D), F32), pltpu.SemaphoreType.DMA((2,))]),
        out_shape=jax.ShapeDtypeStruct((T, D), F32),
        name="combine_rows",
        compiler_params=_cparams(("arbitrary",)),
    )(pos0, pos1, ys, x1, w01, mod_l)


def route_plan(comb_t, sel_t):
    sel = sel_t > 0.5
    cnt = jnp.sum(sel, axis=1).astype(I32)
    tiles = (cnt + ETILE - 1) // ETILE
    tend = jnp.cumsum(tiles)
    toff = tend - tiles
    rank = jnp.cumsum(sel.astype(I32), axis=1) - 1
    pos = toff[:, None] * ETILE + rank
    pos0 = jnp.min(jnp.where(sel, pos, P_ROWS), axis=0).astype(I32)
    pos1 = jnp.max(jnp.where(sel, pos, -1), axis=0).astype(I32)
    w0 = jnp.sum(jnp.where(sel & (pos == pos0[None]), comb_t, 0.0), axis=0)
    w1 = jnp.sum(jnp.where(sel & (pos == pos1[None]), comb_t, 0.0), axis=0)
    n_used = tend[-1]
    tidx = jnp.arange(N_ETILES, dtype=I32)
    te_raw = jnp.sum((tend[None, :] <= tidx[:, None]).astype(I32), axis=1)
    te_last = jnp.sum((tend <= n_used - 1).astype(I32))
    te = jnp.where(tidx < n_used, te_raw, te_last).astype(I32)
    te = jnp.minimum(te, N_EXPERTS - 1)
    tv = jnp.clip(cnt[te] - (tidx - toff[te]) * ETILE, 0, ETILE)
    tv = jnp.where(tidx < n_used, tv, 0).astype(I32)
    tail = n_used + jnp.arange(N_EXPERTS, dtype=I32)
    ztile = jnp.concatenate([jnp.where(tiles > 0, tend - 1, -1), jnp.where(tail < N_ETILES, tail, -1)]).astype(I32)
    used = tidx < n_used
    first = (used & ((tidx == 0) | (te != jnp.roll(te, 1)))).astype(I32)
    nxt_tile = tend[te]
    nxt = jnp.where(nxt_tile < n_used, te[jnp.minimum(nxt_tile, N_ETILES - 1)], -1).astype(I32)
    return pos0, pos1, jnp.stack([w0, w1], axis=1), (te, tv, first, nxt), ztile


def _final_kernel(x_ref, g_ref, o_ref):
    o_ref[...] = _rms(x_ref[...]) * g_ref[...]


def final_norm(xa, g):
    nb = SEQ // CTX
    return pl.pallas_call(
        _final_kernel,
        grid=(BATCH, nb),
        in_specs=[pl.BlockSpec((CTX, D), lambda b, j: (b * (nb + 1) + 1 + j, 0)),
                  pl.BlockSpec((1, D), lambda b, j: (0, 0))],
        out_specs=pl.BlockSpec((CTX, D), lambda b, j: (b * nb + j, 0)),
        out_shape=jax.ShapeDtypeStruct((BATCH * SEQ, D), F32),
        name="final_norm",
        compiler_params=_cparams(("arbitrary", "arbitrary")),
    )(xa, g)


def _rope_tables():
    t = jnp.arange(SEQ, dtype=I32)
    row = (t // GRID_W).astype(F32)
    col = (t % GRID_W).astype(F32)
    n_freq = HD // 4
    inv_freq = ROPE_BASE ** (-jnp.arange(n_freq, dtype=F32) / n_freq)
    ar = row[:, None] * inv_freq[None, :]
    ac = col[:, None] * inv_freq[None, :]
    cos_t = jnp.concatenate([jnp.cos(ar), jnp.cos(ar), jnp.cos(ac), jnp.cos(ac)], axis=1)
    sin_t = jnp.concatenate([-jnp.sin(ar), jnp.sin(ar), -jnp.sin(ac), jnp.sin(ac)], axis=1)
    return cos_t, sin_t


GATE_LO, GATE_HI = 3328, 3344
D_IN = N_MAIN + 4 * B_HEADS
D_IN_PAD = N_MAIN + HD
RP_ROWS = 256


def _repack_kernel(w_ref, o_ref, g_ref):
    o_ref[0, :, 0:GATE_LO] = w_ref[0, :, 0:GATE_LO].astype(BF16)
    o_ref[0, :, GATE_LO:N_MAIN] = w_ref[0, :, GATE_HI:D_IN].astype(BF16)
    lane = lax.broadcasted_iota(I32, (RP_ROWS, HD), 1)
    g_ref[0] = jnp.where(lane < 4 * B_HEADS, w_ref[0, :, GATE_LO:GATE_LO + HD], 0.0).astype(BF16)


def repack_w_in(w_in):
    blk = lambda l, r: (l, r, 0)
    return pl.pallas_call(
        _repack_kernel,
        grid=(DEPTH, D // RP_ROWS),
        in_specs=[pl.BlockSpec((1, RP_ROWS, D_IN_PAD), blk)],
        out_specs=[pl.BlockSpec((1, RP_ROWS, N_MAIN), blk), pl.BlockSpec((1, RP_ROWS, HD), blk)],
        out_shape=[jax.ShapeDtypeStruct((DEPTH, D, N_MAIN), BF16), jax.ShapeDtypeStruct((DEPTH, D, HD), BF16)],
        name="repack_w_in",
        compiler_params=_cparams(("arbitrary", "arbitrary")),
    )(w_in)


def kernel(x, c, ctx, c_ctx, w_ada, b_ada, norm_mix, norm_ffn, w_in, b_gates, conv_qk, sink, rpb, w_out,
           w_router, b_router, w1, w3, w2, norm_final):
    xa = jnp.concatenate([ctx, x], axis=1).reshape(T, D)
    cond8 = jnp.concatenate([c, c_ctx[None], jnp.zeros((3, D), F32)], axis=0)
    mod = ada_all(cond8, w_ada, b_ada)
    cos_t, sin_t = _rope_tables()
    w_main, w_gate = repack_w_in(jnp.pad(w_in.astype(BF16), ((0, 0), (0, 0), (0, D_IN_PAD - D_IN))))
    w_out_b = w_out.astype(BF16)
    w_router_hi = w_router.T.astype(BF16)
    w_router_lo = (w_router.T - w_router_hi.astype(F32)).astype(BF16)
    b_router_c = b_router.reshape(N_EXPERTS, 1)

    for l in range(DEPTH):
        z, zg = in_proj(xa, norm_mix[l].reshape(1, D), mod[l], w_main, w_gate, l)
        g_row = zg[:, :4 * B_HEADS].T
        ya = window_attn(z, sink[l], cos_t, sin_t)
        yb = mlstm(z, g_row, b_gates[l], conv_qk[l])
        yn = nbr_attn(z, rpb[l])
        x1, h2, comb_t, sel_t = out_proj(ya, yb, yn, w_out_b, xa, mod[l], norm_ffn[l].reshape(1, D),
                                         w_router_hi, w_router_lo, b_router_c, l)
        pos0, pos1, w01, plan, ztile = route_plan(comb_t, sel_t)
        hs = scatter_rows(pos0, pos1, ztile, h2)
        ys = ffn(plan, hs, w1, w3, w2, l)
        xa = combine_rows(pos0, pos1, ys, x1, w01, mod[l])
    return final_norm(xa, norm_final.reshape(1, D)).reshape(BATCH, SEQ, D)
```

```python
import functools

import jax
import jax.numpy as jnp
from jax import lax
from jax.experimental import pallas as pl
from jax.experimental.pallas import tpu as pltpu

F32 = jnp.float32
BF16 = jnp.bfloat16
I32 = jnp.int32

D = 2048
BATCH = 4
SEQ = 2048
CTX = 256
SEG = CTX + SEQ
T = BATCH * SEG
DEPTH = 4
GRID_W = 64
HD = 128
A_HEADS, A_KV, A_GRP = 6, 2, 3
B_HEADS = 4
C_HEADS = 6
A_BLOCK = 128
NB_ROWS, NB_COLS = 8, 16
CHUNK = 256
N_EXPERTS, N_GROUPS, PER_GROUP = 16, 4, 4
FF = 1024
EPS = 1e-6
ROPE_BASE = 10000.0
NEG = -1e30
ATT_SCALE = HD ** -0.5

N_MAIN = 5632
COL_AQ, COL_AK, COL_AV = 0, 6, 8
COL_BQ, COL_BK, COL_BV, COL_BO = 10, 14, 18, 22
COL_CQ, COL_CK, COL_CV = 26, 32, 38

ETILE = 256
N_ETILES = (2 * T) // ETILE + N_EXPERTS
P_ROWS = N_ETILES * ETILE

VMEM_LIMIT = 56 * 1024 * 1024


def _cparams(sem):
    return pltpu.CompilerParams(dimension_semantics=sem, vmem_limit_bytes=VMEM_LIMIT)


def _silu(v):
    return v * jax.nn.sigmoid(v)


def _log_sigmoid(v):
    return jnp.minimum(v, 0.0) - jnp.log1p(jnp.exp(-jnp.abs(v)))


ADA_TN = 1024


def _ada_kernel(s_ref, w_ref, b_ref, o_ref):
    s = _silu(s_ref[...]).astype(BF16)
    o_ref[0] = jnp.dot(s, w_ref[0].astype(BF16), preferred_element_type=F32) + b_ref[0]


def ada_all(cond8, w_ada, b_ada):
    n = w_ada.shape[-1]
    return pl.pallas_call(
        _ada_kernel,
        grid=(DEPTH, n // ADA_TN),
        in_specs=[pl.BlockSpec((8, D), lambda l, j: (0, 0)),
                  pl.BlockSpec((1, D, ADA_TN), lambda l, j: (l, 0, j)),
                  pl.BlockSpec((1, 1, ADA_TN), lambda l, j: (l, 0, j))],
        out_specs=pl.BlockSpec((1, 8, ADA_TN), lambda l, j: (l, 0, j)),
        out_shape=jax.ShapeDtypeStruct((DEPTH, 8, n), F32),
        name="ada_mod",
        compiler_params=_cparams(("arbitrary", "arbitrary")),
    )(cond8, w_ada, b_ada.reshape(DEPTH, 1, n))


def _row_mod(mod_ref, chunk, batch, is_ctx):
    lat = mod_ref[pl.ds(batch, 1), chunk * D:(chunk + 1) * D]
    ctx = mod_ref[4:5, chunk * D:(chunk + 1) * D]
    return jnp.where(is_ctx, ctx, lat)


def _rms(x):
    return x * lax.rsqrt(jnp.mean(x * x, axis=-1, keepdims=True) + EPS)


IN_TM = SEG // 2
IN_TN = 1408


def _in_kernel(x_ref, g_ref, mod_ref, w_ref, wg_ref, z_ref, zg_ref, h_scr):
    i = pl.program_id(0)
    j = pl.program_id(1)

    @pl.when(j == 0)
    def _():
        batch = i // 2
        rows = lax.broadcasted_iota(I32, (IN_TM, 1), 0) + (i % 2) * IN_TM
        is_ctx = rows < CTX
        xn = _rms(x_ref[...]) * g_ref[...]
        h = xn * (1.0 + _row_mod(mod_ref, 1, batch, is_ctx)) + _row_mod(mod_ref, 0, batch, is_ctx)
        hb = h.astype(BF16)
        h_scr[...] = hb
        zg_ref[...] = jnp.dot(hb, wg_ref[...], preferred_element_type=F32)

    z_ref[...] = jnp.dot(h_scr[...], w_ref[...], preferred_element_type=F32).astype(BF16)


def in_proj(xa, g, mod_l, w_main, w_gate, layer):
    return pl.pallas_call(
        _in_kernel,
        grid=(T // IN_TM, N_MAIN // IN_TN),
        in_specs=[pl.BlockSpec((IN_TM, D), lambda i, j: (i, 0)),
                  pl.BlockSpec((1, D), lambda i, j: (0, 0)),
                  pl.BlockSpec((8, 6 * D), lambda i, j: (0, 0)),
                  pl.BlockSpec((None, D, IN_TN), lambda i, j: (layer, 0, j)),
                  pl.BlockSpec((None, D, HD), lambda i, j: (layer, 0, 0))],
        out_specs=[pl.BlockSpec((IN_TM, IN_TN), lambda i, j: (i, j)),
                   pl.BlockSpec((IN_TM, HD), lambda i, j: (i, 0))],
        out_shape=[jax.ShapeDtypeStruct((T, N_MAIN), BF16),
                   jax.ShapeDtypeStruct((T, HD), F32)],
        scratch_shapes=[pltpu.VMEM((IN_TM, D), BF16)],
        name="in_proj",
        compiler_params=_cparams(("arbitrary", "arbitrary")),
    )(xa, g, mod_l, w_main, w_gate)


NBLK = SEQ // A_BLOCK
KPAD = SEG + A_BLOCK


def _dot_nt(a, b):
    return lax.dot_general(a, b, (((1,), (1,)), ((), ())), preferred_element_type=F32)


def _attend(scores, values, extra=None):
    m = scores[0].max(axis=-1, keepdims=True)
    for s in scores[1:]:
        m = jnp.maximum(m, s.max(axis=-1, keepdims=True))
    if extra is not None:
        m = jnp.maximum(m, extra)
    den = None if extra is None else jnp.exp(extra - m)
    out = None
    for s, v in zip(scores, values):
        e = jnp.exp(s - m)
        d = e.sum(axis=-1, keepdims=True)
        den = d if den is None else den + d
        o = jnp.dot(e.astype(BF16), v, preferred_element_type=F32)
        out = o if out is None else out + o
    return out / den


def _win_kernel(sink_ref, q_ref, k_ref, v_ref, cos_ref, sin_ref, o_ref, qs, ks, vs):
    kv = pl.program_id(1)
    lane = lax.broadcasted_iota(I32, (A_BLOCK, HD), 1)
    first_half = (lane % 64) < 32

    def rope(zf, cos, sin):
        zr = jnp.where(first_half, pltpu.roll(zf, 96, 1), pltpu.roll(zf, 32, 1))
        return zf * cos + zr * sin

    ks[0:CTX, :] = k_ref[0:CTX, :]
    ks[SEG:KPAD, :] = jnp.zeros((A_BLOCK, HD), BF16)
    vs[0:SEG, :] = v_ref[...]
    vs[SEG:KPAD, :] = jnp.zeros((A_BLOCK, HD), BF16)

    def rope_blk(i, c):
        r0 = pl.multiple_of(i * A_BLOCK, A_BLOCK)
        cos = cos_ref[pl.ds(r0, A_BLOCK), :]
        sin = sin_ref[pl.ds(r0, A_BLOCK), :]
        ks[pl.ds(CTX + r0, A_BLOCK), :] = rope(k_ref[pl.ds(CTX + r0, A_BLOCK), :].astype(F32), cos, sin).astype(BF16)
        for g in range(A_GRP):
            zf = q_ref[pl.ds(CTX + r0, A_BLOCK), g * HD:(g + 1) * HD].astype(F32)
            qs[i, g * A_BLOCK:(g + 1) * A_BLOCK, :] = rope(zf, cos, sin).astype(BF16)
        return c

    lax.fori_loop(0, NBLK, rope_blk, 0)

    nq = A_GRP * A_BLOCK
    row = lax.broadcasted_iota(I32, (nq, 1), 0)
    sink = jnp.where(row < A_BLOCK, sink_ref[kv * A_GRP],
                     jnp.where(row < 2 * A_BLOCK, sink_ref[kv * A_GRP + 1], sink_ref[kv * A_GRP + 2]))
    r = lax.broadcasted_iota(I32, (nq, 3 * A_BLOCK), 0) % A_BLOCK
    c = lax.broadcasted_iota(I32, (nq, 3 * A_BLOCK), 1)
    band = (c >= r) & (c <= r + 2 * A_BLOCK)

    def blk(i, carry):
        w0 = pl.multiple_of(CTX - A_BLOCK + i * A_BLOCK, A_BLOCK)
        q = qs[i]
        s_c = _dot_nt(q, ks[0:CTX, :]) * ATT_SCALE
        s_w = _dot_nt(q, ks[pl.ds(w0, 3 * A_BLOCK), :]) * ATT_SCALE
        kpos = (i - 1) * A_BLOCK + c
        s_w = jnp.where(band & (kpos >= 0) & (kpos < SEQ), s_w, NEG)
        o = _attend([s_c, s_w], [vs[0:CTX, :], vs[pl.ds(w0, 3 * A_BLOCK), :]], sink)
        o0 = pl.multiple_of(CTX + i * A_BLOCK, A_BLOCK)
        for g in range(A_GRP):
            o_ref[pl.ds(o0, A_BLOCK), g * HD:(g + 1) * HD] = o[g * A_BLOCK:(g + 1) * A_BLOCK].astype(BF16)
        return carry

    lax.fori_loop(0, NBLK, blk, 0, unroll=2)

    for g in range(A_GRP):
        s = _dot_nt(q_ref[0:CTX, g * HD:(g + 1) * HD], k_ref[0:CTX, :]) * ATT_SCALE
        o = _attend([s], [v_ref[0:CTX, :]], jnp.full((CTX, 1), sink_ref[kv * A_GRP + g], F32))
        o_ref[0:CTX, g * HD:(g + 1) * HD] = o.astype(BF16)


def window_attn(z, sink_l, cos_t, sin_t):
    return pl.pallas_call(
        _win_kernel,
        grid=(BATCH, A_KV),
        in_specs=[pl.BlockSpec(memory_space=pltpu.SMEM),
                  pl.BlockSpec((SEG, A_GRP * HD), lambda b, kv: (b, kv)),
                  pl.BlockSpec((SEG, HD), lambda b, kv: (b, COL_AK + kv)),
                  pl.BlockSpec((SEG, HD), lambda b, kv: (b, COL_AV + kv)),
                  pl.BlockSpec((SEQ, HD), lambda b, kv: (0, 0)),
                  pl.BlockSpec((SEQ, HD), lambda b, kv: (0, 0))],
        out_specs=pl.BlockSpec((SEG, A_GRP * HD), lambda b, kv: (b, kv)),
        out_shape=jax.ShapeDtypeStruct((T, A_HEADS * HD), BF16),
        scratch_shapes=[pltpu.VMEM((NBLK, A_GRP * A_BLOCK, HD), BF16),
                        pltpu.VMEM((KPAD, HD), BF16),
                        pltpu.VMEM((KPAD, HD), BF16)],
        name="window_attn",
        compiler_params=_cparams(("arbitrary", "arbitrary")),
    )(sink_l, z, z, z, cos_t, sin_t)


GRID_ROWS = SEQ // GRID_W
NBQ_ROWS = 4
NBK_ROWS = 12
NBQ, NBK = NBQ_ROWS * GRID_W, NBK_ROWS * GRID_W
N_NBLK = GRID_ROWS // NBQ_ROWS
NB_BASE_MAX = GRID_ROWS - NBK_ROWS


def _nbr_key_base(first_row):
    return jnp.clip(first_row - NB_ROWS // 2, 0, NB_BASE_MAX)


N_DR, N_DC = 2 * NB_ROWS - 1, 2 * NB_COLS - 1


def _nbr_block_offsets():
    out = []
    for first_row in (0, NBQ_ROWS, GRID_ROWS - NBQ_ROWS):
        base = min(max(first_row - NB_ROWS // 2, 0), NB_BASE_MAX)
        kind = []
        for qi in range(NBQ_ROWS):
            r = first_row + qi
            start = min(max(r - NB_ROWS // 2, 0), GRID_ROWS - NB_ROWS)
            kind.append([kr - r + NB_ROWS - 1 if start <= kr < start + NB_ROWS else N_DR
                         for kr in range(base, base + NBK_ROWS)])
        out.append(kind)
    return out


def _nbr_build_bias(rpb_ref, h, half, bias):
    qc = lax.broadcasted_iota(I32, (GRID_W, 2 * GRID_W), 0)
    lane = lax.broadcasted_iota(I32, (GRID_W, 2 * GRID_W), 1)
    kc = lane % GRID_W
    start_c = jnp.clip(qc - NB_COLS // 2, 0, GRID_W - NB_COLS)
    col_ok = (kc >= start_c) & (kc < start_c + NB_COLS)
    dc = kc - qc + NB_COLS - 1
    left = lane < GRID_W
    for d in range(N_DR):
        t = jnp.full((GRID_W, 2 * GRID_W), NEG, F32)
        for j in range(N_DC):
            t = jnp.where(dc == j, rpb_ref[(h * N_DR + d) * N_DC + j], t)
        t = jnp.where(col_ok, t, NEG)
        half[d, 0] = jnp.where(left, t, NEG)
        half[d, 1] = jnp.where(left, NEG, t)
    masked = jnp.full((GRID_W, 2 * GRID_W), NEG, F32)
    for kind, per_q in enumerate(_nbr_block_offsets()):
        for qi, dr in enumerate(per_q):
            for p in range(NBK_ROWS // 2):
                lo = half[dr[2 * p], 0] if dr[2 * p] < N_DR else masked
                hi = half[dr[2 * p + 1], 1] if dr[2 * p + 1] < N_DR else masked
                bias[kind, qi * GRID_W:(qi + 1) * GRID_W, p * 2 * GRID_W:(p + 1) * 2 * GRID_W] = jnp.maximum(lo, hi)


def _nbr_kernel(rpb_ref, q_ref, k_ref, v_ref, o_ref, half, bias):
    @pl.when(pl.program_id(1) == 0)
    def _():
        _nbr_build_bias(rpb_ref, pl.program_id(0), half, bias)

    def block(i):
        kind = jnp.where(i == 0, 0, jnp.where(i == N_NBLK - 1, 2, 1))
        q0 = pl.multiple_of(CTX + i * NBQ, NBQ)
        k0 = pl.multiple_of(CTX + _nbr_key_base(i * NBQ_ROWS) * GRID_W, GRID_W)
        q = q_ref[pl.ds(q0, NBQ), :]
        s_c = _dot_nt(q, k_ref[0:CTX, :]) * ATT_SCALE
        s_n = _dot_nt(q, k_ref[pl.ds(k0, NBK), :]) * ATT_SCALE + bias[kind]
        o = _attend([s_c, s_n], [v_ref[0:CTX, :], v_ref[pl.ds(k0, NBK), :]])
        o_ref[pl.ds(q0, NBQ), :] = o.astype(BF16)

    def blocks(i, carry):
        block(2 * i)
        block(2 * i + 1)
        return carry

    lax.fori_loop(0, N_NBLK // 2, blocks, 0)

    s = _dot_nt(q_ref[0:CTX, :], k_ref[0:CTX, :]) * ATT_SCALE
    o_ref[0:CTX, :] = _attend([s], [v_ref[0:CTX, :]]).astype(BF16)


def nbr_attn(z, rpb_l):
    return pl.pallas_call(
        _nbr_kernel,
        grid=(C_HEADS, BATCH),
        in_specs=[pl.BlockSpec(memory_space=pltpu.SMEM),
                  pl.BlockSpec((SEG, HD), lambda h, b: (b, COL_CQ + h)),
                  pl.BlockSpec((SEG, HD), lambda h, b: (b, COL_CK + h)),
                  pl.BlockSpec((SEG, HD), lambda h, b: (b, COL_CV + h))],
        out_specs=pl.BlockSpec((SEG, HD), lambda h, b: (b, h)),
        out_shape=jax.ShapeDtypeStruct((T, C_HEADS * HD), BF16),
        scratch_shapes=[pltpu.VMEM((N_DR, 2, GRID_W, 2 * GRID_W), F32), pltpu.VMEM((3, NBQ, NBK), F32)],
        name="nbr_attn",
        compiler_params=_cparams(("arbitrary", "arbitrary")),
    )(rpb_l.reshape(-1), z, z, z)


N_LCHUNK = SEQ // CHUNK
PADR = 8
CONV_ROWS = PADR + CTX + PADR + SEQ + PADR


def _conv_off(r0):
    return jnp.where(r0 < CTX, r0 + PADR, r0 + 2 * PADR)


def _mlstm_kernel(bg_ref, q_ref, k_ref, v_ref, og_ref, gr_ref, cwq_ref, cwk_ref, y_ref,
                  qs, ks, stage, hf, hb):
    h = pl.program_id(1)
    rowi = lax.broadcasted_iota(I32, (CHUNK, 1), 0)

    def conv_silu(src_ref, cw_ref, dst, post):
        stage[...] = jnp.zeros((CONV_ROWS, HD), F32)
        stage[PADR:PADR + CTX, :] = src_ref[0:CTX, :].astype(F32)
        stage[2 * PADR + CTX:2 * PADR + SEG, :] = src_ref[CTX:SEG, :].astype(F32)

        def blk(c, carry):
            r0 = pl.multiple_of(c * CHUNK, CHUNK)
            s0 = pl.multiple_of(_conv_off(r0), PADR)
            cur = stage[pl.ds(s0, CHUNK), :]
            prev_last = stage[pl.ds(s0 - PADR, PADR), :][PADR - 1:PADR, :]
            next_first = stage[pl.ds(s0 + CHUNK, PADR), :][0:1, :]
            zm = jnp.where(rowi == 0, prev_last, pltpu.roll(cur, 1, 0))
            zp = jnp.where(rowi == CHUNK - 1, next_first, pltpu.roll(cur, CHUNK - 1, 0))
            y = zm * cw_ref[0:1, :] + cur * cw_ref[1:2, :] + zp * cw_ref[2:3, :]
            dst[pl.ds(r0, CHUNK), :] = (_silu(y) * post).astype(BF16)
            return carry

        lax.fori_loop(0, SEG // CHUNK, blk, 0)

    conv_silu(q_ref, cwq_ref, qs, 1.0)
    conv_silu(k_ref, cwk_ref, ks, ATT_SCALE)

    t_idx = lax.broadcasted_iota(I32, (CHUNK, CHUNK), 0)
    s_idx = lax.broadcasted_iota(I32, (CHUNK, CHUNK), 1)

    def chunk(r0, bwd, state, dst):
        c_mat, n_vec, m_prev = state
        ki, kf = (2, 3) if bwd else (0, 1)
        bi = bg_ref[ki * B_HEADS + h]
        bf = bg_ref[kf * B_HEADS + h]
        qc = qs[pl.ds(r0, CHUNK), :]
        kc = ks[pl.ds(r0, CHUNK), :]
        vc = v_ref[pl.ds(r0, CHUNK), :]
        i_row = gr_ref[pl.ds(ki * B_HEADS + h, 1), pl.ds(r0, CHUNK)] + bi
        lf_row = _log_sigmoid(gr_ref[pl.ds(kf * B_HEADS + h, 1), pl.ds(r0, CHUNK)] + bf)
        causal = (s_idx >= t_idx) if bwd else (s_idx <= t_idx)
        diag = s_idx == t_idx
        b_col = jnp.sum(jnp.where(causal, lf_row, 0.0), axis=1, keepdims=True)
        b_row = jnp.sum(jnp.where(diag, b_col, 0.0), axis=0, keepdims=True)
        a_row = i_row - b_row
        a_col = jnp.sum(jnp.where(diag, a_row, 0.0), axis=1, keepdims=True)
        mx = jnp.maximum(m_prev, jnp.max(jnp.where(causal, a_row, NEG), axis=1, keepdims=True))
        dm = jnp.exp(jnp.where(causal, a_row - mx, NEG))
        s = _dot_nt(qc, kc) * dm
        w_int = jnp.exp(m_prev - mx)
        num = (jnp.dot(s.astype(BF16), vc, preferred_element_type=F32)
               + w_int * jnp.dot(qc, c_mat.astype(BF16), preferred_element_type=F32))
        den = (jnp.sum(s, axis=1, keepdims=True)
               + w_int * jnp.sum(qc.astype(F32) * n_vec, axis=1, keepdims=True))
        m_t = b_col + mx
        dst[pl.ds(r0, CHUNK), :] = num / jnp.maximum(jnp.abs(den), jnp.exp(-m_t))
        b_end = jnp.sum(lf_row, axis=1, keepdims=True)
        m_end = jnp.maximum(m_prev, jnp.max(a_row, axis=1, keepdims=True))
        decay = jnp.exp(m_prev - m_end)
        w_col = jnp.exp(a_col - m_end)
        kv = lax.dot_general(kc, (w_col * vc.astype(F32)).astype(BF16), (((0,), (0,)), ((), ())),
                             preferred_element_type=F32)
        c_new = decay * c_mat + kv
        n_new = decay * n_vec + jnp.sum(w_col * kc.astype(F32), axis=0, keepdims=True)
        return c_new, n_new, b_end + m_end

    zero = (jnp.zeros((HD, HD), F32), jnp.zeros((1, HD), F32), jnp.zeros((1, 1), F32))
    st_f = chunk(0, False, zero, hf)
    st_b = chunk(0, True, zero, hb)

    def body(j, carry):
        sf, sb = carry
        rf = pl.multiple_of(CTX + j * CHUNK, CHUNK)
        rb = pl.multiple_of(CTX + (N_LCHUNK - 1 - j) * CHUNK, CHUNK)
        return chunk(rf, False, sf, hf), chunk(rb, True, sb, hb)

    lax.fori_loop(0, N_LCHUNK, body, (st_f, st_b))

    def fin(c, carry):
        r0 = pl.multiple_of(c * CHUNK, CHUNK)
        gate = jax.nn.sigmoid(og_ref[pl.ds(r0, CHUNK), :].astype(F32))
        y_ref[pl.ds(r0, CHUNK), :] = (gate * (hf[pl.ds(r0, CHUNK), :] + hb[pl.ds(r0, CHUNK), :])).astype(BF16)
        return carry

    lax.fori_loop(0, SEG // CHUNK, fin, 0)


def mlstm(z, g_row, b_gates_l, conv_l):
    return pl.pallas_call(
        _mlstm_kernel,
        grid=(BATCH, B_HEADS),
        in_specs=[pl.BlockSpec(memory_space=pltpu.SMEM),
                  pl.BlockSpec((SEG, HD), lambda b, h: (b, COL_BQ + h)),
                  pl.BlockSpec((SEG, HD), lambda b, h: (b, COL_BK + h)),
                  pl.BlockSpec((SEG, HD), lambda b, h: (b, COL_BV + h)),
                  pl.BlockSpec((SEG, HD), lambda b, h: (b, COL_BO + h)),
                  pl.BlockSpec((4 * B_HEADS, SEG), lambda b, h: (0, b)),
                  pl.BlockSpec((3, HD), lambda b, h: (0, h)),
                  pl.BlockSpec((3, HD), lambda b, h: (0, B_HEADS + h))],
        out_specs=pl.BlockSpec((SEG, HD), lambda b, h: (b, h)),
        out_shape=jax.ShapeDtypeStruct((T, B_HEADS * HD), BF16),
        scratch_shapes=[pltpu.VMEM((SEG, HD), BF16), pltpu.VMEM((SEG, HD), BF16),
                        pltpu.VMEM((CONV_ROWS, HD), F32),
                        pltpu.VMEM((SEG, HD), F32), pltpu.VMEM((SEG, HD), F32)],
        name="mlstm",
        compiler_params=_cparams(("arbitrary", "arbitrary")),
    )(b_gates_l, z, z, z, z, g_row, conv_l, conv_l)


OUT_TM = SEG // 6


def _top2_routing(biased, scores, comb_ref, sel_ref):
    rows = [biased[e:e + 1, :] for e in range(N_EXPERTS)]
    gscore = []
    for g in range(N_GROUPS):
        v0, v1, v2, v3 = rows[4 * g:4 * g + 4]
        hi01, lo01 = jnp.maximum(v0, v1), jnp.minimum(v0, v1)
        hi23, lo23 = jnp.maximum(v2, v3), jnp.minimum(v2, v3)
        gscore.append(jnp.maximum(hi01, hi23) + jnp.maximum(jnp.minimum(hi01, hi23), jnp.maximum(lo01, lo23)))
    picked = []
    total = None
    for e in range(N_EXPERTS):
        g = e // PER_GROUP
        ok = None
        for o in range(N_GROUPS):
            if o != g:
                c = (gscore[g] > gscore[o]) if o < g else (gscore[g] >= gscore[o])
                ok = c if ok is None else (ok & c)
        ahead = jnp.zeros_like(rows[e])
        for o in range(4 * g, 4 * g + 4):
            if o != e:
                c = (rows[o] > rows[e]) if o > e else (rows[o] >= rows[e])
                ahead = ahead + jnp.where(c, 1.0, 0.0)
        sel = jnp.where(ok & (ahead < 2.0), 1.0, 0.0)
        sel_ref[e:e + 1, :] = sel
        picked.append(sel * scores[e:e + 1, :])
        total = picked[-1] if total is None else total + picked[-1]
    for e in range(N_EXPERTS):
        comb_ref[e:e + 1, :] = picked[e] / total


OUT_RC = 16


N_OUT_TILES = T // OUT_TM


def _out_kernel(ya_ref, yb_ref, yn_ref, w_ref, x_ref, mod_ref, g_ref, wrh_ref, wrl_ref, br_ref,
                x1_ref, h2_ref, comb_ref, sel_ref, ycat, acc, acc_next, h_hi, h_lo):
    i = pl.program_id(0)

    def matmul():
        ycat[:, 0:A_HEADS * HD] = ya_ref[...]
        ycat[:, A_HEADS * HD:(A_HEADS + B_HEADS) * HD] = yb_ref[...]
        ycat[:, (A_HEADS + B_HEADS) * HD:D] = yn_ref[...]
        acc_next[...] = jnp.dot(ycat[...], w_ref[...], preferred_element_type=F32)

    def epilogue():
        t = i - 1
        batch = t // 6
        for c in range(OUT_TM // OUT_RC):
            rows = slice(c * OUT_RC, (c + 1) * OUT_RC)
            mrow = jnp.where((t % 6) * OUT_TM + c * OUT_RC < CTX, 4, batch)

            def mod(k):
                return mod_ref[pl.ds(mrow, 1), k * D:(k + 1) * D]

            x1 = x_ref[rows, :] + mod(2) * acc[rows, :]
            x1_ref[rows, :] = x1
            h2 = (_rms(x1) * g_ref[...]) * (1.0 + mod(4)) + mod(3)
            h2_ref[rows, :] = h2
            hi = h2.astype(BF16)
            h_hi[rows, :] = hi
            h_lo[rows, :] = (h2 - hi.astype(F32)).astype(BF16)
        scores = jax.nn.sigmoid(_dot_nt(wrh_ref[...], h_hi[...]) + _dot_nt(wrl_ref[...], h_hi[...])
                                + _dot_nt(wrh_ref[...], h_lo[...]))
        _top2_routing(scores + br_ref[...], scores, comb_ref, sel_ref)

    @pl.when(i == 0)
    def _():
        matmul()

    @pl.when((i > 0) & (i < N_OUT_TILES))
    def _():
        matmul()
        epilogue()

    @pl.when(i == N_OUT_TILES)
    def _():
        epilogue()

    @pl.when(i < N_OUT_TILES)
    def _():
        acc[...] = acc_next[...]


def out_proj(ya, yb, yn, w_out_b, xa, mod_l, g, w_router_hi, w_router_lo, b_router_c, layer):
    cur = lambda i: (jnp.minimum(i, N_OUT_TILES - 1), 0)
    row = lambda i: (jnp.maximum(i - 1, 0), 0)
    fixed = lambda i: (0, 0)
    return pl.pallas_call(
        _out_kernel,
        grid=(N_OUT_TILES + 1,),
        in_specs=[pl.BlockSpec((OUT_TM, A_HEADS * HD), cur),
                  pl.BlockSpec((OUT_TM, B_HEADS * HD), cur),
                  pl.BlockSpec((OUT_TM, C_HEADS * HD), cur),
                  pl.BlockSpec((None, D, D), lambda i: (layer, 0, 0)),
                  pl.BlockSpec((OUT_TM, D), row),
                  pl.BlockSpec((8, 6 * D), fixed),
                  pl.BlockSpec((1, D), fixed),
                  pl.BlockSpec((N_EXPERTS, D), fixed),
                  pl.BlockSpec((N_EXPERTS, D), fixed),
                  pl.BlockSpec((N_EXPERTS, 1), fixed)],
        out_specs=[pl.BlockSpec((OUT_TM, D), row),
                   pl.BlockSpec((OUT_TM, D), row),
                   pl.BlockSpec((N_EXPERTS, OUT_TM), lambda i: (0, jnp.maximum(i - 1, 0))),
                   pl.BlockSpec((N_EXPERTS, OUT_TM), lambda i: (0, jnp.maximum(i - 1, 0)))],
        out_shape=[jax.ShapeDtypeStruct((T, D), F32),
                   jax.ShapeDtypeStruct((T, D), F32),
                   jax.ShapeDtypeStruct((N_EXPERTS, T), F32),
                   jax.ShapeDtypeStruct((N_EXPERTS, T), F32)],
        scratch_shapes=[pltpu.VMEM((OUT_TM, D), BF16), pltpu.VMEM((OUT_TM, D), F32), pltpu.VMEM((OUT_TM, D), F32),
                        pltpu.VMEM((OUT_TM, D), BF16), pltpu.VMEM((OUT_TM, D), BF16)],
        name="out_proj",
        compiler_params=_cparams(("arbitrary",)),
    )(ya, yb, yn, w_out_b, xa, mod_l, g, w_router_hi, w_router_lo, b_router_c)


SC_TM = 1024
ROW_DMA_UNROLL = 8


def _scatter_kernel(pos0_ref, pos1_ref, ztile_ref, h_ref, hs_ref, zero_scr, sem, zsem):
    i = pl.program_id(0)

    @pl.when(i == 0)
    def _():
        zero_scr[...] = jnp.zeros((ETILE, D), F32)

        def fill_copy(t):
            return pltpu.make_async_copy(zero_scr, hs_ref.at[pl.ds(pl.multiple_of(t * ETILE, ETILE), ETILE), :], zsem)

        def fill_start(e, c):
            @pl.when(ztile_ref[e] >= 0)
            def _():
                fill_copy(ztile_ref[e]).start()
            return c

        def fill_wait(e, c):
            @pl.when(ztile_ref[e] >= 0)
            def _():
                fill_copy(ztile_ref[e]).wait()
            return c

        lax.fori_loop(0, 2 * N_EXPERTS, fill_start, 0)
        lax.fori_loop(0, 2 * N_EXPERTS, fill_wait, 0)

    def row_copy(r, p):
        return pltpu.make_async_copy(h_ref.at[pl.ds(r, 1), :], hs_ref.at[pl.ds(p, 1), :], sem)

    def issue(g, c):
        r0 = pl.multiple_of(g * ROW_DMA_UNROLL, ROW_DMA_UNROLL)
        for u in range(ROW_DMA_UNROLL):
            t = i * SC_TM + r0 + u
            row_copy(r0 + u, pos0_ref[t]).start()
            row_copy(r0 + u, pos1_ref[t]).start()
        return c

    lax.fori_loop(0, SC_TM // ROW_DMA_UNROLL, issue, 0)

    def drain(r, c):
        row_copy(0, 0).wait()
        row_copy(0, 0).wait()
        return c

    lax.fori_loop(0, SC_TM, drain, 0, unroll=ROW_DMA_UNROLL)


def scatter_rows(pos0, pos1, ztile, h2):
    return pl.pallas_call(
        _scatter_kernel,
        grid_spec=pltpu.PrefetchScalarGridSpec(
            num_scalar_prefetch=3,
            grid=(T // SC_TM,),
            in_specs=[pl.BlockSpec((SC_TM, D), lambda i, *_: (i, 0))],
            out_specs=pl.BlockSpec(memory_space=pl.ANY),
            scratch_shapes=[pltpu.VMEM((ETILE, D), F32), pltpu.SemaphoreType.DMA, pltpu.SemaphoreType.DMA]),
        out_shape=jax.ShapeDtypeStruct((P_ROWS, D), F32),
        name="scatter_rows",
        compiler_params=_cparams(("arbitrary",)),
    )(pos0, pos1, ztile, h2)


CAST_ROWS = 256


def _cast_into(dst, src, n_rows):
    def body(c, carry):
        r0 = pl.multiple_of(c * CAST_ROWS, CAST_ROWS)
        dst[pl.ds(r0, CAST_ROWS), :] = src[pl.ds(r0, CAST_ROWS), :].astype(BF16)
        return carry

    lax.fori_loop(0, n_rows // CAST_ROWS, body, 0)


def _ffn_kernel(te_ref, tv_ref, first_ref, nxt_ref, hs_ref, w1_hbm, w3_hbm, w2_hbm, ys_ref,
                st1, st3, st2, w1b, w3b, w2b, sem, *, layer):
    j = pl.program_id(0)

    def fetch(e):
        return (pltpu.make_async_copy(w1_hbm.at[layer, e], st1, sem.at[0]),
                pltpu.make_async_copy(w3_hbm.at[layer, e], st3, sem.at[1]),
                pltpu.make_async_copy(w2_hbm.at[layer, e], st2, sem.at[2]))

    @pl.when(j == 0)
    def _():
        for cp in fetch(te_ref[0]):
            cp.start()

    @pl.when(first_ref[j] == 1)
    def _():
        for cp in fetch(te_ref[j]):
            cp.wait()
        _cast_into(w1b, st1, D)
        _cast_into(w3b, st3, D)
        _cast_into(w2b, st2, FF)

        @pl.when(nxt_ref[j] >= 0)
        def _():
            for cp in fetch(nxt_ref[j]):
                cp.start()

    def expert_rows(n):
        xb = hs_ref[0:n, :].astype(BF16)
        a = jnp.dot(xb, w1b[...], preferred_element_type=F32)
        b = jnp.dot(xb, w3b[...], preferred_element_type=F32)
        act = (_silu(a) * b).astype(BF16)
        ys_ref[0:n, :] = jnp.dot(act, w2b[...], preferred_element_type=F32)

    @pl.when(tv_ref[j] > ETILE // 2)
    def _():
        expert_rows(ETILE)

    @pl.when((tv_ref[j] > 0) & (tv_ref[j] <= ETILE // 2))
    def _():
        expert_rows(ETILE // 2)
        ys_ref[ETILE // 2:ETILE, :] = jnp.zeros((ETILE // 2, D), F32)

    @pl.when(tv_ref[j] == 0)
    def _():
        ys_ref[...] = jnp.zeros((ETILE, D), F32)


def _tile_or_first(j, te, tv, *_):
    return jnp.where(tv[j] > 0, j, 0)


def ffn(plan, hs, w1, w3, w2, layer):
    return pl.pallas_call(
        functools.partial(_ffn_kernel, layer=layer),
        grid_spec=pltpu.PrefetchScalarGridSpec(
            num_scalar_prefetch=4,
            grid=(N_ETILES,),
            in_specs=[pl.BlockSpec((ETILE, D), lambda j, *p: (_tile_or_first(j, *p), 0)),
                      pl.BlockSpec(memory_space=pl.ANY),
                      pl.BlockSpec(memory_space=pl.ANY),
                      pl.BlockSpec(memory_space=pl.ANY)],
            out_specs=pl.BlockSpec((ETILE, D), lambda j, *p: (j, 0)),
            scratch_shapes=[pltpu.VMEM((D, FF), F32), pltpu.VMEM((D, FF), F32), pltpu.VMEM((FF, D), F32),
                            pltpu.VMEM((D, FF), BF16), pltpu.VMEM((D, FF), BF16), pltpu.VMEM((FF, D), BF16),
                            pltpu.SemaphoreType.DMA((3,))]),
        out_shape=jax.ShapeDtypeStruct((P_ROWS, D), F32),
        name="ffn",
        compiler_params=_cparams(("arbitrary",)),
    )(*plan, hs, w1, w3, w2)


CB_TM = SEG // 3
CB_PER_SEG = SEG // CB_TM


CB_RC = 32


def _combine_kernel(pos0_ref, pos1_ref, ys_ref, x_ref, w_ref, mod_ref, o_ref, buf, sem):
    i = pl.program_id(0)
    n = pl.num_programs(0)
    slot = i % 2

    def row_copy(s, k, r, p):
        return pltpu.make_async_copy(ys_ref.at[pl.ds(p, 1), :], buf.at[s, k, pl.ds(r, 1), :], sem.at[s])

    def gather(tile, s):
        def issue(g, c):
            r0 = pl.multiple_of(g * ROW_DMA_UNROLL, ROW_DMA_UNROLL)
            for u in range(ROW_DMA_UNROLL):
                t = tile * CB_TM + r0 + u
                row_copy(s, 0, r0 + u, pos0_ref[t]).start()
                row_copy(s, 1, r0 + u, pos1_ref[t]).start()
            return c

        lax.fori_loop(0, CB_TM // ROW_DMA_UNROLL, issue, 0)

    @pl.when(i == 0)
    def _():
        gather(0, 0)

    @pl.when(i + 1 < n)
    def _():
        gather(i + 1, 1 - slot)

    def drain(r, c):
        row_copy(slot, 0, 0, 0).wait()
        row_copy(slot, 1, 0, 0).wait()
        return c

    lax.fori_loop(0, CB_TM, drain, 0, unroll=ROW_DMA_UNROLL)
    def chunk(c, carry):
        r0 = pl.multiple_of(c * CB_RC, CB_RC)
        rows = pl.ds(r0, CB_RC)
        is_ctx = ((i % CB_PER_SEG) == 0) & (r0 < CTX)
        gate = mod_ref[pl.ds(jnp.where(is_ctx, 4, i // CB_PER_SEG), 1), 5 * D:6 * D]
        f = w_ref[rows, 0:1] * buf[slot, 0, rows, :] + w_ref[rows, 1:2] * buf[slot, 1, rows, :]
        o_ref[rows, :] = x_ref[rows, :] + gate * f
        return carry

    lax.fori_loop(0, CB_TM // CB_RC, chunk, 0)


def combine_rows(pos0, pos1, ys, x1, w01, mod_l):
    return pl.pallas_call(
        _combine_kernel,
        grid_spec=pltpu.PrefetchScalarGridSpec(
            num_scalar_prefetch=2,
            grid=(T // CB_TM,),
            in_specs=[pl.BlockSpec(memory_space=pl.ANY),
                      pl.BlockSpec((CB_TM, D), lambda i, *_: (i, 0)),
                      pl.BlockSpec((CB_TM, 2), lambda i, *_: (i, 0)),
                      pl.BlockSpec((8, 6 * D), lambda i, *_: (0, 0))],
            out_specs=pl.BlockSpec((CB_TM, D), lambda i, *_: (i, 0)),
            scratch_shapes=[pltpu.VMEM((2, 2, CB_TM, D), F32), pltpu.SemaphoreType.DMA((2,))]),
        out_shape=jax.ShapeDtypeStruct((T, D), F32),
        name="combine_rows",
        compiler_params=_cparams(("arbitrary",)),
    )(pos0, pos1, ys, x1, w01, mod_l)


def route_plan(comb_t, sel_t):
    sel = sel_t > 0.5
    cnt = jnp.sum(sel, axis=1).astype(I32)
    tiles = (cnt + ETILE - 1) // ETILE
    tend = jnp.cumsum(tiles)
    toff = tend - tiles
    rank = jnp.cumsum(sel.astype(I32), axis=1) - 1
    pos = toff[:, None] * ETILE + rank
    pos0 = jnp.min(jnp.where(sel, pos, P_ROWS), axis=0).astype(I32)
    pos1 = jnp.max(jnp.where(sel, pos, -1), axis=0).astype(I32)
    w0 = jnp.sum(jnp.where(sel & (pos == pos0[None]), comb_t, 0.0), axis=0)
    w1 = jnp.sum(jnp.where(sel & (pos == pos1[None]), comb_t, 0.0), axis=0)
    n_used = tend[-1]
    tidx = jnp.arange(N_ETILES, dtype=I32)
    te_raw = jnp.sum((tend[None, :] <= tidx[:, None]).astype(I32), axis=1)
    te_last = jnp.sum((tend <= n_used - 1).astype(I32))
    te = jnp.where(tidx < n_used, te_raw, te_last).astype(I32)
    te = jnp.minimum(te, N_EXPERTS - 1)
    tv = jnp.clip(cnt[te] - (tidx - toff[te]) * ETILE, 0, ETILE)
    tv = jnp.where(tidx < n_used, tv, 0).astype(I32)
    tail = n_used + jnp.arange(N_EXPERTS, dtype=I32)
    ztile = jnp.concatenate([jnp.where(tiles > 0, tend - 1, -1), jnp.where(tail < N_ETILES, tail, -1)]).astype(I32)
    used = tidx < n_used
    first = (used & ((tidx == 0) | (te != jnp.roll(te, 1)))).astype(I32)
    nxt_tile = tend[te]
    nxt = jnp.where(nxt_tile < n_used, te[jnp.minimum(nxt_tile, N_ETILES - 1)], -1).astype(I32)
    return pos0, pos1, jnp.stack([w0, w1], axis=1), (te, tv, first, nxt), ztile


def _final_kernel(x_ref, g_ref, o_ref):
    o_ref[...] = _rms(x_ref[...]) * g_ref[...]


def final_norm(xa, g):
    nb = SEQ // CTX
    return pl.pallas_call(
        _final_kernel,
        grid=(BATCH, nb),
        in_specs=[pl.BlockSpec((CTX, D), lambda b, j: (b * (nb + 1) + 1 + j, 0)),
                  pl.BlockSpec((1, D), lambda b, j: (0, 0))],
        out_specs=pl.BlockSpec((CTX, D), lambda b, j: (b * nb + j, 0)),
        out_shape=jax.ShapeDtypeStruct((BATCH * SEQ, D), F32),
        name="final_norm",
        compiler_params=_cparams(("arbitrary", "arbitrary")),
    )(xa, g)


def _rope_tables():
    t = jnp.arange(SEQ, dtype=I32)
    row = (t // GRID_W).astype(F32)
    col = (t % GRID_W).astype(F32)
    n_freq = HD // 4
    inv_freq = ROPE_BASE ** (-jnp.arange(n_freq, dtype=F32) / n_freq)
    ar = row[:, None] * inv_freq[None, :]
    ac = col[:, None] * inv_freq[None, :]
    cos_t = jnp.concatenate([jnp.cos(ar), jnp.cos(ar), jnp.cos(ac), jnp.cos(ac)], axis=1)
    sin_t = jnp.concatenate([-jnp.sin(ar), jnp.sin(ar), -jnp.sin(ac), jnp.sin(ac)], axis=1)
    return cos_t, sin_t


GATE_LO, GATE_HI = 3328, 3344
D_IN = N_MAIN + 4 * B_HEADS
D_IN_PAD = N_MAIN + HD
RP_ROWS = 256


def _repack_kernel(w_ref, o_ref, g_ref):
    o_ref[0, :, 0:GATE_LO] = w_ref[0, :, 0:GATE_LO].astype(BF16)
    o_ref[0, :, GATE_LO:N_MAIN] = w_ref[0, :, GATE_HI:D_IN].astype(BF16)
    lane = lax.broadcasted_iota(I32, (RP_ROWS, HD), 1)
    g_ref[0] = jnp.where(lane < 4 * B_HEADS, w_ref[0, :, GATE_LO:GATE_LO + HD], 0.0).astype(BF16)


def repack_w_in(w_in):
    blk = lambda l, r: (l, r, 0)
    return pl.pallas_call(
        _repack_kernel,
        grid=(DEPTH, D // RP_ROWS),
        in_specs=[pl.BlockSpec((1, RP_ROWS, D_IN_PAD), blk)],
        out_specs=[pl.BlockSpec((1, RP_ROWS, N_MAIN), blk), pl.BlockSpec((1, RP_ROWS, HD), blk)],
        out_shape=[jax.ShapeDtypeStruct((DEPTH, D, N_MAIN), BF16), jax.ShapeDtypeStruct((DEPTH, D, HD), BF16)],
        name="repack_w_in",
        compiler_params=_cparams(("arbitrary", "arbitrary")),
    )(w_in)


def kernel(x, c, ctx, c_ctx, w_ada, b_ada, norm_mix, norm_ffn, w_in, b_gates, conv_qk, sink, rpb, w_out,
           w_router, b_router, w1, w3, w2, norm_final):
    xa = jnp.concatenate([ctx, x], axis=1).reshape(T, D)
    cond8 = jnp.concatenate([c, c_ctx[None], jnp.zeros((3, D), F32)], axis=0)
    mod = ada_all(cond8, w_ada, b_ada)
    cos_t, sin_t = _rope_tables()
    w_main, w_gate = repack_w_in(jnp.concatenate(
        [w_in.astype(BF16), jnp.zeros((DEPTH, D, D_IN_PAD - D_IN), BF16)], axis=2))
    w_out_b = w_out.astype(BF16)
    w_router_hi = w_router.T.astype(BF16)
    w_router_lo = (w_router.T - w_router_hi.astype(F32)).astype(BF16)
    b_router_c = b_router.reshape(N_EXPERTS, 1)

    for l in range(DEPTH):
        z, zg = in_proj(xa, norm_mix[l].reshape(1, D), mod[l], w_main, w_gate, l)
        g_row = zg[:, :4 * B_HEADS].T
        ya = window_attn(z, sink[l], cos_t, sin_t)
        yb = mlstm(z, g_row, b_gates[l], conv_qk[l])
        yn = nbr_attn(z, rpb[l])
        x1, h2, comb_t, sel_t = out_proj(ya, yb, yn, w_out_b, xa, mod[l], norm_ffn[l].reshape(1, D),
                                         w_router_hi, w_router_lo, b_router_c, l)
        pos0, pos1, w01, plan, ztile = route_plan(comb_t, sel_t)
        hs = scatter_rows(pos0, pos1, ztile, h2)
        ys = ffn(plan, hs, w1, w3, w2, l)
        xa = combine_rows(pos0, pos1, ys, x1, w01, mod[l])
    return final_norm(xa, norm_final.reshape(1, D)).reshape(BATCH, SEQ, D)
```

```python
import functools

import jax
import jax.numpy as jnp
from jax import lax
from jax.experimental import pallas as pl
from jax.experimental.pallas import tpu as pltpu

F32 = jnp.float32
BF16 = jnp.bfloat16
I32 = jnp.int32

D = 2048
BATCH = 4
SEQ = 2048
CTX = 256
SEG = CTX + SEQ
T = BATCH * SEG
DEPTH = 4
GRID_W = 64
HD = 128
A_HEADS, A_KV, A_GRP = 6, 2, 3
B_HEADS = 4
C_HEADS = 6
A_BLOCK = 128
NB_ROWS, NB_COLS = 8, 16
CHUNK = 256
N_EXPERTS, N_GROUPS, PER_GROUP = 16, 4, 4
FF = 1024
EPS = 1e-6
ROPE_BASE = 10000.0
NEG = -1e30
ATT_SCALE = HD ** -0.5

N_MAIN = 5632
COL_AQ, COL_AK, COL_AV = 0, 6, 8
COL_BQ, COL_BK, COL_BV, COL_BO = 10, 14, 18, 22
COL_CQ, COL_CK, COL_CV = 26, 32, 38

ETILE = 256
N_ETILES = (2 * T) // ETILE + N_EXPERTS
P_ROWS = N_ETILES * ETILE

VMEM_LIMIT = 56 * 1024 * 1024


def _cparams(sem):
    return pltpu.CompilerParams(dimension_semantics=sem, vmem_limit_bytes=VMEM_LIMIT)


def _silu(v):
    return v * jax.nn.sigmoid(v)


def _log_sigmoid(v):
    return jnp.minimum(v, 0.0) - jnp.log1p(jnp.exp(-jnp.abs(v)))


ADA_TN = 1024


def _ada_kernel(s_ref, w_ref, b_ref, o_ref):
    s = _silu(s_ref[...]).astype(BF16)
    o_ref[0] = jnp.dot(s, w_ref[0].astype(BF16), preferred_element_type=F32) + b_ref[0]


def ada_all(cond8, w_ada, b_ada):
    n = w_ada.shape[-1]
    return pl.pallas_call(
        _ada_kernel,
        grid=(DEPTH, n // ADA_TN),
        in_specs=[pl.BlockSpec((8, D), lambda l, j: (0, 0)),
                  pl.BlockSpec((1, D, ADA_TN), lambda l, j: (l, 0, j)),
                  pl.BlockSpec((1, 1, ADA_TN), lambda l, j: (l, 0, j))],
        out_specs=pl.BlockSpec((1, 8, ADA_TN), lambda l, j: (l, 0, j)),
        out_shape=jax.ShapeDtypeStruct((DEPTH, 8, n), F32),
        name="ada_mod",
        compiler_params=_cparams(("arbitrary", "arbitrary")),
    )(cond8, w_ada, b_ada.reshape(DEPTH, 1, n))


def _row_mod(mod_ref, chunk, batch, is_ctx):
    lat = mod_ref[pl.ds(batch, 1), chunk * D:(chunk + 1) * D]
    ctx = mod_ref[4:5, chunk * D:(chunk + 1) * D]
    return jnp.where(is_ctx, ctx, lat)


def _rms(x):
    return x * lax.rsqrt(jnp.mean(x * x, axis=-1, keepdims=True) + EPS)


IN_TM = SEG // 2
IN_TN = 1408


def _in_kernel(x_ref, g_ref, mod_ref, w_ref, wg_ref, z_ref, zg_ref, h_scr):
    i = pl.program_id(0)
    j = pl.program_id(1)

    @pl.when(j == 0)
    def _():
        batch = i // 2
        rows = lax.broadcasted_iota(I32, (IN_TM, 1), 0) + (i % 2) * IN_TM
        is_ctx = rows < CTX
        xn = _rms(x_ref[...]) * g_ref[...]
        h = xn * (1.0 + _row_mod(mod_ref, 1, batch, is_ctx)) + _row_mod(mod_ref, 0, batch, is_ctx)
        hb = h.astype(BF16)
        h_scr[...] = hb
        zg_ref[...] = jnp.dot(hb, wg_ref[...], preferred_element_type=F32)

    z_ref[...] = jnp.dot(h_scr[...], w_ref[...], preferred_element_type=F32).astype(BF16)


def in_proj(xa, g, mod_l, w_main, w_gate, layer):
    return pl.pallas_call(
        _in_kernel,
        grid=(T // IN_TM, N_MAIN // IN_TN),
        in_specs=[pl.BlockSpec((IN_TM, D), lambda i, j: (i, 0)),
                  pl.BlockSpec((1, D), lambda i, j: (0, 0)),
                  pl.BlockSpec((8, 6 * D), lambda i, j: (0, 0)),
                  pl.BlockSpec((None, D, IN_TN), lambda i, j: (layer, 0, j)),
                  pl.BlockSpec((None, D, HD), lambda i, j: (layer, 0, 0))],
        out_specs=[pl.BlockSpec((IN_TM, IN_TN), lambda i, j: (i, j)),
                   pl.BlockSpec((IN_TM, HD), lambda i, j: (i, 0))],
        out_shape=[jax.ShapeDtypeStruct((T, N_MAIN), BF16),
                   jax.ShapeDtypeStruct((T, HD), F32)],
        scratch_shapes=[pltpu.VMEM((IN_TM, D), BF16)],
        name="in_proj",
        compiler_params=_cparams(("arbitrary", "arbitrary")),
    )(xa, g, mod_l, w_main, w_gate)


NBLK = SEQ // A_BLOCK
KPAD = SEG + A_BLOCK


def _dot_nt(a, b):
    return lax.dot_general(a, b, (((1,), (1,)), ((), ())), preferred_element_type=F32)


def _attend(scores, values, extra=None):
    m = scores[0].max(axis=-1, keepdims=True)
    for s in scores[1:]:
        m = jnp.maximum(m, s.max(axis=-1, keepdims=True))
    if extra is not None:
        m = jnp.maximum(m, extra)
    den = None if extra is None else jnp.exp(extra - m)
    out = None
    for s, v in zip(scores, values):
        e = jnp.exp(s - m)
        d = e.sum(axis=-1, keepdims=True)
        den = d if den is None else den + d
        o = jnp.dot(e.astype(BF16), v, preferred_element_type=F32)
        out = o if out is None else out + o
    return out / den


def _win_kernel(sink_ref, q_ref, k_ref, v_ref, cos_ref, sin_ref, o_ref, qs, ks, vs):
    kv = pl.program_id(1)
    lane = lax.broadcasted_iota(I32, (A_BLOCK, HD), 1)
    first_half = (lane % 64) < 32

    def rope(zf, cos, sin):
        zr = jnp.where(first_half, pltpu.roll(zf, 96, 1), pltpu.roll(zf, 32, 1))
        return zf * cos + zr * sin

    ks[0:CTX, :] = k_ref[0:CTX, :]
    ks[SEG:KPAD, :] = jnp.zeros((A_BLOCK, HD), BF16)
    vs[0:SEG, :] = v_ref[...]
    vs[SEG:KPAD, :] = jnp.zeros((A_BLOCK, HD), BF16)

    def rope_blk(i, c):
        r0 = pl.multiple_of(i * A_BLOCK, A_BLOCK)
        cos = cos_ref[pl.ds(r0, A_BLOCK), :]
        sin = sin_ref[pl.ds(r0, A_BLOCK), :]
        ks[pl.ds(CTX + r0, A_BLOCK), :] = rope(k_ref[pl.ds(CTX + r0, A_BLOCK), :].astype(F32), cos, sin).astype(BF16)
        for g in range(A_GRP):
            zf = q_ref[pl.ds(CTX + r0, A_BLOCK), g * HD:(g + 1) * HD].astype(F32)
            qs[i, g * A_BLOCK:(g + 1) * A_BLOCK, :] = rope(zf, cos, sin).astype(BF16)
        return c

    lax.fori_loop(0, NBLK, rope_blk, 0)

    nq = A_GRP * A_BLOCK
    row = lax.broadcasted_iota(I32, (nq, 1), 0)
    sink = jnp.where(row < A_BLOCK, sink_ref[kv * A_GRP],
                     jnp.where(row < 2 * A_BLOCK, sink_ref[kv * A_GRP + 1], sink_ref[kv * A_GRP + 2]))
    r = lax.broadcasted_iota(I32, (nq, 3 * A_BLOCK), 0) % A_BLOCK
    c = lax.broadcasted_iota(I32, (nq, 3 * A_BLOCK), 1)
    band = (c >= r) & (c <= r + 2 * A_BLOCK)

    def blk(i, carry):
        w0 = pl.multiple_of(CTX - A_BLOCK + i * A_BLOCK, A_BLOCK)
        q = qs[i]
        s_c = _dot_nt(q, ks[0:CTX, :]) * ATT_SCALE
        s_w = _dot_nt(q, ks[pl.ds(w0, 3 * A_BLOCK), :]) * ATT_SCALE
        kpos = (i - 1) * A_BLOCK + c
        s_w = jnp.where(band & (kpos >= 0) & (kpos < SEQ), s_w, NEG)
        o = _attend([s_c, s_w], [vs[0:CTX, :], vs[pl.ds(w0, 3 * A_BLOCK), :]], sink)
        o0 = pl.multiple_of(CTX + i * A_BLOCK, A_BLOCK)
        for g in range(A_GRP):
            o_ref[pl.ds(o0, A_BLOCK), g * HD:(g + 1) * HD] = o[g * A_BLOCK:(g + 1) * A_BLOCK].astype(BF16)
        return carry

    lax.fori_loop(0, NBLK, blk, 0, unroll=2)

    for g in range(A_GRP):
        s = _dot_nt(q_ref[0:CTX, g * HD:(g + 1) * HD], k_ref[0:CTX, :]) * ATT_SCALE
        o = _attend([s], [v_ref[0:CTX, :]], jnp.full((CTX, 1), sink_ref[kv * A_GRP + g], F32))
        o_ref[0:CTX, g * HD:(g + 1) * HD] = o.astype(BF16)


def window_attn(z, sink_l, cos_t, sin_t):
    return pl.pallas_call(
        _win_kernel,
        grid=(BATCH, A_KV),
        in_specs=[pl.BlockSpec(memory_space=pltpu.SMEM),
                  pl.BlockSpec((SEG, A_GRP * HD), lambda b, kv: (b, kv)),
                  pl.BlockSpec((SEG, HD), lambda b, kv: (b, COL_AK + kv)),
                  pl.BlockSpec((SEG, HD), lambda b, kv: (b, COL_AV + kv)),
                  pl.BlockSpec((SEQ, HD), lambda b, kv: (0, 0)),
                  pl.BlockSpec((SEQ, HD), lambda b, kv: (0, 0))],
        out_specs=pl.BlockSpec((SEG, A_GRP * HD), lambda b, kv: (b, kv)),
        out_shape=jax.ShapeDtypeStruct((T, A_HEADS * HD), BF16),
        scratch_shapes=[pltpu.VMEM((NBLK, A_GRP * A_BLOCK, HD), BF16),
                        pltpu.VMEM((KPAD, HD), BF16),
                        pltpu.VMEM((KPAD, HD), BF16)],
        name="window_attn",
        compiler_params=_cparams(("arbitrary", "arbitrary")),
    )(sink_l, z, z, z, cos_t, sin_t)


GRID_ROWS = SEQ // GRID_W
NBQ_ROWS = 4
NBK_ROWS = 12
NBQ, NBK = NBQ_ROWS * GRID_W, NBK_ROWS * GRID_W
N_NBLK = GRID_ROWS // NBQ_ROWS
NB_BASE_MAX = GRID_ROWS - NBK_ROWS


def _nbr_key_base(first_row):
    return jnp.clip(first_row - NB_ROWS // 2, 0, NB_BASE_MAX)


N_DR, N_DC = 2 * NB_ROWS - 1, 2 * NB_COLS - 1


def _nbr_block_offsets():
    out = []
    for first_row in (0, NBQ_ROWS, GRID_ROWS - NBQ_ROWS):
        base = min(max(first_row - NB_ROWS // 2, 0), NB_BASE_MAX)
        kind = []
        for qi in range(NBQ_ROWS):
            r = first_row + qi
            start = min(max(r - NB_ROWS // 2, 0), GRID_ROWS - NB_ROWS)
            kind.append([kr - r + NB_ROWS - 1 if start <= kr < start + NB_ROWS else N_DR
                         for kr in range(base, base + NBK_ROWS)])
        out.append(kind)
    return out


def _nbr_build_bias(rpb_ref, h, half, bias):
    qc = lax.broadcasted_iota(I32, (GRID_W, 2 * GRID_W), 0)
    lane = lax.broadcasted_iota(I32, (GRID_W, 2 * GRID_W), 1)
    kc = lane % GRID_W
    start_c = jnp.clip(qc - NB_COLS // 2, 0, GRID_W - NB_COLS)
    col_ok = (kc >= start_c) & (kc < start_c + NB_COLS)
    dc = kc - qc + NB_COLS - 1
    left = lane < GRID_W
    for d in range(N_DR):
        t = jnp.full((GRID_W, 2 * GRID_W), NEG, F32)
        for j in range(N_DC):
            t = jnp.where(dc == j, rpb_ref[(h * N_DR + d) * N_DC + j], t)
        t = jnp.where(col_ok, t, NEG)
        half[d, 0] = jnp.where(left, t, NEG)
        half[d, 1] = jnp.where(left, NEG, t)
    masked = jnp.full((GRID_W, 2 * GRID_W), NEG, F32)
    for kind, per_q in enumerate(_nbr_block_offsets()):
        for qi, dr in enumerate(per_q):
            for p in range(NBK_ROWS // 2):
                lo = half[dr[2 * p], 0] if dr[2 * p] < N_DR else masked
                hi = half[dr[2 * p + 1], 1] if dr[2 * p + 1] < N_DR else masked
                bias[kind, qi * GRID_W:(qi + 1) * GRID_W, p * 2 * GRID_W:(p + 1) * 2 * GRID_W] = jnp.maximum(lo, hi)


def _nbr_kernel(rpb_ref, q_ref, k_ref, v_ref, o_ref, half, bias):
    @pl.when(pl.program_id(1) == 0)
    def _():
        _nbr_build_bias(rpb_ref, pl.program_id(0), half, bias)

    def block(i):
        kind = jnp.where(i == 0, 0, jnp.where(i == N_NBLK - 1, 2, 1))
        q0 = pl.multiple_of(CTX + i * NBQ, NBQ)
        k0 = pl.multiple_of(CTX + _nbr_key_base(i * NBQ_ROWS) * GRID_W, GRID_W)
        q = q_ref[pl.ds(q0, NBQ), :]
        s_c = _dot_nt(q, k_ref[0:CTX, :]) * ATT_SCALE
        s_n = _dot_nt(q, k_ref[pl.ds(k0, NBK), :]) * ATT_SCALE + bias[kind]
        o = _attend([s_c, s_n], [v_ref[0:CTX, :], v_ref[pl.ds(k0, NBK), :]])
        o_ref[pl.ds(q0, NBQ), :] = o.astype(BF16)

    def blocks(i, carry):
        block(2 * i)
        block(2 * i + 1)
        return carry

    lax.fori_loop(0, N_NBLK // 2, blocks, 0)

    s = _dot_nt(q_ref[0:CTX, :], k_ref[0:CTX, :]) * ATT_SCALE
    o_ref[0:CTX, :] = _attend([s], [v_ref[0:CTX, :]]).astype(BF16)


def nbr_attn(z, rpb_l):
    return pl.pallas_call(
        _nbr_kernel,
        grid=(C_HEADS, BATCH),
        in_specs=[pl.BlockSpec(memory_space=pltpu.SMEM),
                  pl.BlockSpec((SEG, HD), lambda h, b: (b, COL_CQ + h)),
                  pl.BlockSpec((SEG, HD), lambda h, b: (b, COL_CK + h)),
                  pl.BlockSpec((SEG, HD), lambda h, b: (b, COL_CV + h))],
        out_specs=pl.BlockSpec((SEG, HD), lambda h, b: (b, h)),
        out_shape=jax.ShapeDtypeStruct((T, C_HEADS * HD), BF16),
        scratch_shapes=[pltpu.VMEM((N_DR, 2, GRID_W, 2 * GRID_W), F32), pltpu.VMEM((3, NBQ, NBK), F32)],
        name="nbr_attn",
        compiler_params=_cparams(("arbitrary", "arbitrary")),
    )(rpb_l.reshape(-1), z, z, z)


N_LCHUNK = SEQ // CHUNK
PADR = 8
CONV_ROWS = PADR + CTX + PADR + SEQ + PADR


def _conv_off(r0):
    return jnp.where(r0 < CTX, r0 + PADR, r0 + 2 * PADR)


def _mlstm_kernel(bg_ref, q_ref, k_ref, v_ref, og_ref, gr_ref, cwq_ref, cwk_ref, y_ref,
                  qs, ks, stage, hf, hb):
    h = pl.program_id(1)
    rowi = lax.broadcasted_iota(I32, (CHUNK, 1), 0)

    def conv_silu(src_ref, cw_ref, dst, post):
        stage[...] = jnp.zeros((CONV_ROWS, HD), F32)
        stage[PADR:PADR + CTX, :] = src_ref[0:CTX, :].astype(F32)
        stage[2 * PADR + CTX:2 * PADR + SEG, :] = src_ref[CTX:SEG, :].astype(F32)

        def blk(c, carry):
            r0 = pl.multiple_of(c * CHUNK, CHUNK)
            s0 = pl.multiple_of(_conv_off(r0), PADR)
            cur = stage[pl.ds(s0, CHUNK), :]
            prev_last = stage[pl.ds(s0 - PADR, PADR), :][PADR - 1:PADR, :]
            next_first = stage[pl.ds(s0 + CHUNK, PADR), :][0:1, :]
            zm = jnp.where(rowi == 0, prev_last, pltpu.roll(cur, 1, 0))
            zp = jnp.where(rowi == CHUNK - 1, next_first, pltpu.roll(cur, CHUNK - 1, 0))
            y = zm * cw_ref[0:1, :] + cur * cw_ref[1:2, :] + zp * cw_ref[2:3, :]
            dst[pl.ds(r0, CHUNK), :] = (_silu(y) * post).astype(BF16)
            return carry

        lax.fori_loop(0, SEG // CHUNK, blk, 0)

    conv_silu(q_ref, cwq_ref, qs, 1.0)
    conv_silu(k_ref, cwk_ref, ks, ATT_SCALE)

    t_idx = lax.broadcasted_iota(I32, (CHUNK, CHUNK), 0)
    s_idx = lax.broadcasted_iota(I32, (CHUNK, CHUNK), 1)

    def chunk(r0, bwd, state, dst):
        c_mat, n_vec, m_prev = state
        ki, kf = (2, 3) if bwd else (0, 1)
        bi = bg_ref[ki * B_HEADS + h]
        bf = bg_ref[kf * B_HEADS + h]
        qc = qs[pl.ds(r0, CHUNK), :]
        kc = ks[pl.ds(r0, CHUNK), :]
        vc = v_ref[pl.ds(r0, CHUNK), :]
        i_row = gr_ref[pl.ds(ki * B_HEADS + h, 1), pl.ds(r0, CHUNK)] + bi
        lf_row = _log_sigmoid(gr_ref[pl.ds(kf * B_HEADS + h, 1), pl.ds(r0, CHUNK)] + bf)
        causal = (s_idx >= t_idx) if bwd else (s_idx <= t_idx)
        diag = s_idx == t_idx
        b_col = jnp.sum(jnp.where(causal, lf_row, 0.0), axis=1, keepdims=True)
        b_row = jnp.sum(jnp.where(diag, b_col, 0.0), axis=0, keepdims=True)
        a_row = i_row - b_row
        a_col = jnp.sum(jnp.where(diag, a_row, 0.0), axis=1, keepdims=True)
        mx = jnp.maximum(m_prev, jnp.max(jnp.where(causal, a_row, NEG), axis=1, keepdims=True))
        dm = jnp.exp(jnp.where(causal, a_row - mx, NEG))
        s = _dot_nt(qc, kc) * dm
        w_int = jnp.exp(m_prev - mx)
        num = (jnp.dot(s.astype(BF16), vc, preferred_element_type=F32)
               + w_int * jnp.dot(qc, c_mat.astype(BF16), preferred_element_type=F32))
        den = (jnp.sum(s, axis=1, keepdims=True)
               + w_int * jnp.sum(qc.astype(F32) * n_vec, axis=1, keepdims=True))
        m_t = b_col + mx
        dst[pl.ds(r0, CHUNK), :] = num / jnp.maximum(jnp.abs(den), jnp.exp(-m_t))
        b_end = jnp.sum(lf_row, axis=1, keepdims=True)
        m_end = jnp.maximum(m_prev, jnp.max(a_row, axis=1, keepdims=True))
        decay = jnp.exp(m_prev - m_end)
        w_col = jnp.exp(a_col - m_end)
        kv = lax.dot_general(kc, (w_col * vc.astype(F32)).astype(BF16), (((0,), (0,)), ((), ())),
                             preferred_element_type=F32)
        c_new = decay * c_mat + kv
        n_new = decay * n_vec + jnp.sum(w_col * kc.astype(F32), axis=0, keepdims=True)
        return c_new, n_new, b_end + m_end

    zero = (jnp.zeros((HD, HD), F32), jnp.zeros((1, HD), F32), jnp.zeros((1, 1), F32))
    st_f = chunk(0, False, zero, hf)
    st_b = chunk(0, True, zero, hb)

    def body(j, carry):
        sf, sb = carry
        rf = pl.multiple_of(CTX + j * CHUNK, CHUNK)
        rb = pl.multiple_of(CTX + (N_LCHUNK - 1 - j) * CHUNK, CHUNK)
        return chunk(rf, False, sf, hf), chunk(rb, True, sb, hb)

    lax.fori_loop(0, N_LCHUNK, body, (st_f, st_b))

    def fin(c, carry):
        r0 = pl.multiple_of(c * CHUNK, CHUNK)
        gate = jax.nn.sigmoid(og_ref[pl.ds(r0, CHUNK), :].astype(F32))
        y_ref[pl.ds(r0, CHUNK), :] = (gate * (hf[pl.ds(r0, CHUNK), :] + hb[pl.ds(r0, CHUNK), :])).astype(BF16)
        return carry

    lax.fori_loop(0, SEG // CHUNK, fin, 0)


def mlstm(z, g_row, b_gates_l, conv_l):
    return pl.pallas_call(
        _mlstm_kernel,
        grid=(BATCH, B_HEADS),
        in_specs=[pl.BlockSpec(memory_space=pltpu.SMEM),
                  pl.BlockSpec((SEG, HD), lambda b, h: (b, COL_BQ + h)),
                  pl.BlockSpec((SEG, HD), lambda b, h: (b, COL_BK + h)),
                  pl.BlockSpec((SEG, HD), lambda b, h: (b, COL_BV + h)),
                  pl.BlockSpec((SEG, HD), lambda b, h: (b, COL_BO + h)),
                  pl.BlockSpec((4 * B_HEADS, SEG), lambda b, h: (0, b)),
                  pl.BlockSpec((3, HD), lambda b, h: (0, h)),
                  pl.BlockSpec((3, HD), lambda b, h: (0, B_HEADS + h))],
        out_specs=pl.BlockSpec((SEG, HD), lambda b, h: (b, h)),
        out_shape=jax.ShapeDtypeStruct((T, B_HEADS * HD), BF16),
        scratch_shapes=[pltpu.VMEM((SEG, HD), BF16), pltpu.VMEM((SEG, HD), BF16),
                        pltpu.VMEM((CONV_ROWS, HD), F32),
                        pltpu.VMEM((SEG, HD), F32), pltpu.VMEM((SEG, HD), F32)],
        name="mlstm",
        compiler_params=_cparams(("arbitrary", "arbitrary")),
    )(b_gates_l, z, z, z, z, g_row, conv_l, conv_l)


OUT_TM = SEG // 6


def _top2_routing(biased, scores, comb_ref, sel_ref):
    rows = [biased[e:e + 1, :] for e in range(N_EXPERTS)]
    gscore = []
    for g in range(N_GROUPS):
        v0, v1, v2, v3 = rows[4 * g:4 * g + 4]
        hi01, lo01 = jnp.maximum(v0, v1), jnp.minimum(v0, v1)
        hi23, lo23 = jnp.maximum(v2, v3), jnp.minimum(v2, v3)
        gscore.append(jnp.maximum(hi01, hi23) + jnp.maximum(jnp.minimum(hi01, hi23), jnp.maximum(lo01, lo23)))
    picked = []
    total = None
    for e in range(N_EXPERTS):
        g = e // PER_GROUP
        ok = None
        for o in range(N_GROUPS):
            if o != g:
                c = (gscore[g] > gscore[o]) if o < g else (gscore[g] >= gscore[o])
                ok = c if ok is None else (ok & c)
        ahead = jnp.zeros_like(rows[e])
        for o in range(4 * g, 4 * g + 4):
            if o != e:
                c = (rows[o] > rows[e]) if o > e else (rows[o] >= rows[e])
                ahead = ahead + jnp.where(c, 1.0, 0.0)
        sel = jnp.where(ok & (ahead < 2.0), 1.0, 0.0)
        sel_ref[e:e + 1, :] = sel
        picked.append(sel * scores[e:e + 1, :])
        total = picked[-1] if total is None else total + picked[-1]
    for e in range(N_EXPERTS):
        comb_ref[e:e + 1, :] = picked[e] / total


OUT_RC = 16


N_OUT_TILES = T // OUT_TM


def _out_kernel(ya_ref, yb_ref, yn_ref, w_ref, x_ref, mod_ref, g_ref, wrh_ref, wrl_ref, br_ref,
                x1_ref, h2_ref, comb_ref, sel_ref, ycat, acc, acc_next, h_hi, h_lo):
    i = pl.program_id(0)

    def matmul():
        ycat[:, 0:A_HEADS * HD] = ya_ref[...]
        ycat[:, A_HEADS * HD:(A_HEADS + B_HEADS) * HD] = yb_ref[...]
        ycat[:, (A_HEADS + B_HEADS) * HD:D] = yn_ref[...]
        acc_next[...] = jnp.dot(ycat[...], w_ref[...], preferred_element_type=F32)

    def epilogue():
        t = i - 1
        batch = t // 6
        for c in range(OUT_TM // OUT_RC):
            rows = slice(c * OUT_RC, (c + 1) * OUT_RC)
            mrow = jnp.where((t % 6) * OUT_TM + c * OUT_RC < CTX, 4, batch)

            def mod(k):
                return mod_ref[pl.ds(mrow, 1), k * D:(k + 1) * D]

            x1 = x_ref[rows, :] + mod(2) * acc[rows, :]
            x1_ref[rows, :] = x1
            h2 = (_rms(x1) * g_ref[...]) * (1.0 + mod(4)) + mod(3)
            h2_ref[rows, :] = h2
            hi = h2.astype(BF16)
            h_hi[rows, :] = hi
            h_lo[rows, :] = (h2 - hi.astype(F32)).astype(BF16)
        scores = jax.nn.sigmoid(_dot_nt(wrh_ref[...], h_hi[...]) + _dot_nt(wrl_ref[...], h_hi[...])
                                + _dot_nt(wrh_ref[...], h_lo[...]))
        _top2_routing(scores + br_ref[...], scores, comb_ref, sel_ref)

    @pl.when(i == 0)
    def _():
        matmul()

    @pl.when((i > 0) & (i < N_OUT_TILES))
    def _():
        matmul()
        epilogue()

    @pl.when(i == N_OUT_TILES)
    def _():
        epilogue()

    @pl.when(i < N_OUT_TILES)
    def _():
        acc[...] = acc_next[...]


def out_proj(ya, yb, yn, w_out_b, xa, mod_l, g, w_router_hi, w_router_lo, b_router_c, layer):
    cur = lambda i: (jnp.minimum(i, N_OUT_TILES - 1), 0)
    row = lambda i: (jnp.maximum(i - 1, 0), 0)
    fixed = lambda i: (0, 0)
    return pl.pallas_call(
        _out_kernel,
        grid=(N_OUT_TILES + 1,),
        in_specs=[pl.BlockSpec((OUT_TM, A_HEADS * HD), cur),
                  pl.BlockSpec((OUT_TM, B_HEADS * HD), cur),
                  pl.BlockSpec((OUT_TM, C_HEADS * HD), cur),
                  pl.BlockSpec((None, D, D), lambda i: (layer, 0, 0)),
                  pl.BlockSpec((OUT_TM, D), row),
                  pl.BlockSpec((8, 6 * D), fixed),
                  pl.BlockSpec((1, D), fixed),
                  pl.BlockSpec((N_EXPERTS, D), fixed),
                  pl.BlockSpec((N_EXPERTS, D), fixed),
                  pl.BlockSpec((N_EXPERTS, 1), fixed)],
        out_specs=[pl.BlockSpec((OUT_TM, D), row),
                   pl.BlockSpec((OUT_TM, D), row),
                   pl.BlockSpec((N_EXPERTS, OUT_TM), lambda i: (0, jnp.maximum(i - 1, 0))),
                   pl.BlockSpec((N_EXPERTS, OUT_TM), lambda i: (0, jnp.maximum(i - 1, 0)))],
        out_shape=[jax.ShapeDtypeStruct((T, D), F32),
                   jax.ShapeDtypeStruct((T, D), F32),
                   jax.ShapeDtypeStruct((N_EXPERTS, T), F32),
                   jax.ShapeDtypeStruct((N_EXPERTS, T), F32)],
        scratch_shapes=[pltpu.VMEM((OUT_TM, D), BF16), pltpu.VMEM((OUT_TM, D), F32), pltpu.VMEM((OUT_TM, D), F32),
                        pltpu.VMEM((OUT_TM, D), BF16), pltpu.VMEM((OUT_TM, D), BF16)],
        name="out_proj",
        compiler_params=_cparams(("arbitrary",)),
    )(ya, yb, yn, w_out_b, xa, mod_l, g, w_router_hi, w_router_lo, b_router_c)


SC_TM = 1024
ROW_DMA_UNROLL = 8


def _scatter_kernel(pos0_ref, pos1_ref, ztile_ref, h_ref, hs_ref, zero_scr, sem, zsem):
    i = pl.program_id(0)

    @pl.when(i == 0)
    def _():
        zero_scr[...] = jnp.zeros((ETILE, D), F32)

        def fill_copy(t):
            return pltpu.make_async_copy(zero_scr, hs_ref.at[pl.ds(pl.multiple_of(t * ETILE, ETILE), ETILE), :], zsem)

        def fill_start(e, c):
            @pl.when(ztile_ref[e] >= 0)
            def _():
                fill_copy(ztile_ref[e]).start()
            return c

        def fill_wait(e, c):
            @pl.when(ztile_ref[e] >= 0)
            def _():
                fill_copy(ztile_ref[e]).wait()
            return c

        lax.fori_loop(0, 2 * N_EXPERTS, fill_start, 0)
        lax.fori_loop(0, 2 * N_EXPERTS, fill_wait, 0)

    def row_copy(r, p):
        return pltpu.make_async_copy(h_ref.at[pl.ds(r, 1), :], hs_ref.at[pl.ds(p, 1), :], sem)

    def issue(g, c):
        r0 = pl.multiple_of(g * ROW_DMA_UNROLL, ROW_DMA_UNROLL)
        for u in range(ROW_DMA_UNROLL):
            t = i * SC_TM + r0 + u
            row_copy(r0 + u, pos0_ref[t]).start()
            row_copy(r0 + u, pos1_ref[t]).start()
        return c

    lax.fori_loop(0, SC_TM // ROW_DMA_UNROLL, issue, 0)

    def drain(r, c):
        row_copy(0, 0).wait()
        row_copy(0, 0).wait()
        return c

    lax.fori_loop(0, SC_TM, drain, 0, unroll=ROW_DMA_UNROLL)


def scatter_rows(pos0, pos1, ztile, h2):
    return pl.pallas_call(
        _scatter_kernel,
        grid_spec=pltpu.PrefetchScalarGridSpec(
            num_scalar_prefetch=3,
            grid=(T // SC_TM,),
            in_specs=[pl.BlockSpec((SC_TM, D), lambda i, *_: (i, 0))],
            out_specs=pl.BlockSpec(memory_space=pl.ANY),
            scratch_shapes=[pltpu.VMEM((ETILE, D), F32), pltpu.SemaphoreType.DMA, pltpu.SemaphoreType.DMA]),
        out_shape=jax.ShapeDtypeStruct((P_ROWS, D), F32),
        name="scatter_rows",
        compiler_params=_cparams(("arbitrary",)),
    )(pos0, pos1, ztile, h2)


CAST_ROWS = 256


def _cast_into(dst, src, n_rows):
    def body(c, carry):
        r0 = pl.multiple_of(c * CAST_ROWS, CAST_ROWS)
        dst[pl.ds(r0, CAST_ROWS), :] = src[pl.ds(r0, CAST_ROWS), :].astype(BF16)
        return carry

    lax.fori_loop(0, n_rows // CAST_ROWS, body, 0)


def _ffn_kernel(te_ref, tv_ref, first_ref, nxt_ref, hs_ref, w1_hbm, w3_hbm, w2_hbm, ys_ref,
                st1, st3, st2, w1b, w3b, w2b, sem, *, layer):
    j = pl.program_id(0)

    def fetch(e):
        return (pltpu.make_async_copy(w1_hbm.at[layer, e], st1, sem.at[0]),
                pltpu.make_async_copy(w3_hbm.at[layer, e], st3, sem.at[1]),
                pltpu.make_async_copy(w2_hbm.at[layer, e], st2, sem.at[2]))

    @pl.when(j == 0)
    def _():
        for cp in fetch(te_ref[0]):
            cp.start()

    @pl.when(first_ref[j] == 1)
    def _():
        for cp in fetch(te_ref[j]):
            cp.wait()
        _cast_into(w1b, st1, D)
        _cast_into(w3b, st3, D)
        _cast_into(w2b, st2, FF)

        @pl.when(nxt_ref[j] >= 0)
        def _():
            for cp in fetch(nxt_ref[j]):
                cp.start()

    def expert_rows(n):
        xb = hs_ref[0:n, :].astype(BF16)
        a = jnp.dot(xb, w1b[...], preferred_element_type=F32)
        b = jnp.dot(xb, w3b[...], preferred_element_type=F32)
        act = (_silu(a) * b).astype(BF16)
        ys_ref[0:n, :] = jnp.dot(act, w2b[...], preferred_element_type=F32)

    @pl.when(tv_ref[j] > ETILE // 2)
    def _():
        expert_rows(ETILE)

    @pl.when((tv_ref[j] > 0) & (tv_ref[j] <= ETILE // 2))
    def _():
        expert_rows(ETILE // 2)
        ys_ref[ETILE // 2:ETILE, :] = jnp.zeros((ETILE // 2, D), F32)

    @pl.when(tv_ref[j] == 0)
    def _():
        ys_ref[...] = jnp.zeros((ETILE, D), F32)


def _tile_or_first(j, te, tv, *_):
    return jnp.where(tv[j] > 0, j, 0)


def ffn(plan, hs, w1, w3, w2, layer):
    return pl.pallas_call(
        functools.partial(_ffn_kernel, layer=layer),
        grid_spec=pltpu.PrefetchScalarGridSpec(
            num_scalar_prefetch=4,
            grid=(N_ETILES,),
            in_specs=[pl.BlockSpec((ETILE, D), lambda j, *p: (_tile_or_first(j, *p), 0)),
                      pl.BlockSpec(memory_space=pl.ANY),
                      pl.BlockSpec(memory_space=pl.ANY),
                      pl.BlockSpec(memory_space=pl.ANY)],
            out_specs=pl.BlockSpec((ETILE, D), lambda j, *p: (j, 0)),
            scratch_shapes=[pltpu.VMEM((D, FF), F32), pltpu.VMEM((D, FF), F32), pltpu.VMEM((FF, D), F32),
                            pltpu.VMEM((D, FF), BF16), pltpu.VMEM((D, FF), BF16), pltpu.VMEM((FF, D), BF16),
                            pltpu.SemaphoreType.DMA((3,))]),
        out_shape=jax.ShapeDtypeStruct((P_ROWS, D), F32),
        name="ffn",
        compiler_params=_cparams(("arbitrary",)),
    )(*plan, hs, w1, w3, w2)


CB_TM = CTX
CB_PER_SEG = SEG // CB_TM


CB_RC = 32


def _combine_kernel(pos0_ref, pos1_ref, ys_ref, x_ref, w_ref, mod_ref, g_ref, o_ref, buf, sem, *, final):
    i = pl.program_id(0)
    n = pl.num_programs(0)
    slot = i % 2

    def row_copy(s, k, r, p):
        return pltpu.make_async_copy(ys_ref.at[pl.ds(p, 1), :], buf.at[s, k, pl.ds(r, 1), :], sem.at[s])

    def gather(tile, s):
        def issue(g, c):
            r0 = pl.multiple_of(g * ROW_DMA_UNROLL, ROW_DMA_UNROLL)
            for u in range(ROW_DMA_UNROLL):
                t = tile * CB_TM + r0 + u
                row_copy(s, 0, r0 + u, pos0_ref[t]).start()
                row_copy(s, 1, r0 + u, pos1_ref[t]).start()
            return c

        lax.fori_loop(0, CB_TM // ROW_DMA_UNROLL, issue, 0)

    @pl.when(i == 0)
    def _():
        gather(0, 0)

    @pl.when(i + 1 < n)
    def _():
        gather(i + 1, 1 - slot)

    def drain(r, c):
        row_copy(slot, 0, 0, 0).wait()
        row_copy(slot, 1, 0, 0).wait()
        return c

    lax.fori_loop(0, CB_TM, drain, 0, unroll=ROW_DMA_UNROLL)

    def chunk(c, carry):
        r0 = pl.multiple_of(c * CB_RC, CB_RC)
        rows = pl.ds(r0, CB_RC)
        is_ctx = ((i % CB_PER_SEG) == 0) & (r0 < CTX)
        gate = mod_ref[pl.ds(jnp.where(is_ctx, 4, i // CB_PER_SEG), 1), 5 * D:6 * D]
        f = w_ref[rows, 0:1] * buf[slot, 0, rows, :] + w_ref[rows, 1:2] * buf[slot, 1, rows, :]
        v = x_ref[rows, :] + gate * f
        o_ref[rows, :] = _rms(v) * g_ref[...] if final else v
        return carry

    lax.fori_loop(0, CB_TM // CB_RC, chunk, 0)


def combine_rows(pos0, pos1, ys, x1, w01, mod_l, g_final, final):
    if final:
        lat = SEQ // CB_TM
        out_map = lambda i, *_: ((i // CB_PER_SEG) * lat + jnp.maximum(i % CB_PER_SEG - 1, 0), 0)
        out_rows = BATCH * SEQ
    else:
        out_map = lambda i, *_: (i, 0)
        out_rows = T
    return pl.pallas_call(
        functools.partial(_combine_kernel, final=final),
        grid_spec=pltpu.PrefetchScalarGridSpec(
            num_scalar_prefetch=2,
            grid=(T // CB_TM,),
            in_specs=[pl.BlockSpec(memory_space=pl.ANY),
                      pl.BlockSpec((CB_TM, D), lambda i, *_: (i, 0)),
                      pl.BlockSpec((CB_TM, 2), lambda i, *_: (i, 0)),
                      pl.BlockSpec((8, 6 * D), lambda i, *_: (0, 0)),
                      pl.BlockSpec((1, D), lambda i, *_: (0, 0))],
            out_specs=pl.BlockSpec((CB_TM, D), out_map),
            scratch_shapes=[pltpu.VMEM((2, 2, CB_TM, D), F32), pltpu.SemaphoreType.DMA((2,))]),
        out_shape=jax.ShapeDtypeStruct((out_rows, D), F32),
        name="combine_rows",
        compiler_params=_cparams(("arbitrary",)),
    )(pos0, pos1, ys, x1, w01, mod_l, g_final)


def route_plan(comb_t, sel_t):
    sel = sel_t > 0.5
    cnt = jnp.sum(sel, axis=1).astype(I32)
    tiles = (cnt + ETILE - 1) // ETILE
    tend = jnp.cumsum(tiles)
    toff = tend - tiles
    rank = jnp.cumsum(sel.astype(I32), axis=1) - 1
    pos = toff[:, None] * ETILE + rank
    pos0 = jnp.min(jnp.where(sel, pos, P_ROWS), axis=0).astype(I32)
    pos1 = jnp.max(jnp.where(sel, pos, -1), axis=0).astype(I32)
    w0 = jnp.sum(jnp.where(sel & (pos == pos0[None]), comb_t, 0.0), axis=0)
    w1 = jnp.sum(jnp.where(sel & (pos == pos1[None]), comb_t, 0.0), axis=0)
    n_used = tend[-1]
    tidx = jnp.arange(N_ETILES, dtype=I32)
    te_raw = jnp.sum((tend[None, :] <= tidx[:, None]).astype(I32), axis=1)
    te_last = jnp.sum((tend <= n_used - 1).astype(I32))
    te = jnp.where(tidx < n_used, te_raw, te_last).astype(I32)
    te = jnp.minimum(te, N_EXPERTS - 1)
    tv = jnp.clip(cnt[te] - (tidx - toff[te]) * ETILE, 0, ETILE)
    tv = jnp.where(tidx < n_used, tv, 0).astype(I32)
    tail = n_used + jnp.arange(N_EXPERTS, dtype=I32)
    ztile = jnp.concatenate([jnp.where(tiles > 0, tend - 1, -1), jnp.where(tail < N_ETILES, tail, -1)]).astype(I32)
    used = tidx < n_used
    first = (used & ((tidx == 0) | (te != jnp.roll(te, 1)))).astype(I32)
    nxt_tile = tend[te]
    nxt = jnp.where(nxt_tile < n_used, te[jnp.minimum(nxt_tile, N_ETILES - 1)], -1).astype(I32)
    return pos0, pos1, jnp.stack([w0, w1], axis=1), (te, tv, first, nxt), ztile


def _rope_tables():
    t = jnp.arange(SEQ, dtype=I32)
    row = (t // GRID_W).astype(F32)
    col = (t % GRID_W).astype(F32)
    n_freq = HD // 4
    inv_freq = ROPE_BASE ** (-jnp.arange(n_freq, dtype=F32) / n_freq)
    ar = row[:, None] * inv_freq[None, :]
    ac = col[:, None] * inv_freq[None, :]
    cos_t = jnp.concatenate([jnp.cos(ar), jnp.cos(ar), jnp.cos(ac), jnp.cos(ac)], axis=1)
    sin_t = jnp.concatenate([-jnp.sin(ar), jnp.sin(ar), -jnp.sin(ac), jnp.sin(ac)], axis=1)
    return cos_t, sin_t


GATE_LO, GATE_HI = 3328, 3344
D_IN = N_MAIN + 4 * B_HEADS
D_IN_PAD = N_MAIN + HD
RP_ROWS = 256


def _repack_kernel(w_ref, o_ref, g_ref):
    o_ref[0, :, 0:GATE_LO] = w_ref[0, :, 0:GATE_LO].astype(BF16)
    o_ref[0, :, GATE_LO:N_MAIN] = w_ref[0, :, GATE_HI:D_IN].astype(BF16)
    lane = lax.broadcasted_iota(I32, (RP_ROWS, HD), 1)
    g_ref[0] = jnp.where(lane < 4 * B_HEADS, w_ref[0, :, GATE_LO:GATE_LO + HD], 0.0).astype(BF16)


def repack_w_in(w_in):
    blk = lambda l, r: (l, r, 0)
    return pl.pallas_call(
        _repack_kernel,
        grid=(DEPTH, D // RP_ROWS),
        in_specs=[pl.BlockSpec((1, RP_ROWS, D_IN_PAD), blk)],
        out_specs=[pl.BlockSpec((1, RP_ROWS, N_MAIN), blk), pl.BlockSpec((1, RP_ROWS, HD), blk)],
        out_shape=[jax.ShapeDtypeStruct((DEPTH, D, N_MAIN), BF16), jax.ShapeDtypeStruct((DEPTH, D, HD), BF16)],
        name="repack_w_in",
        compiler_params=_cparams(("arbitrary", "arbitrary")),
    )(w_in)


def kernel(x, c, ctx, c_ctx, w_ada, b_ada, norm_mix, norm_ffn, w_in, b_gates, conv_qk, sink, rpb, w_out,
           w_router, b_router, w1, w3, w2, norm_final):
    xa = jnp.concatenate([ctx, x], axis=1).reshape(T, D)
    cond8 = jnp.concatenate([c, c_ctx[None], jnp.zeros((3, D), F32)], axis=0)
    mod = ada_all(cond8, w_ada, b_ada)
    cos_t, sin_t = _rope_tables()
    w_main, w_gate = repack_w_in(jnp.pad(w_in, ((0, 0), (0, 0), (0, D_IN_PAD - D_IN))).astype(BF16))
    w_out_b = w_out.astype(BF16)
    w_router_hi = w_router.T.astype(BF16)
    w_router_lo = (w_router.T - w_router_hi.astype(F32)).astype(BF16)
    b_router_c = b_router.reshape(N_EXPERTS, 1)

    for l in range(DEPTH):
        z, zg = in_proj(xa, norm_mix[l].reshape(1, D), mod[l], w_main, w_gate, l)
        g_row = zg[:, :4 * B_HEADS].T
        ya = window_attn(z, sink[l], cos_t, sin_t)
        yb = mlstm(z, g_row, b_gates[l], conv_qk[l])
        yn = nbr_attn(z, rpb[l])
        x1, h2, comb_t, sel_t = out_proj(ya, yb, yn, w_out_b, xa, mod[l], norm_ffn[l].reshape(1, D),
                                         w_router_hi, w_router_lo, b_router_c, l)
        pos0, pos1, w01, plan, ztile = route_plan(comb_t, sel_t)
        hs = scatter_rows(pos0, pos1, ztile, h2)
        ys = ffn(plan, hs, w1, w3, w2, l)
        xa = combine_rows(pos0, pos1, ys, x1, w01, mod[l], norm_final.reshape(1, D), l == DEPTH - 1)
    return xa.reshape(BATCH, SEQ, D)
```

```python
import functools

import jax
import jax.numpy as jnp
from jax import lax
from jax.experimental import pallas as pl
from jax.experimental.pallas import tpu as pltpu

F32 = jnp.float32
BF16 = jnp.bfloat16
I32 = jnp.int32

D = 2048
BATCH = 4
SEQ = 2048
CTX = 256
SEG = CTX + SEQ
T = BATCH * SEG
DEPTH = 4
GRID_W = 64
HD = 128
A_HEADS, A_KV, A_GRP = 6, 2, 3
B_HEADS = 4
C_HEADS = 6
A_BLOCK = 128
NB_ROWS, NB_COLS = 8, 16
CHUNK = 256
N_EXPERTS, N_GROUPS, PER_GROUP = 16, 4, 4
FF = 1024
EPS = 1e-6
ROPE_BASE = 10000.0
NEG = -1e30
ATT_SCALE = HD ** -0.5

N_MAIN = 5632
COL_AQ, COL_AK, COL_AV = 0, 6, 8
COL_BQ, COL_BK, COL_BV, COL_BO = 10, 14, 18, 22
COL_CQ, COL_CK, COL_CV = 26, 32, 38

ETILE = 256
N_ETILES = (2 * T) // ETILE + N_EXPERTS
P_ROWS = N_ETILES * ETILE

VMEM_LIMIT = 56 * 1024 * 1024


def _cparams(sem):
    return pltpu.CompilerParams(dimension_semantics=sem, vmem_limit_bytes=VMEM_LIMIT)


def _silu(v):
    return v * jax.nn.sigmoid(v)


def _log_sigmoid(v):
    return jnp.minimum(v, 0.0) - jnp.log1p(jnp.exp(-jnp.abs(v)))


ADA_TN = 1024


def _ada_kernel(s_ref, w_ref, b_ref, o_ref):
    s = _silu(s_ref[...]).astype(BF16)
    o_ref[0] = jnp.dot(s, w_ref[0].astype(BF16), preferred_element_type=F32) + b_ref[0]


def ada_all(cond8, w_ada, b_ada):
    n = w_ada.shape[-1]
    return pl.pallas_call(
        _ada_kernel,
        grid=(DEPTH, n // ADA_TN),
        in_specs=[pl.BlockSpec((8, D), lambda l, j: (0, 0)),
                  pl.BlockSpec((1, D, ADA_TN), lambda l, j: (l, 0, j)),
                  pl.BlockSpec((1, 1, ADA_TN), lambda l, j: (l, 0, j))],
        out_specs=pl.BlockSpec((1, 8, ADA_TN), lambda l, j: (l, 0, j)),
        out_shape=jax.ShapeDtypeStruct((DEPTH, 8, n), F32),
        name="ada_mod",
        compiler_params=_cparams(("arbitrary", "arbitrary")),
    )(cond8, w_ada, b_ada.reshape(DEPTH, 1, n))


def _row_mod(mod_ref, chunk, batch, is_ctx):
    lat = mod_ref[pl.ds(batch, 1), chunk * D:(chunk + 1) * D]
    ctx = mod_ref[4:5, chunk * D:(chunk + 1) * D]
    return jnp.where(is_ctx, ctx, lat)


def _rms(x):
    return x * lax.rsqrt(jnp.mean(x * x, axis=-1, keepdims=True) + EPS)


IN_TM = SEG // 2
IN_TN = 1408


def _in_kernel(x_ref, g_ref, mod_ref, w_ref, wg_ref, z_ref, zg_ref, h_scr):
    i = pl.program_id(0)
    j = pl.program_id(1)

    @pl.when(j == 0)
    def _():
        batch = i // 2
        rows = lax.broadcasted_iota(I32, (IN_TM, 1), 0) + (i % 2) * IN_TM
        is_ctx = rows < CTX
        xn = _rms(x_ref[...]) * g_ref[...]
        h = xn * (1.0 + _row_mod(mod_ref, 1, batch, is_ctx)) + _row_mod(mod_ref, 0, batch, is_ctx)
        hb = h.astype(BF16)
        h_scr[...] = hb
        zg_ref[...] = jnp.dot(hb, wg_ref[...], preferred_element_type=F32)

    z_ref[...] = jnp.dot(h_scr[...], w_ref[...], preferred_element_type=F32).astype(BF16)


def in_proj(xa, g, mod_l, w_main, w_gate, layer):
    return pl.pallas_call(
        _in_kernel,
        grid=(T // IN_TM, N_MAIN // IN_TN),
        in_specs=[pl.BlockSpec((IN_TM, D), lambda i, j: (i, 0)),
                  pl.BlockSpec((1, D), lambda i, j: (0, 0)),
                  pl.BlockSpec((8, 6 * D), lambda i, j: (0, 0)),
                  pl.BlockSpec((None, D, IN_TN), lambda i, j: (layer, 0, j)),
                  pl.BlockSpec((None, D, HD), lambda i, j: (layer, 0, 0))],
        out_specs=[pl.BlockSpec((IN_TM, IN_TN), lambda i, j: (i, j)),
                   pl.BlockSpec((IN_TM, HD), lambda i, j: (i, 0))],
        out_shape=[jax.ShapeDtypeStruct((T, N_MAIN), BF16),
                   jax.ShapeDtypeStruct((T, HD), F32)],
        scratch_shapes=[pltpu.VMEM((IN_TM, D), BF16)],
        name="in_proj",
        compiler_params=_cparams(("arbitrary", "arbitrary")),
    )(xa, g, mod_l, w_main, w_gate)


NBLK = SEQ // A_BLOCK
KPAD = SEG + A_BLOCK


def _dot_nt(a, b):
    return lax.dot_general(a, b, (((1,), (1,)), ((), ())), preferred_element_type=F32)


def _attend(scores, values, extra=None):
    m = scores[0].max(axis=-1, keepdims=True)
    for s in scores[1:]:
        m = jnp.maximum(m, s.max(axis=-1, keepdims=True))
    if extra is not None:
        m = jnp.maximum(m, extra)
    den = None if extra is None else jnp.exp(extra - m)
    out = None
    for s, v in zip(scores, values):
        e = jnp.exp(s - m)
        d = e.sum(axis=-1, keepdims=True)
        den = d if den is None else den + d
        o = jnp.dot(e.astype(BF16), v, preferred_element_type=F32)
        out = o if out is None else out + o
    return out / den


def _win_kernel(sink_ref, q_ref, k_ref, v_ref, cos_ref, sin_ref, o_ref, qs, ks, vs):
    kv = pl.program_id(1)
    lane = lax.broadcasted_iota(I32, (A_BLOCK, HD), 1)
    first_half = (lane % 64) < 32

    def rope(zf, cos, sin):
        zr = jnp.where(first_half, pltpu.roll(zf, 96, 1), pltpu.roll(zf, 32, 1))
        return zf * cos + zr * sin

    ks[0:CTX, :] = k_ref[0:CTX, :]
    ks[SEG:KPAD, :] = jnp.zeros((A_BLOCK, HD), BF16)
    vs[0:SEG, :] = v_ref[...]
    vs[SEG:KPAD, :] = jnp.zeros((A_BLOCK, HD), BF16)

    def rope_blk(i, c):
        r0 = pl.multiple_of(i * A_BLOCK, A_BLOCK)
        cos = cos_ref[pl.ds(r0, A_BLOCK), :]
        sin = sin_ref[pl.ds(r0, A_BLOCK), :]
        ks[pl.ds(CTX + r0, A_BLOCK), :] = rope(k_ref[pl.ds(CTX + r0, A_BLOCK), :].astype(F32), cos, sin).astype(BF16)
        for g in range(A_GRP):
            zf = q_ref[pl.ds(CTX + r0, A_BLOCK), g * HD:(g + 1) * HD].astype(F32)
            qs[i, g * A_BLOCK:(g + 1) * A_BLOCK, :] = rope(zf, cos, sin).astype(BF16)
        return c

    lax.fori_loop(0, NBLK, rope_blk, 0)

    nq = A_GRP * A_BLOCK
    row = lax.broadcasted_iota(I32, (nq, 1), 0)
    sink = jnp.where(row < A_BLOCK, sink_ref[kv * A_GRP],
                     jnp.where(row < 2 * A_BLOCK, sink_ref[kv * A_GRP + 1], sink_ref[kv * A_GRP + 2]))
    r = lax.broadcasted_iota(I32, (nq, 3 * A_BLOCK), 0) % A_BLOCK
    c = lax.broadcasted_iota(I32, (nq, 3 * A_BLOCK), 1)
    band = (c >= r) & (c <= r + 2 * A_BLOCK)

    def blk(i, carry):
        w0 = pl.multiple_of(CTX - A_BLOCK + i * A_BLOCK, A_BLOCK)
        q = qs[i]
        s_c = _dot_nt(q, ks[0:CTX, :]) * ATT_SCALE
        s_w = _dot_nt(q, ks[pl.ds(w0, 3 * A_BLOCK), :]) * ATT_SCALE
        kpos = (i - 1) * A_BLOCK + c
        s_w = jnp.where(band & (kpos >= 0) & (kpos < SEQ), s_w, NEG)
        o = _attend([s_c, s_w], [vs[0:CTX, :], vs[pl.ds(w0, 3 * A_BLOCK), :]], sink)
        o0 = pl.multiple_of(CTX + i * A_BLOCK, A_BLOCK)
        for g in range(A_GRP):
            o_ref[pl.ds(o0, A_BLOCK), g * HD:(g + 1) * HD] = o[g * A_BLOCK:(g + 1) * A_BLOCK].astype(BF16)
        return carry

    lax.fori_loop(0, NBLK, blk, 0, unroll=2)

    for g in range(A_GRP):
        s = _dot_nt(q_ref[0:CTX, g * HD:(g + 1) * HD], k_ref[0:CTX, :]) * ATT_SCALE
        o = _attend([s], [v_ref[0:CTX, :]], jnp.full((CTX, 1), sink_ref[kv * A_GRP + g], F32))
        o_ref[0:CTX, g * HD:(g + 1) * HD] = o.astype(BF16)


def window_attn(z, sink_l, cos_t, sin_t):
    return pl.pallas_call(
        _win_kernel,
        grid=(BATCH, A_KV),
        in_specs=[pl.BlockSpec(memory_space=pltpu.SMEM),
                  pl.BlockSpec((SEG, A_GRP * HD), lambda b, kv: (b, kv)),
                  pl.BlockSpec((SEG, HD), lambda b, kv: (b, COL_AK + kv)),
                  pl.BlockSpec((SEG, HD), lambda b, kv: (b, COL_AV + kv)),
                  pl.BlockSpec((SEQ, HD), lambda b, kv: (0, 0)),
                  pl.BlockSpec((SEQ, HD), lambda b, kv: (0, 0))],
        out_specs=pl.BlockSpec((SEG, A_GRP * HD), lambda b, kv: (b, kv)),
        out_shape=jax.ShapeDtypeStruct((T, A_HEADS * HD), BF16),
        scratch_shapes=[pltpu.VMEM((NBLK, A_GRP * A_BLOCK, HD), BF16),
                        pltpu.VMEM((KPAD, HD), BF16),
                        pltpu.VMEM((KPAD, HD), BF16)],
        name="window_attn",
        compiler_params=_cparams(("arbitrary", "arbitrary")),
    )(sink_l, z, z, z, cos_t, sin_t)


GRID_ROWS = SEQ // GRID_W
NBQ_ROWS = 4
NBK_ROWS = 12
NBQ, NBK = NBQ_ROWS * GRID_W, NBK_ROWS * GRID_W
N_NBLK = GRID_ROWS // NBQ_ROWS
NB_BASE_MAX = GRID_ROWS - NBK_ROWS


def _nbr_key_base(first_row):
    return jnp.clip(first_row - NB_ROWS // 2, 0, NB_BASE_MAX)


N_DR, N_DC = 2 * NB_ROWS - 1, 2 * NB_COLS - 1


def _nbr_block_offsets():
    out = []
    for first_row in (0, NBQ_ROWS, GRID_ROWS - NBQ_ROWS):
        base = min(max(first_row - NB_ROWS // 2, 0), NB_BASE_MAX)
        kind = []
        for qi in range(NBQ_ROWS):
            r = first_row + qi
            start = min(max(r - NB_ROWS // 2, 0), GRID_ROWS - NB_ROWS)
            kind.append([kr - r + NB_ROWS - 1 if start <= kr < start + NB_ROWS else N_DR
                         for kr in range(base, base + NBK_ROWS)])
        out.append(kind)
    return out


def _nbr_build_bias(rpb_ref, h, half, bias):
    qc = lax.broadcasted_iota(I32, (GRID_W, 2 * GRID_W), 0)
    lane = lax.broadcasted_iota(I32, (GRID_W, 2 * GRID_W), 1)
    kc = lane % GRID_W
    start_c = jnp.clip(qc - NB_COLS // 2, 0, GRID_W - NB_COLS)
    col_ok = (kc >= start_c) & (kc < start_c + NB_COLS)
    dc = kc - qc + NB_COLS - 1
    left = lane < GRID_W
    for d in range(N_DR):
        t = jnp.full((GRID_W, 2 * GRID_W), NEG, F32)
        for j in range(N_DC):
            t = jnp.where(dc == j, rpb_ref[(h * N_DR + d) * N_DC + j], t)
        t = jnp.where(col_ok, t, NEG)
        half[d, 0] = jnp.where(left, t, NEG)
        half[d, 1] = jnp.where(left, NEG, t)
    masked = jnp.full((GRID_W, 2 * GRID_W), NEG, F32)
    for kind, per_q in enumerate(_nbr_block_offsets()):
        for qi, dr in enumerate(per_q):
            for p in range(NBK_ROWS // 2):
                lo = half[dr[2 * p], 0] if dr[2 * p] < N_DR else masked
                hi = half[dr[2 * p + 1], 1] if dr[2 * p + 1] < N_DR else masked
                bias[kind, qi * GRID_W:(qi + 1) * GRID_W, p * 2 * GRID_W:(p + 1) * 2 * GRID_W] = jnp.maximum(lo, hi)


def _nbr_kernel(rpb_ref, q_ref, k_ref, v_ref, o_ref, half, bias):
    @pl.when(pl.program_id(1) == 0)
    def _():
        _nbr_build_bias(rpb_ref, pl.program_id(0), half, bias)

    def block(i):
        kind = jnp.where(i == 0, 0, jnp.where(i == N_NBLK - 1, 2, 1))
        q0 = pl.multiple_of(CTX + i * NBQ, NBQ)
        k0 = pl.multiple_of(CTX + _nbr_key_base(i * NBQ_ROWS) * GRID_W, GRID_W)
        q = q_ref[pl.ds(q0, NBQ), :]
        s_c = _dot_nt(q, k_ref[0:CTX, :]) * ATT_SCALE
        s_n = _dot_nt(q, k_ref[pl.ds(k0, NBK), :]) * ATT_SCALE + bias[kind]
        o = _attend([s_c, s_n], [v_ref[0:CTX, :], v_ref[pl.ds(k0, NBK), :]])
        o_ref[pl.ds(q0, NBQ), :] = o.astype(BF16)

    def blocks(i, carry):
        block(2 * i)
        block(2 * i + 1)
        return carry

    lax.fori_loop(0, N_NBLK // 2, blocks, 0)

    s = _dot_nt(q_ref[0:CTX, :], k_ref[0:CTX, :]) * ATT_SCALE
    o_ref[0:CTX, :] = _attend([s], [v_ref[0:CTX, :]]).astype(BF16)


def nbr_attn(z, rpb_l):
    return pl.pallas_call(
        _nbr_kernel,
        grid=(C_HEADS, BATCH),
        in_specs=[pl.BlockSpec(memory_space=pltpu.SMEM),
                  pl.BlockSpec((SEG, HD), lambda h, b: (b, COL_CQ + h)),
                  pl.BlockSpec((SEG, HD), lambda h, b: (b, COL_CK + h)),
                  pl.BlockSpec((SEG, HD), lambda h, b: (b, COL_CV + h))],
        out_specs=pl.BlockSpec((SEG, HD), lambda h, b: (b, h)),
        out_shape=jax.ShapeDtypeStruct((T, C_HEADS * HD), BF16),
        scratch_shapes=[pltpu.VMEM((N_DR, 2, GRID_W, 2 * GRID_W), F32), pltpu.VMEM((3, NBQ, NBK), F32)],
        name="nbr_attn",
        compiler_params=_cparams(("arbitrary", "arbitrary")),
    )(rpb_l.reshape(-1), z, z, z)


N_LCHUNK = SEQ // CHUNK
PADR = 8
CONV_ROWS = PADR + CTX + PADR + SEQ + PADR


def _conv_off(r0):
    return jnp.where(r0 < CTX, r0 + PADR, r0 + 2 * PADR)


def _mlstm_kernel(bg_ref, q_ref, k_ref, v_ref, og_ref, gr_ref, cwq_ref, cwk_ref, y_ref,
                  qs, ks, stage, hf, hb):
    h = pl.program_id(1)
    rowi = lax.broadcasted_iota(I32, (CHUNK, 1), 0)

    def conv_silu(src_ref, cw_ref, dst, post):
        stage[...] = jnp.zeros((CONV_ROWS, HD), F32)
        stage[PADR:PADR + CTX, :] = src_ref[0:CTX, :].astype(F32)
        stage[2 * PADR + CTX:2 * PADR + SEG, :] = src_ref[CTX:SEG, :].astype(F32)

        def blk(c, carry):
            r0 = pl.multiple_of(c * CHUNK, CHUNK)
            s0 = pl.multiple_of(_conv_off(r0), PADR)
            cur = stage[pl.ds(s0, CHUNK), :]
            prev_last = stage[pl.ds(s0 - PADR, PADR), :][PADR - 1:PADR, :]
            next_first = stage[pl.ds(s0 + CHUNK, PADR), :][0:1, :]
            zm = jnp.where(rowi == 0, prev_last, pltpu.roll(cur, 1, 0))
            zp = jnp.where(rowi == CHUNK - 1, next_first, pltpu.roll(cur, CHUNK - 1, 0))
            y = zm * cw_ref[0:1, :] + cur * cw_ref[1:2, :] + zp * cw_ref[2:3, :]
            dst[pl.ds(r0, CHUNK), :] = (_silu(y) * post).astype(BF16)
            return carry

        lax.fori_loop(0, SEG // CHUNK, blk, 0)

    conv_silu(q_ref, cwq_ref, qs, 1.0)
    conv_silu(k_ref, cwk_ref, ks, ATT_SCALE)

    t_idx = lax.broadcasted_iota(I32, (CHUNK, CHUNK), 0)
    s_idx = lax.broadcasted_iota(I32, (CHUNK, CHUNK), 1)

    def chunk(r0, bwd, state, dst):
        c_mat, n_vec, m_prev = state
        ki, kf = (2, 3) if bwd else (0, 1)
        bi = bg_ref[ki * B_HEADS + h]
        bf = bg_ref[kf * B_HEADS + h]
        qc = qs[pl.ds(r0, CHUNK), :]
        kc = ks[pl.ds(r0, CHUNK), :]
        vc = v_ref[pl.ds(r0, CHUNK), :]
        i_row = gr_ref[pl.ds(ki * B_HEADS + h, 1), pl.ds(r0, CHUNK)] + bi
        lf_row = _log_sigmoid(gr_ref[pl.ds(kf * B_HEADS + h, 1), pl.ds(r0, CHUNK)] + bf)
        causal = (s_idx >= t_idx) if bwd else (s_idx <= t_idx)
        diag = s_idx == t_idx
        b_col = jnp.sum(jnp.where(causal, lf_row, 0.0), axis=1, keepdims=True)
        b_row = jnp.sum(jnp.where(diag, b_col, 0.0), axis=0, keepdims=True)
        a_row = i_row - b_row
        a_col = jnp.sum(jnp.where(diag, a_row, 0.0), axis=1, keepdims=True)
        mx = jnp.maximum(m_prev, jnp.max(jnp.where(causal, a_row, NEG), axis=1, keepdims=True))
        dm = jnp.exp(jnp.where(causal, a_row - mx, NEG))
        s = _dot_nt(qc, kc) * dm
        w_int = jnp.exp(m_prev - mx)
        num = (jnp.dot(s.astype(BF16), vc, preferred_element_type=F32)
               + w_int * jnp.dot(qc, c_mat.astype(BF16), preferred_element_type=F32))
        den = (jnp.sum(s, axis=1, keepdims=True)
               + w_int * jnp.sum(qc.astype(F32) * n_vec, axis=1, keepdims=True))
        m_t = b_col + mx
        dst[pl.ds(r0, CHUNK), :] = num / jnp.maximum(jnp.abs(den), jnp.exp(-m_t))
        b_end = jnp.sum(lf_row, axis=1, keepdims=True)
        m_end = jnp.maximum(m_prev, jnp.max(a_row, axis=1, keepdims=True))
        decay = jnp.exp(m_prev - m_end)
        w_col = jnp.exp(a_col - m_end)
        kv = lax.dot_general(kc, (w_col * vc.astype(F32)).astype(BF16), (((0,), (0,)), ((), ())),
                             preferred_element_type=F32)
        c_new = decay * c_mat + kv
        n_new = decay * n_vec + jnp.sum(w_col * kc.astype(F32), axis=0, keepdims=True)
        return c_new, n_new, b_end + m_end

    zero = (jnp.zeros((HD, HD), F32), jnp.zeros((1, HD), F32), jnp.zeros((1, 1), F32))
    st_f = chunk(0, False, zero, hf)
    st_b = chunk(0, True, zero, hb)

    def body(j, carry):
        sf, sb = carry
        rf = pl.multiple_of(CTX + j * CHUNK, CHUNK)
        rb = pl.multiple_of(CTX + (N_LCHUNK - 1 - j) * CHUNK, CHUNK)
        return chunk(rf, False, sf, hf), chunk(rb, True, sb, hb)

    lax.fori_loop(0, N_LCHUNK, body, (st_f, st_b))

    def fin(c, carry):
        r0 = pl.multiple_of(c * CHUNK, CHUNK)
        gate = jax.nn.sigmoid(og_ref[pl.ds(r0, CHUNK), :].astype(F32))
        y_ref[pl.ds(r0, CHUNK), :] = (gate * (hf[pl.ds(r0, CHUNK), :] + hb[pl.ds(r0, CHUNK), :])).astype(BF16)
        return carry

    lax.fori_loop(0, SEG // CHUNK, fin, 0)


def mlstm(z, g_row, b_gates_l, conv_l):
    return pl.pallas_call(
        _mlstm_kernel,
        grid=(BATCH, B_HEADS),
        in_specs=[pl.BlockSpec(memory_space=pltpu.SMEM),
                  pl.BlockSpec((SEG, HD), lambda b, h: (b, COL_BQ + h)),
                  pl.BlockSpec((SEG, HD), lambda b, h: (b, COL_BK + h)),
                  pl.BlockSpec((SEG, HD), lambda b, h: (b, COL_BV + h)),
                  pl.BlockSpec((SEG, HD), lambda b, h: (b, COL_BO + h)),
                  pl.BlockSpec((4 * B_HEADS, SEG), lambda b, h: (0, b)),
                  pl.BlockSpec((3, HD), lambda b, h: (0, h)),
                  pl.BlockSpec((3, HD), lambda b, h: (0, B_HEADS + h))],
        out_specs=pl.BlockSpec((SEG, HD), lambda b, h: (b, h)),
        out_shape=jax.ShapeDtypeStruct((T, B_HEADS * HD), BF16),
        scratch_shapes=[pltpu.VMEM((SEG, HD), BF16), pltpu.VMEM((SEG, HD), BF16),
                        pltpu.VMEM((CONV_ROWS, HD), F32),
                        pltpu.VMEM((SEG, HD), F32), pltpu.VMEM((SEG, HD), F32)],
        name="mlstm",
        compiler_params=_cparams(("arbitrary", "arbitrary")),
    )(b_gates_l, z, z, z, z, g_row, conv_l, conv_l)


OUT_TM = SEG // 6


def _top2_routing(biased, scores, comb_ref, sel_ref):
    rows = [biased[e:e + 1, :] for e in range(N_EXPERTS)]
    gscore = []
    for g in range(N_GROUPS):
        v0, v1, v2, v3 = rows[4 * g:4 * g + 4]
        hi01, lo01 = jnp.maximum(v0, v1), jnp.minimum(v0, v1)
        hi23, lo23 = jnp.maximum(v2, v3), jnp.minimum(v2, v3)
        gscore.append(jnp.maximum(hi01, hi23) + jnp.maximum(jnp.minimum(hi01, hi23), jnp.maximum(lo01, lo23)))
    picked = []
    total = None
    for e in range(N_EXPERTS):
        g = e // PER_GROUP
        ok = None
        for o in range(N_GROUPS):
            if o != g:
                c = (gscore[g] > gscore[o]) if o < g else (gscore[g] >= gscore[o])
                ok = c if ok is None else (ok & c)
        ahead = jnp.zeros_like(rows[e])
        for o in range(4 * g, 4 * g + 4):
            if o != e:
                c = (rows[o] > rows[e]) if o > e else (rows[o] >= rows[e])
                ahead = ahead + jnp.where(c, 1.0, 0.0)
        sel = jnp.where(ok & (ahead < 2.0), 1.0, 0.0)
        sel_ref[e:e + 1, :] = sel
        picked.append(sel * scores[e:e + 1, :])
        total = picked[-1] if total is None else total + picked[-1]
    for e in range(N_EXPERTS):
        comb_ref[e:e + 1, :] = picked[e] / total


OUT_RC = 16


N_OUT_TILES = T // OUT_TM


def _out_kernel(ya_ref, yb_ref, yn_ref, w_ref, x_ref, mod_ref, g_ref, wrh_ref, wrl_ref, br_ref,
                x1_ref, h2_ref, comb_ref, sel_ref, ycat, acc, acc_next, h_hi, h_lo):
    i = pl.program_id(0)

    def matmul():
        ycat[:, 0:A_HEADS * HD] = ya_ref[...]
        ycat[:, A_HEADS * HD:(A_HEADS + B_HEADS) * HD] = yb_ref[...]
        ycat[:, (A_HEADS + B_HEADS) * HD:D] = yn_ref[...]
        acc_next[...] = jnp.dot(ycat[...], w_ref[...], preferred_element_type=F32)

    def epilogue():
        t = i - 1
        batch = t // 6
        for c in range(OUT_TM // OUT_RC):
            rows = slice(c * OUT_RC, (c + 1) * OUT_RC)
            mrow = jnp.where((t % 6) * OUT_TM + c * OUT_RC < CTX, 4, batch)

            def mod(k):
                return mod_ref[pl.ds(mrow, 1), k * D:(k + 1) * D]

            x1 = x_ref[rows, :] + mod(2) * acc[rows, :]
            x1_ref[rows, :] = x1
            h2 = (_rms(x1) * g_ref[...]) * (1.0 + mod(4)) + mod(3)
            h2_ref[rows, :] = h2
            hi = h2.astype(BF16)
            h_hi[rows, :] = hi
            h_lo[rows, :] = (h2 - hi.astype(F32)).astype(BF16)
        scores = jax.nn.sigmoid(_dot_nt(wrh_ref[...], h_hi[...]) + _dot_nt(wrl_ref[...], h_hi[...])
                                + _dot_nt(wrh_ref[...], h_lo[...]))
        _top2_routing(scores + br_ref[...], scores, comb_ref, sel_ref)

    @pl.when(i == 0)
    def _():
        matmul()

    @pl.when((i > 0) & (i < N_OUT_TILES))
    def _():
        matmul()
        epilogue()

    @pl.when(i == N_OUT_TILES)
    def _():
        epilogue()

    @pl.when(i < N_OUT_TILES)
    def _():
        acc[...] = acc_next[...]


def out_proj(ya, yb, yn, w_out_b, xa, mod_l, g, w_router_hi, w_router_lo, b_router_c, layer):
    cur = lambda i: (jnp.minimum(i, N_OUT_TILES - 1), 0)
    row = lambda i: (jnp.maximum(i - 1, 0), 0)
    fixed = lambda i: (0, 0)
    return pl.pallas_call(
        _out_kernel,
        grid=(N_OUT_TILES + 1,),
        in_specs=[pl.BlockSpec((OUT_TM, A_HEADS * HD), cur),
                  pl.BlockSpec((OUT_TM, B_HEADS * HD), cur),
                  pl.BlockSpec((OUT_TM, C_HEADS * HD), cur),
                  pl.BlockSpec((None, D, D), lambda i: (layer, 0, 0)),
                  pl.BlockSpec((OUT_TM, D), row),
                  pl.BlockSpec((8, 6 * D), fixed),
                  pl.BlockSpec((1, D), fixed),
                  pl.BlockSpec((N_EXPERTS, D), fixed),
                  pl.BlockSpec((N_EXPERTS, D), fixed),
                  pl.BlockSpec((N_EXPERTS, 1), fixed)],
        out_specs=[pl.BlockSpec((OUT_TM, D), row),
                   pl.BlockSpec((OUT_TM, D), row),
                   pl.BlockSpec((N_EXPERTS, OUT_TM), lambda i: (0, jnp.maximum(i - 1, 0))),
                   pl.BlockSpec((N_EXPERTS, OUT_TM), lambda i: (0, jnp.maximum(i - 1, 0)))],
        out_shape=[jax.ShapeDtypeStruct((T, D), F32),
                   jax.ShapeDtypeStruct((T, D), F32),
                   jax.ShapeDtypeStruct((N_EXPERTS, T), F32),
                   jax.ShapeDtypeStruct((N_EXPERTS, T), F32)],
        scratch_shapes=[pltpu.VMEM((OUT_TM, D), BF16), pltpu.VMEM((OUT_TM, D), F32), pltpu.VMEM((OUT_TM, D), F32),
                        pltpu.VMEM((OUT_TM, D), BF16), pltpu.VMEM((OUT_TM, D), BF16)],
        name="out_proj",
        compiler_params=_cparams(("arbitrary",)),
    )(ya, yb, yn, w_out_b, xa, mod_l, g, w_router_hi, w_router_lo, b_router_c)


SC_TM = 1024
ROW_DMA_UNROLL = 8


def _scatter_kernel(pos0_ref, pos1_ref, ztile_ref, h_ref, hs_ref, zero_scr, sem, zsem):
    i = pl.program_id(0)

    @pl.when(i == 0)
    def _():
        zero_scr[...] = jnp.zeros((ETILE, D), F32)

        def fill_copy(t):
            return pltpu.make_async_copy(zero_scr, hs_ref.at[pl.ds(pl.multiple_of(t * ETILE, ETILE), ETILE), :], zsem)

        def fill_start(e, c):
            @pl.when(ztile_ref[e] >= 0)
            def _():
                fill_copy(ztile_ref[e]).start()
            return c

        def fill_wait(e, c):
            @pl.when(ztile_ref[e] >= 0)
            def _():
                fill_copy(ztile_ref[e]).wait()
            return c

        lax.fori_loop(0, 2 * N_EXPERTS, fill_start, 0)
        lax.fori_loop(0, 2 * N_EXPERTS, fill_wait, 0)

    def row_copy(r, p):
        return pltpu.make_async_copy(h_ref.at[pl.ds(r, 1), :], hs_ref.at[pl.ds(p, 1), :], sem)

    def issue(g, c):
        r0 = pl.multiple_of(g * ROW_DMA_UNROLL, ROW_DMA_UNROLL)
        for u in range(ROW_DMA_UNROLL):
            t = i * SC_TM + r0 + u
            row_copy(r0 + u, pos0_ref[t]).start()
            row_copy(r0 + u, pos1_ref[t]).start()
        return c

    lax.fori_loop(0, SC_TM // ROW_DMA_UNROLL, issue, 0)

    def drain(r, c):
        row_copy(0, 0).wait()
        row_copy(0, 0).wait()
        return c

    lax.fori_loop(0, SC_TM, drain, 0, unroll=ROW_DMA_UNROLL)


def scatter_rows(pos0, pos1, ztile, h2):
    return pl.pallas_call(
        _scatter_kernel,
        grid_spec=pltpu.PrefetchScalarGridSpec(
            num_scalar_prefetch=3,
            grid=(T // SC_TM,),
            in_specs=[pl.BlockSpec((SC_TM, D), lambda i, *_: (i, 0))],
            out_specs=pl.BlockSpec(memory_space=pl.ANY),
            scratch_shapes=[pltpu.VMEM((ETILE, D), F32), pltpu.SemaphoreType.DMA, pltpu.SemaphoreType.DMA]),
        out_shape=jax.ShapeDtypeStruct((P_ROWS, D), F32),
        name="scatter_rows",
        compiler_params=_cparams(("arbitrary",)),
    )(pos0, pos1, ztile, h2)


CAST_ROWS = 256


def _cast_into(dst, src, n_rows):
    def body(c, carry):
        r0 = pl.multiple_of(c * CAST_ROWS, CAST_ROWS)
        dst[pl.ds(r0, CAST_ROWS), :] = src[pl.ds(r0, CAST_ROWS), :].astype(BF16)
        return carry

    lax.fori_loop(0, n_rows // CAST_ROWS, body, 0)


def _ffn_kernel(te_ref, tv_ref, first_ref, nxt_ref, hs_ref, w1_hbm, w3_hbm, w2_hbm, ys_ref,
                st1, st3, st2, w1b, w3b, w2b, sem, *, layer):
    j = pl.program_id(0)

    def fetch(e):
        return (pltpu.make_async_copy(w1_hbm.at[layer, e], st1, sem.at[0]),
                pltpu.make_async_copy(w3_hbm.at[layer, e], st3, sem.at[1]),
                pltpu.make_async_copy(w2_hbm.at[layer, e], st2, sem.at[2]))

    @pl.when(j == 0)
    def _():
        for cp in fetch(te_ref[0]):
            cp.start()

    @pl.when(first_ref[j] == 1)
    def _():
        for cp in fetch(te_ref[j]):
            cp.wait()
        _cast_into(w1b, st1, D)
        _cast_into(w3b, st3, D)
        _cast_into(w2b, st2, FF)

        @pl.when(nxt_ref[j] >= 0)
        def _():
            for cp in fetch(nxt_ref[j]):
                cp.start()

    def expert_rows(n):
        xb = hs_ref[0:n, :].astype(BF16)
        a = jnp.dot(xb, w1b[...], preferred_element_type=F32)
        b = jnp.dot(xb, w3b[...], preferred_element_type=F32)
        act = (_silu(a) * b).astype(BF16)
        ys_ref[0:n, :] = jnp.dot(act, w2b[...], preferred_element_type=F32)

    @pl.when(tv_ref[j] > ETILE // 2)
    def _():
        expert_rows(ETILE)

    @pl.when((tv_ref[j] > 0) & (tv_ref[j] <= ETILE // 2))
    def _():
        expert_rows(ETILE // 2)
        ys_ref[ETILE // 2:ETILE, :] = jnp.zeros((ETILE // 2, D), F32)

    @pl.when(tv_ref[j] == 0)
    def _():
        ys_ref[...] = jnp.zeros((ETILE, D), F32)


def _tile_or_first(j, te, tv, *_):
    return jnp.where(tv[j] > 0, j, 0)


def ffn(plan, hs, w1, w3, w2, layer):
    return pl.pallas_call(
        functools.partial(_ffn_kernel, layer=layer),
        grid_spec=pltpu.PrefetchScalarGridSpec(
            num_scalar_prefetch=4,
            grid=(N_ETILES,),
            in_specs=[pl.BlockSpec((ETILE, D), lambda j, *p: (_tile_or_first(j, *p), 0)),
                      pl.BlockSpec(memory_space=pl.ANY),
                      pl.BlockSpec(memory_space=pl.ANY),
                      pl.BlockSpec(memory_space=pl.ANY)],
            out_specs=pl.BlockSpec((ETILE, D), lambda j, *p: (j, 0)),
            scratch_shapes=[pltpu.VMEM((D, FF), F32), pltpu.VMEM((D, FF), F32), pltpu.VMEM((FF, D), F32),
                            pltpu.VMEM((D, FF), BF16), pltpu.VMEM((D, FF), BF16), pltpu.VMEM((FF, D), BF16),
                            pltpu.SemaphoreType.DMA((3,))]),
        out_shape=jax.ShapeDtypeStruct((P_ROWS, D), F32),
        name="ffn",
        compiler_params=_cparams(("arbitrary",)),
    )(*plan, hs, w1, w3, w2)


CB_TM = CTX
CB_PER_SEG = SEG // CB_TM


CB_RC = 32


def _combine_kernel(pos0_ref, pos1_ref, ys_ref, x_ref, w_ref, mod_ref, g_ref, o_ref, buf, sem, *, final):
    i = pl.program_id(0)
    n = pl.num_programs(0)
    slot = i % 2

    def row_copy(s, k, r, p):
        return pltpu.make_async_copy(ys_ref.at[pl.ds(p, 1), :], buf.at[s, k, pl.ds(r, 1), :], sem.at[s])

    def gather(tile, s):
        def issue(g, c):
            r0 = pl.multiple_of(g * ROW_DMA_UNROLL, ROW_DMA_UNROLL)
            for u in range(ROW_DMA_UNROLL):
                t = tile * CB_TM + r0 + u
                row_copy(s, 0, r0 + u, pos0_ref[t]).start()
                row_copy(s, 1, r0 + u, pos1_ref[t]).start()
            return c

        lax.fori_loop(0, CB_TM // ROW_DMA_UNROLL, issue, 0)

    @pl.when(i == 0)
    def _():
        gather(0, 0)

    @pl.when(i + 1 < n)
    def _():
        gather(i + 1, 1 - slot)

    def drain(r, c):
        row_copy(slot, 0, 0, 0).wait()
        row_copy(slot, 1, 0, 0).wait()
        return c

    lax.fori_loop(0, CB_TM, drain, 0, unroll=ROW_DMA_UNROLL)

    def chunk(c, carry):
        r0 = pl.multiple_of(c * CB_RC, CB_RC)
        rows = pl.ds(r0, CB_RC)
        is_ctx = ((i % CB_PER_SEG) == 0) & (r0 < CTX)
        gate = mod_ref[pl.ds(jnp.where(is_ctx, 4, i // CB_PER_SEG), 1), 5 * D:6 * D]
        f = w_ref[rows, 0:1] * buf[slot, 0, rows, :] + w_ref[rows, 1:2] * buf[slot, 1, rows, :]
        v = x_ref[rows, :] + gate * f
        o_ref[rows, :] = _rms(v) * g_ref[...] if final else v
        return carry

    lax.fori_loop(0, CB_TM // CB_RC, chunk, 0)


def combine_rows(pos0, pos1, ys, x1, w01, mod_l, g_final, final):
    if final:
        lat = SEQ // CB_TM
        out_map = lambda i, *_: ((i // CB_PER_SEG) * lat + jnp.maximum(i % CB_PER_SEG - 1, 0), 0)
        out_rows = BATCH * SEQ
    else:
        out_map = lambda i, *_: (i, 0)
        out_rows = T
    return pl.pallas_call(
        functools.partial(_combine_kernel, final=final),
        grid_spec=pltpu.PrefetchScalarGridSpec(
            num_scalar_prefetch=2,
            grid=(T // CB_TM,),
            in_specs=[pl.BlockSpec(memory_space=pl.ANY),
                      pl.BlockSpec((CB_TM, D), lambda i, *_: (i, 0)),
                      pl.BlockSpec((CB_TM, 2), lambda i, *_: (i, 0)),
                      pl.BlockSpec((8, 6 * D), lambda i, *_: (0, 0)),
                      pl.BlockSpec((1, D), lambda i, *_: (0, 0))],
            out_specs=pl.BlockSpec((CB_TM, D), out_map),
            scratch_shapes=[pltpu.VMEM((2, 2, CB_TM, D), F32), pltpu.SemaphoreType.DMA((2,))]),
        out_shape=jax.ShapeDtypeStruct((out_rows, D), F32),
        name="combine_rows",
        compiler_params=_cparams(("arbitrary",)),
    )(pos0, pos1, ys, x1, w01, mod_l, g_final)


def route_plan(comb_t, sel_t):
    sel = sel_t > 0.5
    cnt = jnp.sum(sel, axis=1).astype(I32)
    tiles = (cnt + ETILE - 1) // ETILE
    tend = jnp.cumsum(tiles)
    toff = tend - tiles
    rank = jnp.cumsum(sel.astype(I32), axis=1) - 1
    pos = toff[:, None] * ETILE + rank
    pos0 = jnp.min(jnp.where(sel, pos, P_ROWS), axis=0).astype(I32)
    pos1 = jnp.max(jnp.where(sel, pos, -1), axis=0).astype(I32)
    w0 = jnp.sum(jnp.where(sel & (pos == pos0[None]), comb_t, 0.0), axis=0)
    w1 = jnp.sum(jnp.where(sel & (pos == pos1[None]), comb_t, 0.0), axis=0)
    n_used = tend[-1]
    tidx = jnp.arange(N_ETILES, dtype=I32)
    te_raw = jnp.sum((tend[None, :] <= tidx[:, None]).astype(I32), axis=1)
    te_last = jnp.sum((tend <= n_used - 1).astype(I32))
    te = jnp.where(tidx < n_used, te_raw, te_last).astype(I32)
    te = jnp.minimum(te, N_EXPERTS - 1)
    tv = jnp.clip(cnt[te] - (tidx - toff[te]) * ETILE, 0, ETILE)
    tv = jnp.where(tidx < n_used, tv, 0).astype(I32)
    tail = n_used + jnp.arange(N_EXPERTS, dtype=I32)
    ztile = jnp.concatenate([jnp.where(tiles > 0, tend - 1, -1), jnp.where(tail < N_ETILES, tail, -1)]).astype(I32)
    used = tidx < n_used
    first = (used & ((tidx == 0) | (te != jnp.roll(te, 1)))).astype(I32)
    nxt_tile = tend[te]
    nxt = jnp.where(nxt_tile < n_used, te[jnp.minimum(nxt_tile, N_ETILES - 1)], -1).astype(I32)
    return pos0, pos1, jnp.stack([w0, w1], axis=1), (te, tv, first, nxt), ztile


def _cast_kernel(w_ref, o_ref):
    o_ref[...] = w_ref[...].astype(BF16)


def cast_w_out(w_out):
    blk = lambda l, r: (l, r, 0)
    return pl.pallas_call(
        _cast_kernel,
        grid=(DEPTH, 2),
        in_specs=[pl.BlockSpec((1, D // 2, D), blk)],
        out_specs=pl.BlockSpec((1, D // 2, D), blk),
        out_shape=jax.ShapeDtypeStruct((DEPTH, D, D), BF16),
        name="cast_w_out",
        compiler_params=_cparams(("arbitrary", "arbitrary")),
    )(w_out)


def _rope_tables():
    t = jnp.arange(SEQ, dtype=I32)
    row = (t // GRID_W).astype(F32)
    col = (t % GRID_W).astype(F32)
    n_freq = HD // 4
    inv_freq = ROPE_BASE ** (-jnp.arange(n_freq, dtype=F32) / n_freq)
    ar = row[:, None] * inv_freq[None, :]
    ac = col[:, None] * inv_freq[None, :]
    cos_t = jnp.concatenate([jnp.cos(ar), jnp.cos(ar), jnp.cos(ac), jnp.cos(ac)], axis=1)
    sin_t = jnp.concatenate([-jnp.sin(ar), jnp.sin(ar), -jnp.sin(ac), jnp.sin(ac)], axis=1)
    return cos_t, sin_t


GATE_LO, GATE_HI = 3328, 3344
D_IN = N_MAIN + 4 * B_HEADS
D_IN_PAD = N_MAIN + HD
RP_ROWS = 256


def _repack_kernel(w_ref, o_ref, g_ref):
    o_ref[0, :, 0:GATE_LO] = w_ref[0, :, 0:GATE_LO].astype(BF16)
    o_ref[0, :, GATE_LO:N_MAIN] = w_ref[0, :, GATE_HI:D_IN].astype(BF16)
    lane = lax.broadcasted_iota(I32, (RP_ROWS, HD), 1)
    g_ref[0] = jnp.where(lane < 4 * B_HEADS, w_ref[0, :, GATE_LO:GATE_LO + HD], 0.0).astype(BF16)


def repack_w_in(w_in):
    blk = lambda l, r: (l, r, 0)
    return pl.pallas_call(
        _repack_kernel,
        grid=(DEPTH, D // RP_ROWS),
        in_specs=[pl.BlockSpec((1, RP_ROWS, D_IN_PAD), blk)],
        out_specs=[pl.BlockSpec((1, RP_ROWS, N_MAIN), blk), pl.BlockSpec((1, RP_ROWS, HD), blk)],
        out_shape=[jax.ShapeDtypeStruct((DEPTH, D, N_MAIN), BF16), jax.ShapeDtypeStruct((DEPTH, D, HD), BF16)],
        name="repack_w_in",
        compiler_params=_cparams(("arbitrary", "arbitrary")),
    )(w_in)


def kernel(x, c, ctx, c_ctx, w_ada, b_ada, norm_mix, norm_ffn, w_in, b_gates, conv_qk, sink, rpb, w_out,
           w_router, b_router, w1, w3, w2, norm_final):
    xa = jnp.concatenate([ctx, x], axis=1).reshape(T, D)
    cond8 = jnp.concatenate([c, c_ctx[None], jnp.zeros((3, D), F32)], axis=0)
    mod = ada_all(cond8, w_ada, b_ada)
    cos_t, sin_t = _rope_tables()
    w_main, w_gate = repack_w_in(jnp.pad(w_in, ((0, 0), (0, 0), (0, D_IN_PAD - D_IN))).astype(BF16))
    w_out_b = cast_w_out(w_out)
    w_router_hi = w_router.T.astype(BF16)
    w_router_lo = (w_router.T - w_router_hi.astype(F32)).astype(BF16)
    b_router_c = b_router.reshape(N_EXPERTS, 1)

    for l in range(DEPTH):
        z, zg = in_proj(xa, norm_mix[l].reshape(1, D), mod[l], w_main, w_gate, l)
        g_row = zg[:, :4 * B_HEADS].T
        ya = window_attn(z, sink[l], cos_t, sin_t)
        yb = mlstm(z, g_row, b_gates[l], conv_qk[l])
        yn = nbr_attn(z, rpb[l])
        x1, h2, comb_t, sel_t = out_proj(ya, yb, yn, w_out_b, xa, mod[l], norm_ffn[l].reshape(1, D),
                                         w_router_hi, w_router_lo, b_router_c, l)
        pos0, pos1, w01, plan, ztile = route_plan(comb_t, sel_t)
        hs = scatter_rows(pos0, pos1, ztile, h2)
        ys = ffn(plan, hs, w1, w3, w2, l)
        xa = combine_rows(pos0, pos1, ys, x1, w01, mod[l], norm_final.reshape(1, D), l == DEPTH - 1)
    return xa.reshape(BATCH, SEQ, D)
```

```python
import functools

import jax
import jax.numpy as jnp
from jax import lax
from jax.experimental import pallas as pl
from jax.experimental.pallas import tpu as pltpu

F32 = jnp.float32
BF16 = jnp.bfloat16
I32 = jnp.int32

D = 2048
BATCH = 4
SEQ = 2048
CTX = 256
SEG = CTX + SEQ
T = BATCH * SEG
DEPTH = 4
GRID_W = 64
HD = 128
A_HEADS, A_KV, A_GRP = 6, 2, 3
B_HEADS = 4
C_HEADS = 6
A_BLOCK = 128
NB_ROWS, NB_COLS = 8, 16
CHUNK = 256
N_EXPERTS, N_GROUPS, PER_GROUP = 16, 4, 4
FF = 1024
EPS = 1e-6
ROPE_BASE = 10000.0
NEG = -1e30
ATT_SCALE = HD ** -0.5

N_MAIN = 5632
COL_AQ, COL_AK, COL_AV = 0, 6, 8
COL_BQ, COL_BK, COL_BV, COL_BO = 10, 14, 18, 22
COL_CQ, COL_CK, COL_CV = 26, 32, 38

ETILE = 256
N_ETILES = (2 * T) // ETILE + N_EXPERTS
P_ROWS = N_ETILES * ETILE

VMEM_LIMIT = 56 * 1024 * 1024


def _cparams(sem):
    return pltpu.CompilerParams(dimension_semantics=sem, vmem_limit_bytes=VMEM_LIMIT)


def _silu(v):
    return v * jax.nn.sigmoid(v)


def _log_sigmoid(v):
    return jnp.minimum(v, 0.0) - jnp.log1p(jnp.exp(-jnp.abs(v)))


ADA_TN = 1024


def _ada_kernel(s_ref, w_ref, b_ref, o_ref):
    s = _silu(s_ref[...]).astype(BF16)
    o_ref[0] = jnp.dot(s, w_ref[0].astype(BF16), preferred_element_type=F32) + b_ref[0]


def ada_all(cond8, w_ada, b_ada):
    n = w_ada.shape[-1]
    return pl.pallas_call(
        _ada_kernel,
        grid=(DEPTH, n // ADA_TN),
        in_specs=[pl.BlockSpec((8, D), lambda l, j: (0, 0)),
                  pl.BlockSpec((1, D, ADA_TN), lambda l, j: (l, 0, j)),
                  pl.BlockSpec((1, 1, ADA_TN), lambda l, j: (l, 0, j))],
        out_specs=pl.BlockSpec((1, 8, ADA_TN), lambda l, j: (l, 0, j)),
        out_shape=jax.ShapeDtypeStruct((DEPTH, 8, n), F32),
        name="ada_mod",
        compiler_params=_cparams(("arbitrary", "arbitrary")),
    )(cond8, w_ada, b_ada.reshape(DEPTH, 1, n))


def _row_mod(mod_ref, chunk, batch, is_ctx):
    lat = mod_ref[pl.ds(batch, 1), chunk * D:(chunk + 1) * D]
    ctx = mod_ref[4:5, chunk * D:(chunk + 1) * D]
    return jnp.where(is_ctx, ctx, lat)


def _rms(x):
    return x * lax.rsqrt(jnp.mean(x * x, axis=-1, keepdims=True) + EPS)


IN_TM = SEG // 2
IN_TN = 1408


def _in_kernel(x_ref, g_ref, mod_ref, w_ref, wg_ref, z_ref, zg_ref, h_scr):
    i = pl.program_id(0)
    j = pl.program_id(1)

    @pl.when(j == 0)
    def _():
        batch = i // 2
        rows = lax.broadcasted_iota(I32, (IN_TM, 1), 0) + (i % 2) * IN_TM
        is_ctx = rows < CTX
        xn = _rms(x_ref[...]) * g_ref[...]
        h = xn * (1.0 + _row_mod(mod_ref, 1, batch, is_ctx)) + _row_mod(mod_ref, 0, batch, is_ctx)
        hb = h.astype(BF16)
        h_scr[...] = hb
        zg_ref[...] = jnp.dot(hb, wg_ref[...], preferred_element_type=F32)

    z_ref[...] = jnp.dot(h_scr[...], w_ref[...], preferred_element_type=F32).astype(BF16)


def in_proj(xa, g, mod_l, w_main, w_gate, layer):
    return pl.pallas_call(
        _in_kernel,
        grid=(T // IN_TM, N_MAIN // IN_TN),
        in_specs=[pl.BlockSpec((IN_TM, D), lambda i, j: (i, 0)),
                  pl.BlockSpec((1, D), lambda i, j: (0, 0)),
                  pl.BlockSpec((8, 6 * D), lambda i, j: (0, 0)),
                  pl.BlockSpec((None, D, IN_TN), lambda i, j: (layer, 0, j)),
                  pl.BlockSpec((None, D, HD), lambda i, j: (layer, 0, 0))],
        out_specs=[pl.BlockSpec((IN_TM, IN_TN), lambda i, j: (i, j)),
                   pl.BlockSpec((IN_TM, HD), lambda i, j: (i, 0))],
        out_shape=[jax.ShapeDtypeStruct((T, N_MAIN), BF16),
                   jax.ShapeDtypeStruct((T, HD), F32)],
        scratch_shapes=[pltpu.VMEM((IN_TM, D), BF16)],
        name="in_proj",
        compiler_params=_cparams(("arbitrary", "arbitrary")),
    )(xa, g, mod_l, w_main, w_gate)


NBLK = SEQ // A_BLOCK
KPAD = SEG + A_BLOCK


def _dot_nt(a, b):
    return lax.dot_general(a, b, (((1,), (1,)), ((), ())), preferred_element_type=F32)


def _attend(scores, values, extra=None):
    m = scores[0].max(axis=-1, keepdims=True)
    for s in scores[1:]:
        m = jnp.maximum(m, s.max(axis=-1, keepdims=True))
    if extra is not None:
        m = jnp.maximum(m, extra)
    den = None if extra is None else jnp.exp(extra - m)
    out = None
    for s, v in zip(scores, values):
        e = jnp.exp(s - m)
        d = e.sum(axis=-1, keepdims=True)
        den = d if den is None else den + d
        o = jnp.dot(e.astype(BF16), v, preferred_element_type=F32)
        out = o if out is None else out + o
    return out / den


def _win_kernel(sink_ref, q_ref, k_ref, v_ref, cos_ref, sin_ref, o_ref, qs, ks, vs):
    kv = pl.program_id(1)
    lane = lax.broadcasted_iota(I32, (A_BLOCK, HD), 1)
    first_half = (lane % 64) < 32

    def rope(zf, cos, sin):
        zr = jnp.where(first_half, pltpu.roll(zf, 96, 1), pltpu.roll(zf, 32, 1))
        return zf * cos + zr * sin

    ks[0:CTX, :] = k_ref[0:CTX, :]
    ks[SEG:KPAD, :] = jnp.zeros((A_BLOCK, HD), BF16)
    vs[0:SEG, :] = v_ref[...]
    vs[SEG:KPAD, :] = jnp.zeros((A_BLOCK, HD), BF16)

    def rope_blk(i, c):
        r0 = pl.multiple_of(i * A_BLOCK, A_BLOCK)
        cos = cos_ref[pl.ds(r0, A_BLOCK), :]
        sin = sin_ref[pl.ds(r0, A_BLOCK), :]
        ks[pl.ds(CTX + r0, A_BLOCK), :] = rope(k_ref[pl.ds(CTX + r0, A_BLOCK), :].astype(F32), cos, sin).astype(BF16)
        for g in range(A_GRP):
            zf = q_ref[pl.ds(CTX + r0, A_BLOCK), g * HD:(g + 1) * HD].astype(F32)
            qs[i, g * A_BLOCK:(g + 1) * A_BLOCK, :] = rope(zf, cos, sin).astype(BF16)
        return c

    lax.fori_loop(0, NBLK, rope_blk, 0)

    nq = A_GRP * A_BLOCK
    row = lax.broadcasted_iota(I32, (nq, 1), 0)
    sink = jnp.where(row < A_BLOCK, sink_ref[kv * A_GRP],
                     jnp.where(row < 2 * A_BLOCK, sink_ref[kv * A_GRP + 1], sink_ref[kv * A_GRP + 2]))
    r = lax.broadcasted_iota(I32, (nq, 3 * A_BLOCK), 0) % A_BLOCK
    c = lax.broadcasted_iota(I32, (nq, 3 * A_BLOCK), 1)
    band = (c >= r) & (c <= r + 2 * A_BLOCK)

    def blk(i, carry):
        w0 = pl.multiple_of(CTX - A_BLOCK + i * A_BLOCK, A_BLOCK)
        q = qs[i]
        s_c = _dot_nt(q, ks[0:CTX, :]) * ATT_SCALE
        s_w = _dot_nt(q, ks[pl.ds(w0, 3 * A_BLOCK), :]) * ATT_SCALE
        kpos = (i - 1) * A_BLOCK + c
        s_w = jnp.where(band & (kpos >= 0) & (kpos < SEQ), s_w, NEG)
        o = _attend([s_c, s_w], [vs[0:CTX, :], vs[pl.ds(w0, 3 * A_BLOCK), :]], sink)
        o0 = pl.multiple_of(CTX + i * A_BLOCK, A_BLOCK)
        for g in range(A_GRP):
            o_ref[pl.ds(o0, A_BLOCK), g * HD:(g + 1) * HD] = o[g * A_BLOCK:(g + 1) * A_BLOCK].astype(BF16)
        return carry

    lax.fori_loop(0, NBLK, blk, 0, unroll=2)

    for g in range(A_GRP):
        s = _dot_nt(q_ref[0:CTX, g * HD:(g + 1) * HD], k_ref[0:CTX, :]) * ATT_SCALE
        o = _attend([s], [v_ref[0:CTX, :]], jnp.full((CTX, 1), sink_ref[kv * A_GRP + g], F32))
        o_ref[0:CTX, g * HD:(g + 1) * HD] = o.astype(BF16)


def window_attn(z, sink_l, cos_t, sin_t):
    return pl.pallas_call(
        _win_kernel,
        grid=(BATCH, A_KV),
        in_specs=[pl.BlockSpec(memory_space=pltpu.SMEM),
                  pl.BlockSpec((SEG, A_GRP * HD), lambda b, kv: (b, kv)),
                  pl.BlockSpec((SEG, HD), lambda b, kv: (b, COL_AK + kv)),
                  pl.BlockSpec((SEG, HD), lambda b, kv: (b, COL_AV + kv)),
                  pl.BlockSpec((SEQ, HD), lambda b, kv: (0, 0)),
                  pl.BlockSpec((SEQ, HD), lambda b, kv: (0, 0))],
        out_specs=pl.BlockSpec((SEG, A_GRP * HD), lambda b, kv: (b, kv)),
        out_shape=jax.ShapeDtypeStruct((T, A_HEADS * HD), BF16),
        scratch_shapes=[pltpu.VMEM((NBLK, A_GRP * A_BLOCK, HD), BF16),
                        pltpu.VMEM((KPAD, HD), BF16),
                        pltpu.VMEM((KPAD, HD), BF16)],
        name="window_attn",
        compiler_params=_cparams(("arbitrary", "arbitrary")),
    )(sink_l, z, z, z, cos_t, sin_t)


GRID_ROWS = SEQ // GRID_W
NBQ_ROWS = 4
NBK_ROWS = 12
NBQ, NBK = NBQ_ROWS * GRID_W, NBK_ROWS * GRID_W
N_NBLK = GRID_ROWS // NBQ_ROWS
NB_BASE_MAX = GRID_ROWS - NBK_ROWS


def _nbr_key_base(first_row):
    return jnp.clip(first_row - NB_ROWS // 2, 0, NB_BASE_MAX)


N_DR, N_DC = 2 * NB_ROWS - 1, 2 * NB_COLS - 1


def _nbr_block_offsets():
    out = []
    for first_row in (0, NBQ_ROWS, GRID_ROWS - NBQ_ROWS):
        base = min(max(first_row - NB_ROWS // 2, 0), NB_BASE_MAX)
        kind = []
        for qi in range(NBQ_ROWS):
            r = first_row + qi
            start = min(max(r - NB_ROWS // 2, 0), GRID_ROWS - NB_ROWS)
            kind.append([kr - r + NB_ROWS - 1 if start <= kr < start + NB_ROWS else N_DR
                         for kr in range(base, base + NBK_ROWS)])
        out.append(kind)
    return out


def _nbr_build_bias(rpb_ref, h, half, bias):
    qc = lax.broadcasted_iota(I32, (GRID_W, 2 * GRID_W), 0)
    lane = lax.broadcasted_iota(I32, (GRID_W, 2 * GRID_W), 1)
    kc = lane % GRID_W
    start_c = jnp.clip(qc - NB_COLS // 2, 0, GRID_W - NB_COLS)
    col_ok = (kc >= start_c) & (kc < start_c + NB_COLS)
    dc = kc - qc + NB_COLS - 1
    left = lane < GRID_W
    for d in range(N_DR):
        t = jnp.full((GRID_W, 2 * GRID_W), NEG, F32)
        for j in range(N_DC):
            t = jnp.where(dc == j, rpb_ref[(h * N_DR + d) * N_DC + j], t)
        t = jnp.where(col_ok, t, NEG)
        half[d, 0] = jnp.where(left, t, NEG)
        half[d, 1] = jnp.where(left, NEG, t)
    masked = jnp.full((GRID_W, 2 * GRID_W), NEG, F32)
    for kind, per_q in enumerate(_nbr_block_offsets()):
        for qi, dr in enumerate(per_q):
            for p in range(NBK_ROWS // 2):
                lo = half[dr[2 * p], 0] if dr[2 * p] < N_DR else masked
                hi = half[dr[2 * p + 1], 1] if dr[2 * p + 1] < N_DR else masked
                bias[kind, qi * GRID_W:(qi + 1) * GRID_W, p * 2 * GRID_W:(p + 1) * 2 * GRID_W] = jnp.maximum(lo, hi)


def _nbr_kernel(rpb_ref, q_ref, k_ref, v_ref, o_ref, half, bias):
    @pl.when(pl.program_id(1) == 0)
    def _():
        _nbr_build_bias(rpb_ref, pl.program_id(0), half, bias)

    def block(i):
        kind = jnp.where(i == 0, 0, jnp.where(i == N_NBLK - 1, 2, 1))
        q0 = pl.multiple_of(CTX + i * NBQ, NBQ)
        k0 = pl.multiple_of(CTX + _nbr_key_base(i * NBQ_ROWS) * GRID_W, GRID_W)
        q = q_ref[pl.ds(q0, NBQ), :]
        s_c = _dot_nt(q, k_ref[0:CTX, :]) * ATT_SCALE
        s_n = _dot_nt(q, k_ref[pl.ds(k0, NBK), :]) * ATT_SCALE + bias[kind]
        o = _attend([s_c, s_n], [v_ref[0:CTX, :], v_ref[pl.ds(k0, NBK), :]])
        o_ref[pl.ds(q0, NBQ), :] = o.astype(BF16)

    def blocks(i, carry):
        block(2 * i)
        block(2 * i + 1)
        return carry

    lax.fori_loop(0, N_NBLK // 2, blocks, 0)

    s = _dot_nt(q_ref[0:CTX, :], k_ref[0:CTX, :]) * ATT_SCALE
    o_ref[0:CTX, :] = _attend([s], [v_ref[0:CTX, :]]).astype(BF16)


def nbr_attn(z, rpb_l):
    return pl.pallas_call(
        _nbr_kernel,
        grid=(C_HEADS, BATCH),
        in_specs=[pl.BlockSpec(memory_space=pltpu.SMEM),
                  pl.BlockSpec((SEG, HD), lambda h, b: (b, COL_CQ + h)),
                  pl.BlockSpec((SEG, HD), lambda h, b: (b, COL_CK + h)),
                  pl.BlockSpec((SEG, HD), lambda h, b: (b, COL_CV + h))],
        out_specs=pl.BlockSpec((SEG, HD), lambda h, b: (b, h)),
        out_shape=jax.ShapeDtypeStruct((T, C_HEADS * HD), BF16),
        scratch_shapes=[pltpu.VMEM((N_DR, 2, GRID_W, 2 * GRID_W), F32), pltpu.VMEM((3, NBQ, NBK), F32)],
        name="nbr_attn",
        compiler_params=_cparams(("arbitrary", "arbitrary")),
    )(rpb_l.reshape(-1), z, z, z)


N_LCHUNK = SEQ // CHUNK
PADR = 8
CONV_ROWS = PADR + CTX + PADR + SEQ + PADR


def _conv_off(r0):
    return jnp.where(r0 < CTX, r0 + PADR, r0 + 2 * PADR)


def _mlstm_kernel(bg_ref, q_ref, k_ref, v_ref, og_ref, gr_ref, cwq_ref, cwk_ref, y_ref,
                  qs, ks, stage, hf, hb):
    h = pl.program_id(1)
    rowi = lax.broadcasted_iota(I32, (CHUNK, 1), 0)

    def conv_silu(src_ref, cw_ref, dst, post):
        stage[...] = jnp.zeros((CONV_ROWS, HD), F32)
        stage[PADR:PADR + CTX, :] = src_ref[0:CTX, :].astype(F32)
        stage[2 * PADR + CTX:2 * PADR + SEG, :] = src_ref[CTX:SEG, :].astype(F32)

        def blk(c, carry):
            r0 = pl.multiple_of(c * CHUNK, CHUNK)
            s0 = pl.multiple_of(_conv_off(r0), PADR)
            cur = stage[pl.ds(s0, CHUNK), :]
            prev_last = stage[pl.ds(s0 - PADR, PADR), :][PADR - 1:PADR, :]
            next_first = stage[pl.ds(s0 + CHUNK, PADR), :][0:1, :]
            zm = jnp.where(rowi == 0, prev_last, pltpu.roll(cur, 1, 0))
            zp = jnp.where(rowi == CHUNK - 1, next_first, pltpu.roll(cur, CHUNK - 1, 0))
            y = zm * cw_ref[0:1, :] + cur * cw_ref[1:2, :] + zp * cw_ref[2:3, :]
            dst[pl.ds(r0, CHUNK), :] = (_silu(y) * post).astype(BF16)
            return carry

        lax.fori_loop(0, SEG // CHUNK, blk, 0)

    conv_silu(q_ref, cwq_ref, qs, 1.0)
    conv_silu(k_ref, cwk_ref, ks, ATT_SCALE)

    t_idx = lax.broadcasted_iota(I32, (CHUNK, CHUNK), 0)
    s_idx = lax.broadcasted_iota(I32, (CHUNK, CHUNK), 1)

    def chunk(r0, bwd, state, dst):
        c_mat, n_vec, m_prev = state
        ki, kf = (2, 3) if bwd else (0, 1)
        bi = bg_ref[ki * B_HEADS + h]
        bf = bg_ref[kf * B_HEADS + h]
        qc = qs[pl.ds(r0, CHUNK), :]
        kc = ks[pl.ds(r0, CHUNK), :]
        vc = v_ref[pl.ds(r0, CHUNK), :]
        i_row = gr_ref[pl.ds(ki * B_HEADS + h, 1), pl.ds(r0, CHUNK)] + bi
        lf_row = _log_sigmoid(gr_ref[pl.ds(kf * B_HEADS + h, 1), pl.ds(r0, CHUNK)] + bf)
        causal = (s_idx >= t_idx) if bwd else (s_idx <= t_idx)
        diag = s_idx == t_idx
        b_col = jnp.sum(jnp.where(causal, lf_row, 0.0), axis=1, keepdims=True)
        b_row = jnp.sum(jnp.where(diag, b_col, 0.0), axis=0, keepdims=True)
        a_row = i_row - b_row
        a_col = jnp.sum(jnp.where(diag, a_row, 0.0), axis=1, keepdims=True)
        mx = jnp.maximum(m_prev, jnp.max(jnp.where(causal, a_row, NEG), axis=1, keepdims=True))
        dm = jnp.exp(jnp.where(causal, a_row - mx, NEG))
        s = _dot_nt(qc, kc) * dm
        w_int = jnp.exp(m_prev - mx)
        num = (jnp.dot(s.astype(BF16), vc, preferred_element_type=F32)
               + w_int * jnp.dot(qc, c_mat.astype(BF16), preferred_element_type=F32))
        den = (jnp.sum(s, axis=1, keepdims=True)
               + w_int * jnp.sum(qc.astype(F32) * n_vec, axis=1, keepdims=True))
        m_t = b_col + mx
        dst[pl.ds(r0, CHUNK), :] = num / jnp.maximum(jnp.abs(den), jnp.exp(-m_t))
        b_end = jnp.sum(lf_row, axis=1, keepdims=True)
        m_end = jnp.maximum(m_prev, jnp.max(a_row, axis=1, keepdims=True))
        decay = jnp.exp(m_prev - m_end)
        w_col = jnp.exp(a_col - m_end)
        kv = lax.dot_general(kc, (w_col * vc.astype(F32)).astype(BF16), (((0,), (0,)), ((), ())),
                             preferred_element_type=F32)
        c_new = decay * c_mat + kv
        n_new = decay * n_vec + jnp.sum(w_col * kc.astype(F32), axis=0, keepdims=True)
        return c_new, n_new, b_end + m_end

    zero = (jnp.zeros((HD, HD), F32), jnp.zeros((1, HD), F32), jnp.zeros((1, 1), F32))
    st_f = chunk(0, False, zero, hf)
    st_b = chunk(0, True, zero, hb)

    def body(j, carry):
        sf, sb = carry
        rf = pl.multiple_of(CTX + j * CHUNK, CHUNK)
        rb = pl.multiple_of(CTX + (N_LCHUNK - 1 - j) * CHUNK, CHUNK)
        return chunk(rf, False, sf, hf), chunk(rb, True, sb, hb)

    lax.fori_loop(0, N_LCHUNK, body, (st_f, st_b))

    def fin(c, carry):
        r0 = pl.multiple_of(c * CHUNK, CHUNK)
        gate = jax.nn.sigmoid(og_ref[pl.ds(r0, CHUNK), :].astype(F32))
        y_ref[pl.ds(r0, CHUNK), :] = (gate * (hf[pl.ds(r0, CHUNK), :] + hb[pl.ds(r0, CHUNK), :])).astype(BF16)
        return carry

    lax.fori_loop(0, SEG // CHUNK, fin, 0)


def mlstm(z, g_row, b_gates_l, conv_l):
    return pl.pallas_call(
        _mlstm_kernel,
        grid=(BATCH, B_HEADS),
        in_specs=[pl.BlockSpec(memory_space=pltpu.SMEM),
                  pl.BlockSpec((SEG, HD), lambda b, h: (b, COL_BQ + h)),
                  pl.BlockSpec((SEG, HD), lambda b, h: (b, COL_BK + h)),
                  pl.BlockSpec((SEG, HD), lambda b, h: (b, COL_BV + h)),
                  pl.BlockSpec((SEG, HD), lambda b, h: (b, COL_BO + h)),
                  pl.BlockSpec((4 * B_HEADS, SEG), lambda b, h: (0, b)),
                  pl.BlockSpec((3, HD), lambda b, h: (0, h)),
                  pl.BlockSpec((3, HD), lambda b, h: (0, B_HEADS + h))],
        out_specs=pl.BlockSpec((SEG, HD), lambda b, h: (b, h)),
        out_shape=jax.ShapeDtypeStruct((T, B_HEADS * HD), BF16),
        scratch_shapes=[pltpu.VMEM((SEG, HD), BF16), pltpu.VMEM((SEG, HD), BF16),
                        pltpu.VMEM((CONV_ROWS, HD), F32),
                        pltpu.VMEM((SEG, HD), F32), pltpu.VMEM((SEG, HD), F32)],
        name="mlstm",
        compiler_params=_cparams(("arbitrary", "arbitrary")),
    )(b_gates_l, z, z, z, z, g_row, conv_l, conv_l)


OUT_TM = SEG // 6


def _top2_routing(biased, scores, comb_ref, sel_ref):
    rows = [biased[e:e + 1, :] for e in range(N_EXPERTS)]
    gscore = []
    for g in range(N_GROUPS):
        v0, v1, v2, v3 = rows[4 * g:4 * g + 4]
        hi01, lo01 = jnp.maximum(v0, v1), jnp.minimum(v0, v1)
        hi23, lo23 = jnp.maximum(v2, v3), jnp.minimum(v2, v3)
        gscore.append(jnp.maximum(hi01, hi23) + jnp.maximum(jnp.minimum(hi01, hi23), jnp.maximum(lo01, lo23)))
    picked = []
    total = None
    for e in range(N_EXPERTS):
        g = e // PER_GROUP
        ok = None
        for o in range(N_GROUPS):
            if o != g:
                c = (gscore[g] > gscore[o]) if o < g else (gscore[g] >= gscore[o])
                ok = c if ok is None else (ok & c)
        ahead = jnp.zeros_like(rows[e])
        for o in range(4 * g, 4 * g + 4):
            if o != e:
                c = (rows[o] > rows[e]) if o > e else (rows[o] >= rows[e])
                ahead = ahead + jnp.where(c, 1.0, 0.0)
        sel = jnp.where(ok & (ahead < 2.0), 1.0, 0.0)
        sel_ref[e:e + 1, :] = sel
        picked.append(sel * scores[e:e + 1, :])
        total = picked[-1] if total is None else total + picked[-1]
    for e in range(N_EXPERTS):
        comb_ref[e:e + 1, :] = picked[e] / total


OUT_RC = 16


N_OUT_TILES = T // OUT_TM


def _out_kernel(ya_ref, yb_ref, yn_ref, w_ref, x_ref, mod_ref, g_ref, wrh_ref, wrl_ref, br_ref,
                x1_ref, h2_ref, comb_ref, sel_ref, ycat, acc, acc_next, h_hi, h_lo):
    i = pl.program_id(0)

    def matmul():
        ycat[:, 0:A_HEADS * HD] = ya_ref[...]
        ycat[:, A_HEADS * HD:(A_HEADS + B_HEADS) * HD] = yb_ref[...]
        ycat[:, (A_HEADS + B_HEADS) * HD:D] = yn_ref[...]
        acc_next[...] = jnp.dot(ycat[...], w_ref[...], preferred_element_type=F32)

    def epilogue():
        t = i - 1
        batch = t // 6
        for c in range(OUT_TM // OUT_RC):
            rows = slice(c * OUT_RC, (c + 1) * OUT_RC)
            mrow = jnp.where((t % 6) * OUT_TM + c * OUT_RC < CTX, 4, batch)

            def mod(k):
                return mod_ref[pl.ds(mrow, 1), k * D:(k + 1) * D]

            x1 = x_ref[rows, :] + mod(2) * acc[rows, :]
            x1_ref[rows, :] = x1
            h2 = (_rms(x1) * g_ref[...]) * (1.0 + mod(4)) + mod(3)
            h2_ref[rows, :] = h2
            hi = h2.astype(BF16)
            h_hi[rows, :] = hi
            h_lo[rows, :] = (h2 - hi.astype(F32)).astype(BF16)
        scores = jax.nn.sigmoid(_dot_nt(wrh_ref[...], h_hi[...]) + _dot_nt(wrl_ref[...], h_hi[...])
                                + _dot_nt(wrh_ref[...], h_lo[...]))
        _top2_routing(scores + br_ref[...], scores, comb_ref, sel_ref)

    @pl.when(i == 0)
    def _():
        matmul()

    @pl.when((i > 0) & (i < N_OUT_TILES))
    def _():
        matmul()
        epilogue()

    @pl.when(i == N_OUT_TILES)
    def _():
        epilogue()

    @pl.when(i < N_OUT_TILES)
    def _():
        acc[...] = acc_next[...]


def out_proj(ya, yb, yn, w_out_b, xa, mod_l, g, w_router_hi, w_router_lo, b_router_c, layer):
    cur = lambda i: (jnp.minimum(i, N_OUT_TILES - 1), 0)
    row = lambda i: (jnp.maximum(i - 1, 0), 0)
    fixed = lambda i: (0, 0)
    return pl.pallas_call(
        _out_kernel,
        grid=(N_OUT_TILES + 1,),
        in_specs=[pl.BlockSpec((OUT_TM, A_HEADS * HD), cur),
                  pl.BlockSpec((OUT_TM, B_HEADS * HD), cur),
                  pl.BlockSpec((OUT_TM, C_HEADS * HD), cur),
                  pl.BlockSpec((None, D, D), lambda i: (layer, 0, 0)),
                  pl.BlockSpec((OUT_TM, D), row),
                  pl.BlockSpec((8, 6 * D), fixed),
                  pl.BlockSpec((1, D), fixed),
                  pl.BlockSpec((N_EXPERTS, D), fixed),
                  pl.BlockSpec((N_EXPERTS, D), fixed),
                  pl.BlockSpec((N_EXPERTS, 1), fixed)],
        out_specs=[pl.BlockSpec((OUT_TM, D), row),
                   pl.BlockSpec((OUT_TM, D), row),
                   pl.BlockSpec((N_EXPERTS, OUT_TM), lambda i: (0, jnp.maximum(i - 1, 0))),
                   pl.BlockSpec((N_EXPERTS, OUT_TM), lambda i: (0, jnp.maximum(i - 1, 0)))],
        out_shape=[jax.ShapeDtypeStruct((T, D), F32),
                   jax.ShapeDtypeStruct((T, D), F32),
                   jax.ShapeDtypeStruct((N_EXPERTS, T), F32),
                   jax.ShapeDtypeStruct((N_EXPERTS, T), F32)],
        scratch_shapes=[pltpu.VMEM((OUT_TM, D), BF16), pltpu.VMEM((OUT_TM, D), F32), pltpu.VMEM((OUT_TM, D), F32),
                        pltpu.VMEM((OUT_TM, D), BF16), pltpu.VMEM((OUT_TM, D), BF16)],
        name="out_proj",
        compiler_params=_cparams(("arbitrary",)),
    )(ya, yb, yn, w_out_b, xa, mod_l, g, w_router_hi, w_router_lo, b_router_c)


SC_TM = 1024
ROW_DMA_UNROLL = 8


def _scatter_kernel(pos0_ref, pos1_ref, ztile_ref, h_ref, hs_ref, zero_scr, sem, zsem):
    i = pl.program_id(0)

    @pl.when(i == 0)
    def _():
        zero_scr[...] = jnp.zeros((ETILE, D), F32)

        def fill_copy(t):
            return pltpu.make_async_copy(zero_scr, hs_ref.at[pl.ds(pl.multiple_of(t * ETILE, ETILE), ETILE), :], zsem)

        def fill_start(e, c):
            @pl.when(ztile_ref[e] >= 0)
            def _():
                fill_copy(ztile_ref[e]).start()
            return c

        def fill_wait(e, c):
            @pl.when(ztile_ref[e] >= 0)
            def _():
                fill_copy(ztile_ref[e]).wait()
            return c

        lax.fori_loop(0, 2 * N_EXPERTS, fill_start, 0)
        lax.fori_loop(0, 2 * N_EXPERTS, fill_wait, 0)

    def row_copy(r, p):
        return pltpu.make_async_copy(h_ref.at[pl.ds(r, 1), :], hs_ref.at[pl.ds(p, 1), :], sem)

    def issue(g, c):
        r0 = pl.multiple_of(g * ROW_DMA_UNROLL, ROW_DMA_UNROLL)
        for u in range(ROW_DMA_UNROLL):
            t = i * SC_TM + r0 + u
            row_copy(r0 + u, pos0_ref[t]).start()
            row_copy(r0 + u, pos1_ref[t]).start()
        return c

    lax.fori_loop(0, SC_TM // ROW_DMA_UNROLL, issue, 0)

    def drain(r, c):
        row_copy(0, 0).wait()
        row_copy(0, 0).wait()
        return c

    lax.fori_loop(0, SC_TM, drain, 0, unroll=ROW_DMA_UNROLL)


def scatter_rows(pos0, pos1, ztile, h2):
    return pl.pallas_call(
        _scatter_kernel,
        grid_spec=pltpu.PrefetchScalarGridSpec(
            num_scalar_prefetch=3,
            grid=(T // SC_TM,),
            in_specs=[pl.BlockSpec((SC_TM, D), lambda i, *_: (i, 0))],
            out_specs=pl.BlockSpec(memory_space=pl.ANY),
            scratch_shapes=[pltpu.VMEM((ETILE, D), F32), pltpu.SemaphoreType.DMA, pltpu.SemaphoreType.DMA]),
        out_shape=jax.ShapeDtypeStruct((P_ROWS, D), F32),
        name="scatter_rows",
        compiler_params=_cparams(("arbitrary",)),
    )(pos0, pos1, ztile, h2)


CAST_ROWS = 256


def _cast_into(dst, src, n_rows):
    def body(c, carry):
        r0 = pl.multiple_of(c * CAST_ROWS, CAST_ROWS)
        dst[pl.ds(r0, CAST_ROWS), :] = src[pl.ds(r0, CAST_ROWS), :].astype(BF16)
        return carry

    lax.fori_loop(0, n_rows // CAST_ROWS, body, 0)


def _ffn_kernel(te_ref, tv_ref, first_ref, nxt_ref, src_ref, h2_hbm, w1_hbm, w3_hbm, w2_hbm, ys_ref,
                st1, st3, st2, w1b, w3b, w2b, xbuf, xcur, sem, gsem, *, layer):
    j = pl.program_id(0)

    def fetch(e):
        return (pltpu.make_async_copy(w1_hbm.at[layer, e], st1, sem.at[0]),
                pltpu.make_async_copy(w3_hbm.at[layer, e], st3, sem.at[1]),
                pltpu.make_async_copy(w2_hbm.at[layer, e], st2, sem.at[2]))

    @pl.when(j == 0)
    def _():
        for cp in fetch(te_ref[0]):
            cp.start()

    @pl.when(first_ref[j] == 1)
    def _():
        for cp in fetch(te_ref[j]):
            cp.wait()
        _cast_into(w1b, st1, D)
        _cast_into(w3b, st3, D)
        _cast_into(w2b, st2, FF)

        @pl.when(nxt_ref[j] >= 0)
        def _():
            for cp in fetch(nxt_ref[j]):
                cp.start()

    last = pl.num_programs(0) - 1

    def row_copy(r, t):
        return pltpu.make_async_copy(h2_hbm.at[pl.ds(t, 1), :], xbuf.at[pl.ds(r, 1), :], gsem)

    def gather_next():
        for r in range(ETILE):
            row_copy(r, src_ref[(j + 1) * ETILE + r]).start()

    def drain():
        def body(r, c):
            row_copy(0, 0).wait()
            return c

        lax.fori_loop(0, ETILE, body, 0, unroll=ROW_DMA_UNROLL)

    @pl.when(j == 0)
    def _():
        def issue(r, c):
            row_copy(r, src_ref[r]).start()
            return c

        lax.fori_loop(0, ETILE, issue, 0, unroll=ROW_DMA_UNROLL)

    @pl.when((j == 0) | (tv_ref[jnp.maximum(j - 1, 0)] > 0))
    def _():
        drain()
        xcur[...] = xbuf[...].astype(BF16)

    def expert_rows(n):
        xb = xcur[0:n, :]
        a = jnp.dot(xb, w1b[...], preferred_element_type=F32)
        b = jnp.dot(xb, w3b[...], preferred_element_type=F32)
        act = (_silu(a) * b).astype(BF16)
        ys_ref[0:n, :] = jnp.dot(act, w2b[...], preferred_element_type=F32)

    @pl.when(tv_ref[j] > ETILE // 2)
    def _():
        gather_next()
        expert_rows(ETILE)

    @pl.when((tv_ref[j] > 0) & (tv_ref[j] <= ETILE // 2))
    def _():
        gather_next()
        expert_rows(ETILE // 2)
        ys_ref[ETILE // 2:ETILE, :] = jnp.zeros((ETILE // 2, D), F32)

    @pl.when(tv_ref[j] == 0)
    def _():
        ys_ref[...] = jnp.zeros((ETILE, D), F32)

    @pl.when((j == last) & (tv_ref[j] > 0))
    def _():
        drain()


def _tile_or_first(j, te, tv, *_):
    return jnp.where(tv[j] > 0, j, 0)


def ffn(plan, src, h2, w1, w3, w2, layer):
    return pl.pallas_call(
        functools.partial(_ffn_kernel, layer=layer),
        grid_spec=pltpu.PrefetchScalarGridSpec(
            num_scalar_prefetch=5,
            grid=(N_ETILES,),
            in_specs=[pl.BlockSpec(memory_space=pl.ANY),
                      pl.BlockSpec(memory_space=pl.ANY),
                      pl.BlockSpec(memory_space=pl.ANY),
                      pl.BlockSpec(memory_space=pl.ANY)],
            out_specs=pl.BlockSpec((ETILE, D), lambda j, *p: (j, 0)),
            scratch_shapes=[pltpu.VMEM((D, FF), F32), pltpu.VMEM((D, FF), F32), pltpu.VMEM((FF, D), F32),
                            pltpu.VMEM((D, FF), BF16), pltpu.VMEM((D, FF), BF16), pltpu.VMEM((FF, D), BF16),
                            pltpu.VMEM((ETILE, D), F32), pltpu.VMEM((ETILE, D), BF16),
                            pltpu.SemaphoreType.DMA((3,)), pltpu.SemaphoreType.DMA]),
        out_shape=jax.ShapeDtypeStruct((P_ROWS, D), F32),
        name="ffn",
        compiler_params=_cparams(("arbitrary",)),
    )(*plan, src, h2, w1, w3, w2)


CB_TM = CTX
CB_PER_SEG = SEG // CB_TM


CB_RC = 32


def _combine_kernel(pos0_ref, pos1_ref, ys_ref, x_ref, w_ref, mod_ref, g_ref, o_ref, buf, sem, *, final):
    i = pl.program_id(0)
    n = pl.num_programs(0)
    slot = i % 2

    def row_copy(s, k, r, p):
        return pltpu.make_async_copy(ys_ref.at[pl.ds(p, 1), :], buf.at[s, k, pl.ds(r, 1), :], sem.at[s])

    def gather(tile, s):
        def issue(g, c):
            r0 = pl.multiple_of(g * ROW_DMA_UNROLL, ROW_DMA_UNROLL)
            for u in range(ROW_DMA_UNROLL):
                t = tile * CB_TM + r0 + u
                row_copy(s, 0, r0 + u, pos0_ref[t]).start()
                row_copy(s, 1, r0 + u, pos1_ref[t]).start()
            return c

        lax.fori_loop(0, CB_TM // ROW_DMA_UNROLL, issue, 0)

    @pl.when(i == 0)
    def _():
        gather(0, 0)

    @pl.when(i + 1 < n)
    def _():
        gather(i + 1, 1 - slot)

    def drain(r, c):
        row_copy(slot, 0, 0, 0).wait()
        row_copy(slot, 1, 0, 0).wait()
        return c

    lax.fori_loop(0, CB_TM, drain, 0, unroll=ROW_DMA_UNROLL)

    def chunk(c, carry):
        r0 = pl.multiple_of(c * CB_RC, CB_RC)
        rows = pl.ds(r0, CB_RC)
        is_ctx = ((i % CB_PER_SEG) == 0) & (r0 < CTX)
        gate = mod_ref[pl.ds(jnp.where(is_ctx, 4, i // CB_PER_SEG), 1), 5 * D:6 * D]
        f = w_ref[rows, 0:1] * buf[slot, 0, rows, :] + w_ref[rows, 1:2] * buf[slot, 1, rows, :]
        v = x_ref[rows, :] + gate * f
        o_ref[rows, :] = _rms(v) * g_ref[...] if final else v
        return carry

    lax.fori_loop(0, CB_TM // CB_RC, chunk, 0)


def combine_rows(pos0, pos1, ys, x1, w01, mod_l, g_final, final):
    if final:
        lat = SEQ // CB_TM
        out_map = lambda i, *_: ((i // CB_PER_SEG) * lat + jnp.maximum(i % CB_PER_SEG - 1, 0), 0)
        out_rows = BATCH * SEQ
    else:
        out_map = lambda i, *_: (i, 0)
        out_rows = T
    return pl.pallas_call(
        functools.partial(_combine_kernel, final=final),
        grid_spec=pltpu.PrefetchScalarGridSpec(
            num_scalar_prefetch=2,
            grid=(T // CB_TM,),
            in_specs=[pl.BlockSpec(memory_space=pl.ANY),
                      pl.BlockSpec((CB_TM, D), lambda i, *_: (i, 0)),
                      pl.BlockSpec((CB_TM, 2), lambda i, *_: (i, 0)),
                      pl.BlockSpec((8, 6 * D), lambda i, *_: (0, 0)),
                      pl.BlockSpec((1, D), lambda i, *_: (0, 0))],
            out_specs=pl.BlockSpec((CB_TM, D), out_map),
            scratch_shapes=[pltpu.VMEM((2, 2, CB_TM, D), F32), pltpu.SemaphoreType.DMA((2,))]),
        out_shape=jax.ShapeDtypeStruct((out_rows, D), F32),
        name="combine_rows",
        compiler_params=_cparams(("arbitrary",)),
    )(pos0, pos1, ys, x1, w01, mod_l, g_final)


def route_plan(comb_t, sel_t):
    sel = sel_t > 0.5
    cnt = jnp.sum(sel, axis=1).astype(I32)
    tiles = (cnt + ETILE - 1) // ETILE
    tend = jnp.cumsum(tiles)
    toff = tend - tiles
    rank = jnp.cumsum(sel.astype(I32), axis=1) - 1
    pos = toff[:, None] * ETILE + rank
    pos0 = jnp.min(jnp.where(sel, pos, P_ROWS), axis=0).astype(I32)
    pos1 = jnp.max(jnp.where(sel, pos, -1), axis=0).astype(I32)
    w0 = jnp.sum(jnp.where(sel & (pos == pos0[None]), comb_t, 0.0), axis=0)
    w1 = jnp.sum(jnp.where(sel & (pos == pos1[None]), comb_t, 0.0), axis=0)
    n_used = tend[-1]
    tidx = jnp.arange(N_ETILES, dtype=I32)
    te_raw = jnp.sum((tend[None, :] <= tidx[:, None]).astype(I32), axis=1)
    te_last = jnp.sum((tend <= n_used - 1).astype(I32))
    te = jnp.where(tidx < n_used, te_raw, te_last).astype(I32)
    te = jnp.minimum(te, N_EXPERTS - 1)
    tv = jnp.clip(cnt[te] - (tidx - toff[te]) * ETILE, 0, ETILE)
    tv = jnp.where(tidx < n_used, tv, 0).astype(I32)
    tail = n_used + jnp.arange(N_EXPERTS, dtype=I32)
    ztile = jnp.concatenate([jnp.where(tiles > 0, tend - 1, -1), jnp.where(tail < N_ETILES, tail, -1)]).astype(I32)
    used = tidx < n_used
    first = (used & ((tidx == 0) | (te != jnp.roll(te, 1)))).astype(I32)
    nxt_tile = tend[te]
    nxt = jnp.where(nxt_tile < n_used, te[jnp.minimum(nxt_tile, N_ETILES - 1)], -1).astype(I32)
    tok = jnp.arange(T, dtype=I32)
    src = jnp.zeros((P_ROWS + ETILE,), I32).at[pos0].set(tok).at[pos1].set(tok)
    return pos0, pos1, jnp.stack([w0, w1], axis=1), (te, tv, first, nxt), src


def _rope_tables():
    t = jnp.arange(SEQ, dtype=I32)
    row = (t // GRID_W).astype(F32)
    col = (t % GRID_W).astype(F32)
    n_freq = HD // 4
    inv_freq = ROPE_BASE ** (-jnp.arange(n_freq, dtype=F32) / n_freq)
    ar = row[:, None] * inv_freq[None, :]
    ac = col[:, None] * inv_freq[None, :]
    cos_t = jnp.concatenate([jnp.cos(ar), jnp.cos(ar), jnp.cos(ac), jnp.cos(ac)], axis=1)
    sin_t = jnp.concatenate([-jnp.sin(ar), jnp.sin(ar), -jnp.sin(ac), jnp.sin(ac)], axis=1)
    return cos_t, sin_t


GATE_LO, GATE_HI = 3328, 3344
D_IN = N_MAIN + 4 * B_HEADS
D_IN_PAD = N_MAIN + HD
RP_ROWS = 256


def _repack_kernel(w_ref, o_ref, g_ref):
    o_ref[0, :, 0:GATE_LO] = w_ref[0, :, 0:GATE_LO].astype(BF16)
    o_ref[0, :, GATE_LO:N_MAIN] = w_ref[0, :, GATE_HI:D_IN].astype(BF16)
    lane = lax.broadcasted_iota(I32, (RP_ROWS, HD), 1)
    g_ref[0] = jnp.where(lane < 4 * B_HEADS, w_ref[0, :, GATE_LO:GATE_LO + HD], 0.0).astype(BF16)


def repack_w_in(w_in):
    blk = lambda l, r: (l, r, 0)
    return pl.pallas_call(
        _repack_kernel,
        grid=(DEPTH, D // RP_ROWS),
        in_specs=[pl.BlockSpec((1, RP_ROWS, D_IN_PAD), blk)],
        out_specs=[pl.BlockSpec((1, RP_ROWS, N_MAIN), blk), pl.BlockSpec((1, RP_ROWS, HD), blk)],
        out_shape=[jax.ShapeDtypeStruct((DEPTH, D, N_MAIN), BF16), jax.ShapeDtypeStruct((DEPTH, D, HD), BF16)],
        name="repack_w_in",
        compiler_params=_cparams(("arbitrary", "arbitrary")),
    )(w_in)


def kernel(x, c, ctx, c_ctx, w_ada, b_ada, norm_mix, norm_ffn, w_in, b_gates, conv_qk, sink, rpb, w_out,
           w_router, b_router, w1, w3, w2, norm_final):
    xa = jnp.concatenate([ctx, x], axis=1).reshape(T, D)
    cond8 = jnp.concatenate([c, c_ctx[None], jnp.zeros((3, D), F32)], axis=0)
    mod = ada_all(cond8, w_ada, b_ada)
    cos_t, sin_t = _rope_tables()
    w_main, w_gate = repack_w_in(jnp.pad(w_in, ((0, 0), (0, 0), (0, D_IN_PAD - D_IN))).astype(BF16))
    w_out_b = w_out.astype(BF16)
    w_router_hi = w_router.T.astype(BF16)
    w_router_lo = (w_router.T - w_router_hi.astype(F32)).astype(BF16)
    b_router_c = b_router.reshape(N_EXPERTS, 1)

    for l in range(DEPTH):
        z, zg = in_proj(xa, norm_mix[l].reshape(1, D), mod[l], w_main, w_gate, l)
        g_row = zg[:, :4 * B_HEADS].T
        ya = window_attn(z, sink[l], cos_t, sin_t)
        yb = mlstm(z, g_row, b_gates[l], conv_qk[l])
        yn = nbr_attn(z, rpb[l])
        x1, h2, comb_t, sel_t = out_proj(ya, yb, yn, w_out_b, xa, mod[l], norm_ffn[l].reshape(1, D),
                                         w_router_hi, w_router_lo, b_router_c, l)
        pos0, pos1, w01, plan, src = route_plan(comb_t, sel_t)
        ys = ffn(plan, src, h2, w1, w3, w2, l)
        xa = combine_rows(pos0, pos1, ys, x1, w01, mod[l], norm_final.reshape(1, D), l == DEPTH - 1)
    return xa.reshape(BATCH, SEQ, D)
```
